```python
import math
import jax, jax.numpy as jnp
from jax import lax
import numpy as np

D_MODEL = 1024
BATCH = 16
SEQ = 2048
DEPTH = 1

RET_HEADS = 4
RET_DK = 128
RET_DV = 256
RET_CHUNK = 128
ROPE_BASE = 10000.0
ATT_HEADS = 8
ATT_DH = 64
DILATED_PATTERNS = ((128, 1), (512, 4), (2048, 16))
NUM_BUCKETS = 32
MAX_DISTANCE = 1024
N_GROUPS = 4
EXPERTS_PER_GROUP = 4
N_EXPERTS = N_GROUPS * EXPERTS_PER_GROUP
TOP_K = 2
D_EXPERT = 512
MOE_BLOCK = 128
DEEPNORM_ALPHA = (2.0 * DEPTH) ** 0.25
DEEPNORM_BETA = (8.0 * DEPTH) ** -0.25
LN_EPS = 1e-5
NEG_INF = -1e30

RET_QK_W = RET_HEADS * RET_DK
RET_V_W = RET_HEADS * RET_DV
ATT_W = ATT_HEADS * ATT_DH
IN_SPLITS = (RET_QK_W, RET_QK_W, RET_V_W, RET_V_W, ATT_W, ATT_W, ATT_W, D_MODEL, D_MODEL)
IN_OFFSETS = tuple(int(v) for v in np.cumsum((0,) + IN_SPLITS))
IN_COLS = IN_OFFSETS[-1]
SPLIT_POINTS = list(IN_OFFSETS[1:-1])

kernel_name = 'hybrid_retention_dilated_hmoe_block'


def layer_norm(x, g, b):
    xf = x.astype(jnp.float32)
    mu = jnp.mean(xf, axis=-1, keepdims=True)
    var = jnp.mean(jnp.square(xf - mu), axis=-1, keepdims=True)
    y = (xf - mu) * lax.rsqrt(var + LN_EPS) * g.astype(jnp.float32) + b.astype(jnp.float32)
    return y.astype(x.dtype)


def rotary(t):
    s, dh = t.shape[1], t.shape[-1]
    half = dh // 2
    inv = ROPE_BASE ** (-jnp.arange(half, dtype=jnp.float32) / half)
    ang = jnp.arange(s, dtype=jnp.float32)[:, None] * inv[None, :]
    cos = jnp.cos(ang)[None, :, None, :]
    sin = jnp.sin(ang)[None, :, None, :]
    t1 = t[..., :half].astype(jnp.float32)
    t2 = t[..., half:].astype(jnp.float32)
    return jnp.concatenate([t1 * cos - t2 * sin, t1 * sin + t2 * cos], axis=-1).astype(t.dtype)


def retention_chunkwise(q, k, v, log_gamma):
    b, h, s, dk = q.shape
    dv = v.shape[-1]
    c = RET_CHUNK
    n = s // c
    qc = q.reshape(b, h, n, c, dk)
    kc = k.reshape(b, h, n, c, dk)
    vc = v.reshape(b, h, n, c, dv)
    pos = jnp.arange(c, dtype=jnp.float32)
    diff = pos[:, None] - pos[None, :]
    lg = log_gamma[:, None, None]
    decay = jnp.where(diff >= 0, jnp.exp(lg * jnp.maximum(diff, 0.0)), 0.0).astype(q.dtype)
    scores = jnp.einsum('bhnqd,bhnkd->bhnqk', qc, kc) * decay[None, :, None]
    inner = jnp.einsum('bhnqk,bhnkv->bhnqv', scores, vc)
    zeta = jnp.exp(log_gamma[:, None] * (c - 1 - pos)).astype(q.dtype)
    xi = jnp.exp(log_gamma[:, None] * (pos + 1)).astype(q.dtype)
    chunk_decay = jnp.exp(log_gamma * c).astype(q.dtype)[None, :, None, None]
    kv = jnp.einsum('bhnkd,hk,bhnkv->nbhdv', kc, zeta, vc)

    def step(state, kv_n):
        return state * chunk_decay + kv_n, state

    _, prev = lax.scan(step, jnp.zeros((b, h, dk, dv), q.dtype), kv)
    cross = jnp.einsum('bhnqd,nbhdv,hq->bhnqv', qc, prev, xi)
    return (inner + cross).reshape(b, h, s, dv)


def head_group_norm(o, g):
    of = o.astype(jnp.float32)
    mu = jnp.mean(of, axis=-1, keepdims=True)
    var = jnp.mean(jnp.square(of - mu), axis=-1, keepdims=True)
    y = ((of - mu) * lax.rsqrt(var + LN_EPS)).reshape(o.shape[0], o.shape[1], -1)
    return (y * g.astype(jnp.float32)).astype(o.dtype)


def t5_bucket(rel):
    half = NUM_BUCKETS // 2
    max_exact = half // 2
    side = jnp.where(rel > 0, half, 0)
    n = jnp.abs(rel)
    large = max_exact + (jnp.log(jnp.maximum(n, 1).astype(jnp.float32) / max_exact)
                         / math.log(MAX_DISTANCE / max_exact) * (half - max_exact)).astype(jnp.int32)
    large = jnp.minimum(large, half - 1)
    return side + jnp.where(n < max_exact, n, large)


def dilated_pattern(q, k, v, rel_bias, window, dil):
    b, s, h, dh = q.shape
    radius = window // (2 * dil)
    blk = radius
    n = s // dil
    nb = -(-n // blk)
    pad_n = nb * blk - n

    def by_stride(t):
        return t.reshape(b, n, dil, h, dh).transpose(0, 2, 1, 3, 4)

    qs, ks, vs = by_stride(q), by_stride(k), by_stride(v)
    qb = jnp.pad(qs, ((0, 0), (0, 0), (0, pad_n), (0, 0), (0, 0))).reshape(b, dil, nb, blk, h, dh)

    def band(t):
        tp = jnp.pad(t, ((0, 0), (0, 0), (blk, blk + pad_n), (0, 0), (0, 0)))
        blocks = tp.reshape(b, dil, nb + 2, blk, h, dh)
        return jnp.concatenate([blocks[:, :, :-2], blocks[:, :, 1:-1], blocks[:, :, 2:]], axis=3)

    kb, vb = band(ks), band(vs)
    qi = np.arange(blk)
    kk = np.arange(3 * blk)
    delta = kk[None, :] - blk - qi[:, None]
    in_window = np.abs(delta) <= radius
    key_sub = np.arange(nb)[:, None] * blk + kk[None, :] - blk
    valid = (key_sub >= 0) & (key_sub < n)
    mask = jnp.asarray(in_window[None, :, :] & valid[:, None, :])
    bias = rel_bias[t5_bucket(jnp.asarray(delta * dil))].astype(jnp.float32).transpose(2, 0, 1)

    logits = jnp.einsum('brnqhe,brnkhe->brnhqk', qb, kb).astype(jnp.float32) + bias[None, None, None]
    logits = jnp.where(mask[None, None, :, None], logits, NEG_INF)
    m = jnp.max(logits, axis=-1, keepdims=True)
    p = jnp.exp(logits - m)
    denom = jnp.sum(p, axis=-1, keepdims=True)
    o = jnp.einsum('brnhqk,brnkhe->brnhqe', p.astype(v.dtype), vb) / denom.astype(v.dtype)
    lse = (m + jnp.log(denom))[..., 0]
    o = o.transpose(0, 2, 4, 1, 3, 5).reshape(b, nb * blk, dil, h, dh)[:, :n].reshape(b, s, h, dh)
    lse = lse.transpose(0, 2, 4, 1, 3).reshape(b, nb * blk, dil, h)[:, :n].reshape(b, s, h)
    return o, lse


def dilated_attention(q, k, v, rel_bias):
    outs, lses = [], []
    for window, dil in DILATED_PATTERNS:
        o, lse = dilated_pattern(q, k, v, rel_bias, window, dil)
        outs.append(o)
        lses.append(lse)
    w = jax.nn.softmax(jnp.stack(lses, axis=0), axis=0)
    return jnp.sum(w[..., None].astype(q.dtype) * jnp.stack(outs, axis=0), axis=0)


def token_mixers(x, w_in, ret_decay_fwd, ret_decay_bwd, ret_gn_g, w_proj_ret, rel_bias, w_proj_attn, w_out):
    b, s, _ = x.shape
    hcat = x @ w_in
    q_r, k_r, v_r, g_r, q_a, k_a, v_a, gate_r, gate_a = jnp.split(hcat, SPLIT_POINTS, axis=-1)
    q_r = rotary(q_r.reshape(b, s, RET_HEADS, RET_DK))
    k_r = rotary(k_r.reshape(b, s, RET_HEADS, RET_DK)) * (RET_DK ** -0.5)
    v_r = v_r.reshape(b, s, RET_HEADS, RET_DV)
    qh, kh, vh = (t.transpose(0, 2, 1, 3) for t in (q_r, k_r, v_r))
    lg_f = jax.nn.log_sigmoid(ret_decay_fwd.astype(jnp.float32))
    lg_b = jax.nn.log_sigmoid(ret_decay_bwd.astype(jnp.float32))
    o_f = retention_chunkwise(qh, kh, vh, lg_f)
    o_b = jnp.flip(retention_chunkwise(jnp.flip(qh, 2), jnp.flip(kh, 2), jnp.flip(vh, 2), lg_b), 2)
    o_r = head_group_norm((o_f + o_b).transpose(0, 2, 1, 3), ret_gn_g)
    y_r = (jax.nn.silu(g_r) * o_r) @ w_proj_ret
    q_a = q_a.reshape(b, s, ATT_HEADS, ATT_DH) * (ATT_DH ** -0.5)
    k_a = k_a.reshape(b, s, ATT_HEADS, ATT_DH)
    v_a = v_a.reshape(b, s, ATT_HEADS, ATT_DH)
    y_a = dilated_attention(q_a, k_a, v_a, rel_bias).reshape(b, s, ATT_W) @ w_proj_attn
    merged = jax.nn.sigmoid(gate_r) * y_r + jax.nn.sigmoid(gate_a) * y_a
    return merged @ w_out


def hierarchical_moe(x, router_w_group, router_b_group, router_w_expert, router_b_expert, w_gate, w_up, w_down):
    b, s, d = x.shape
    t = b * s
    xt = x.reshape(t, d)
    g_logits = (xt @ router_w_group).astype(jnp.float32) + router_b_group.astype(jnp.float32)
    g_prob = jax.nn.softmax(g_logits, axis=-1)
    g_idx = jnp.argmax(g_logits, axis=-1)
    g_w = jnp.take_along_axis(g_prob, g_idx[:, None], axis=-1)
    e_logits = ((xt @ router_w_expert).astype(jnp.float32) + router_b_expert.astype(jnp.float32)).reshape(t, N_GROUPS, EXPERTS_PER_GROUP)
    e_logits = jnp.take_along_axis(e_logits, g_idx[:, None, None], axis=1)[:, 0]
    top_v, top_i = lax.top_k(e_logits, TOP_K)
    weights = g_w * jax.nn.softmax(top_v, axis=-1)
    expert_id = g_idx[:, None].astype(jnp.int32) * EXPERTS_PER_GROUP + top_i.astype(jnp.int32)
    a = t * TOP_K
    flat_e = expert_id.reshape(a)
    flat_w = weights.reshape(a)
    order = jnp.argsort(flat_e)
    e_sorted = flat_e[order]
    tok_sorted = (order // TOP_K).astype(jnp.int32)
    counts = jnp.zeros((N_EXPERTS,), jnp.int32).at[flat_e].add(1)
    padded = (counts + MOE_BLOCK - 1) // MOE_BLOCK * MOE_BLOCK
    start = jnp.cumsum(counts) - counts
    ends = jnp.cumsum(padded)
    pstart = ends - padded
    dest = pstart[e_sorted] + jnp.arange(a, dtype=jnp.int32) - start[e_sorted]
    n_blocks = -(-a // MOE_BLOCK) + N_EXPERTS
    slot_tok = jnp.full((n_blocks * MOE_BLOCK,), t, jnp.int32).at[dest].set(tok_sorted)
    block_start = jnp.arange(n_blocks, dtype=jnp.int32) * MOE_BLOCK
    block_expert = jnp.minimum(jnp.sum(ends[None, :] <= block_start[:, None], axis=1), N_EXPERTS - 1).astype(jnp.int32)
    x_pad = jnp.concatenate([xt, jnp.zeros((1, d), xt.dtype)], axis=0)
    xb = x_pad[slot_tok].reshape(n_blocks, MOE_BLOCK, d)

    def expert_block(args):
        xblk, e = args
        hid = jax.nn.silu(xblk @ w_gate[e]) * (xblk @ w_up[e])
        return hid @ w_down[e]

    yb = lax.map(expert_block, (xb, block_expert)).reshape(n_blocks * MOE_BLOCK, d)
    y_assign = yb[dest] * flat_w[order][:, None].astype(yb.dtype)
    out = jnp.zeros((t, d), yb.dtype).at[tok_sorted].add(y_assign)
    return out.reshape(b, s, d)


def setup_inputs(seed: int = 0) -> dict:
    key = jax.random.key(seed)
    ks = jax.random.split(key, 24)
    L = DEPTH

    def nrm(k, shape, scale):
        return jax.random.normal(k, shape, jnp.float32) * scale

    x = nrm(ks[0], (BATCH, SEQ, D_MODEL), 1.0)
    col = jnp.arange(IN_COLS)
    v_cols = ((col >= IN_OFFSETS[2]) & (col < IN_OFFSETS[3])) | ((col >= IN_OFFSETS[6]) & (col < IN_OFFSETS[7]))
    w_in = nrm(ks[1], (L, D_MODEL, IN_COLS), D_MODEL ** -0.5) * jnp.where(v_cols, DEEPNORM_BETA, 1.0)
    gamma = 1.0 - jnp.exp(jnp.linspace(math.log(1.0 / 32), math.log(1.0 / 512), RET_HEADS))
    logit = jnp.log(gamma) - jnp.log1p(-gamma)
    ret_decay_fwd = logit[None, :] + nrm(ks[2], (L, RET_HEADS), 0.05)
    ret_decay_bwd = logit[None, :] + nrm(ks[3], (L, RET_HEADS), 0.05)
    ret_gn_g = 1.0 + nrm(ks[4], (L, RET_V_W), 0.02)
    w_proj_ret = nrm(ks[5], (L, RET_V_W, D_MODEL), RET_V_W ** -0.5 * DEEPNORM_BETA)
    rel_bias = nrm(ks[6], (NUM_BUCKETS, ATT_HEADS), 0.5)
    w_proj_attn = nrm(ks[7], (L, ATT_W, D_MODEL), ATT_W ** -0.5 * DEEPNORM_BETA)
    w_out = nrm(ks[8], (L, D_MODEL, D_MODEL), D_MODEL ** -0.5 * DEEPNORM_BETA)
    ln1_g = 1.0 + nrm(ks[9], (L, D_MODEL), 0.02)
    ln1_b = nrm(ks[10], (L, D_MODEL), 0.02)
    router_w_group = nrm(ks[11], (L, D_MODEL, N_GROUPS), D_MODEL ** -0.5)
    router_b_group = nrm(ks[12], (L, N_GROUPS), 0.01)
    router_w_expert = nrm(ks[13], (L, D_MODEL, N_EXPERTS), D_MODEL ** -0.5)
    router_b_expert = nrm(ks[14], (L, N_EXPERTS), 0.01)
    w_gate = nrm(ks[15], (L, N_EXPERTS, D_MODEL, D_EXPERT), D_MODEL ** -0.5)
    w_up = nrm(ks[16], (L, N_EXPERTS, D_MODEL, D_EXPERT), D_MODEL ** -0.5)
    w_down = nrm(ks[17], (L, N_EXPERTS, D_EXPERT, D_MODEL), D_EXPERT ** -0.5 * DEEPNORM_BETA)
    ln2_g = 1.0 + nrm(ks[18], (L, D_MODEL), 0.02)
    ln2_b = nrm(ks[19], (L, D_MODEL), 0.02)
    return {'x': x, 'w_in': w_in, 'ret_decay_fwd': ret_decay_fwd, 'ret_decay_bwd': ret_decay_bwd,
            'ret_gn_g': ret_gn_g, 'w_proj_ret': w_proj_ret, 'rel_bias': rel_bias, 'w_proj_attn': w_proj_attn,
            'w_out': w_out, 'ln1_g': ln1_g, 'ln1_b': ln1_b, 'router_w_group': router_w_group,
            'router_b_group': router_b_group, 'router_w_expert': router_w_expert, 'router_b_expert': router_b_expert,
            'w_gate': w_gate, 'w_up': w_up, 'w_down': w_down, 'ln2_g': ln2_g, 'ln2_b': ln2_b}


def reference(x, w_in, ret_decay_fwd, ret_decay_bwd, ret_gn_g, w_proj_ret, rel_bias, w_proj_attn, w_out,
              ln1_g, ln1_b, router_w_group, router_b_group, router_w_expert, router_b_expert,
              w_gate, w_up, w_down, ln2_g, ln2_b):
    for l in range(DEPTH):
        mix = token_mixers(x, w_in[l], ret_decay_fwd[l], ret_decay_bwd[l], ret_gn_g[l], w_proj_ret[l],
                           rel_bias, w_proj_attn[l], w_out[l])
        x = layer_norm(DEEPNORM_ALPHA * x + mix, ln1_g[l], ln1_b[l])
        ffn = hierarchical_moe(x, router_w_group[l], router_b_group[l], router_w_expert[l], router_b_expert[l],
                               w_gate[l], w_up[l], w_down[l])
        x = layer_norm(DEEPNORM_ALPHA * x + ffn, ln2_g[l], ln2_b[l])
    return x
```

```python
import functools
import math

import numpy as np
import jax
import jax.numpy as jnp
from jax import lax
from jax.experimental import pallas as pl
from jax.experimental.pallas import tpu as pltpu

F32 = jnp.float32
BF16 = jnp.bfloat16

D_MODEL = 1024
RET_HEADS = 4
RET_DK = 128
RET_DV = 256
RET_CHUNK = 128
ROPE_BASE = 10000.0
ATT_HEADS = 8
ATT_DH = 64
ATT_W = ATT_HEADS * ATT_DH
DILATED_PATTERNS = ((128, 1), (512, 4), (2048, 16))
NUM_BUCKETS = 32
MAX_DISTANCE = 1024
N_GROUPS = 4
EXPERTS_PER_GROUP = 4
N_EXPERTS = N_GROUPS * EXPERTS_PER_GROUP
TOP_K = 2
D_EXPERT = 512
DEPTH = 1
DEEPNORM_ALPHA = (2.0 * DEPTH) ** 0.25
LN_EPS = 1e-5
NEG_INF = -1e30

COL_QR, COL_KR, COL_VR, COL_GR, COL_BR, COL_BA, COL_QA, COL_KA, COL_VA = (
    0, 512, 1024, 2048, 3072, 4096, 5120, 5632, 6144)
IN_COLS = 6656
_REF_OFF = dict(qr=0, kr=512, vr=1024, gr=2048, qa=3072, ka=3584, va=4096, br=4608, ba=5632)
_COL_PERM = np.concatenate([
    np.arange(_REF_OFF["qr"], _REF_OFF["qr"] + 512), np.arange(_REF_OFF["kr"], _REF_OFF["kr"] + 512),
    np.arange(_REF_OFF["vr"], _REF_OFF["vr"] + 1024), np.arange(_REF_OFF["gr"], _REF_OFF["gr"] + 1024),
    np.arange(_REF_OFF["br"], _REF_OFF["br"] + 1024), np.arange(_REF_OFF["ba"], _REF_OFF["ba"] + 1024),
    np.arange(_REF_OFF["qa"], _REF_OFF["qa"] + 512), np.arange(_REF_OFF["ka"], _REF_OFF["ka"] + 512),
    np.arange(_REF_OFF["va"], _REF_OFF["va"] + 512)])

V7X_VMEM_LIMIT_BYTES = 56 * 1024 * 1024
LANES = 128

PROJ_TM = 1024
PROJ_TN = 1664
ATT_BQ = 128
ATT_RADIUS = 64
MERGE_TM = 512
EXPERT_ROWS = 256
COMBINE_TM = 256


def _cparams(sem):
    return pltpu.CompilerParams(dimension_semantics=sem, vmem_limit_bytes=V7X_VMEM_LIMIT_BYTES)


def _proj_in_kernel(x_ref, w_ref, o_ref, xb_ref):
    @pl.when(pl.program_id(1) == 0)
    def _():
        xb_ref[...] = x_ref[...].astype(BF16)

    o_ref[...] = jnp.dot(xb_ref[...], w_ref[...], preferred_element_type=F32).astype(o_ref.dtype)


def _proj_in(x2d, w_bf16):
    t = x2d.shape[0]
    tm = min(PROJ_TM, t)
    return pl.pallas_call(
        _proj_in_kernel,
        grid=(t // tm, IN_COLS // PROJ_TN),
        in_specs=[pl.BlockSpec((tm, D_MODEL), lambda i, j: (i, 0)),
                  pl.BlockSpec((D_MODEL, PROJ_TN), lambda i, j: (0, j))],
        out_specs=pl.BlockSpec((tm, PROJ_TN), lambda i, j: (i, j)),
        out_shape=jax.ShapeDtypeStruct((t, IN_COLS), BF16),
        scratch_shapes=[pltpu.VMEM((tm, D_MODEL), BF16)],
        compiler_params=_cparams(("parallel", "arbitrary")),
        name="proj_in",
    )(x2d, w_bf16)


def _retention_kernel(dec_ref, q_ref, k_ref, v_ref, g_ref, cos_ref, sin_ref, gn_ref, o_ref,
                      kr_ref, nb_ref, st_ref, *, n_chunks):
    c_len = RET_CHUNK
    h = pl.program_id(1)
    lgf = dec_ref[0, h]
    lgb = dec_ref[1, h]
    row = lax.broadcasted_iota(jnp.int32, (c_len, c_len), 0).astype(F32)
    col = lax.broadcasted_iota(jnp.int32, (c_len, c_len), 1).astype(F32)
    diff = row - col
    dmat = jnp.where(diff > 0, jnp.exp(lgf * jnp.maximum(diff, 0.0)),
                     jnp.where(diff < 0, jnp.exp(lgb * jnp.maximum(-diff, 0.0)), 2.0))
    zeta_f = jnp.exp(lgf * (c_len - 1.0 - row))
    zeta_b = jnp.exp(lgb * row)
    xi_f = jnp.exp(lgf * (row + 1.0))
    xi_b = jnp.exp(lgb * (c_len - row))
    ones_row = jnp.zeros((1, RET_DV), F32)
    decay_f = jnp.exp(ones_row + lgf * c_len)
    decay_b = jnp.exp(ones_row + lgb * c_len)
    k_scale = RET_DK ** -0.5
    tn_dims = (((0,), (0,)), ((), ()))
    nt_dims = (((1,), (1,)), ((), ()))

    def rows_of(c):
        return pl.ds(pl.multiple_of(c * c_len, c_len), c_len)

    def rotary(t, rows):
        return t * cos_ref[rows, :] + pltpu.roll(t, RET_DK // 2, 1) * sin_ref[rows, :]

    st_ref[...] = jnp.zeros_like(st_ref)

    def back_body(i, carry):
        c = n_chunks - 1 - i
        rows = rows_of(c)
        kr = rotary(k_ref[rows, :].astype(F32), rows) * k_scale
        kr_ref[rows, :] = kr
        nb_ref[c] = st_ref[...].astype(BF16)
        kv = lax.dot_general((kr * zeta_b).astype(BF16), v_ref[rows, :], tn_dims,
                             preferred_element_type=F32)
        st_ref[...] = st_ref[...] * decay_b + kv
        return carry

    lax.fori_loop(0, n_chunks, back_body, 0)

    st_ref[...] = jnp.zeros_like(st_ref)

    def fwd_body(c, carry):
        rows = rows_of(c)
        qr = rotary(q_ref[rows, :].astype(F32), rows)
        kr = kr_ref[rows, :]
        vb = v_ref[rows, :]
        scores = lax.dot_general(qr.astype(BF16), kr.astype(BF16), nt_dims, preferred_element_type=F32)
        o = jnp.dot((scores * dmat).astype(BF16), vb, preferred_element_type=F32)
        o += jnp.dot((qr * xi_f).astype(BF16), st_ref[...].astype(BF16), preferred_element_type=F32)
        o += jnp.dot((qr * xi_b).astype(BF16), nb_ref[c], preferred_element_type=F32)
        kv = lax.dot_general((kr * zeta_f).astype(BF16), vb, tn_dims, preferred_element_type=F32)
        st_ref[...] = st_ref[...] * decay_f + kv
        mu = jnp.mean(o, axis=-1, keepdims=True)
        d = o - mu
        var = jnp.mean(d * d, axis=-1, keepdims=True)
        y = d * lax.rsqrt(var + LN_EPS) * gn_ref[...]
        gate = g_ref[rows, :].astype(F32)
        o_ref[rows, :] = (gate * jax.nn.sigmoid(gate) * y).astype(o_ref.dtype)
        return carry

    lax.fori_loop(0, n_chunks, fwd_body, 0)


def _retention(hcat3, dec, cos_t, sin_t, gn_g):
    b, s, _ = hcat3.shape
    n_chunks = s // RET_CHUNK
    kern = functools.partial(_retention_kernel, n_chunks=n_chunks)
    return pl.pallas_call(
        kern,
        grid=(b, RET_HEADS),
        in_specs=[
            pl.BlockSpec(memory_space=pltpu.SMEM),
            pl.BlockSpec((None, s, RET_DK), lambda i, h: (i, 0, COL_QR // RET_DK + h)),
            pl.BlockSpec((None, s, RET_DK), lambda i, h: (i, 0, COL_KR // RET_DK + h)),
            pl.BlockSpec((None, s, RET_DV), lambda i, h: (i, 0, COL_VR // RET_DV + h)),
            pl.BlockSpec((None, s, RET_DV), lambda i, h: (i, 0, COL_GR // RET_DV + h)),
            pl.BlockSpec((s, RET_DK), lambda i, h: (0, 0)),
            pl.BlockSpec((s, RET_DK), lambda i, h: (0, 0)),
            pl.BlockSpec((1, RET_DV), lambda i, h: (0, h)),
        ],
        out_specs=pl.BlockSpec((None, s, RET_DV), lambda i, h: (i, 0, h)),
        out_shape=jax.ShapeDtypeStruct((b, s, RET_HEADS * RET_DV), BF16),
        scratch_shapes=[pltpu.VMEM((s, RET_DK), F32),
                        pltpu.VMEM((n_chunks, RET_DK, RET_DV), BF16),
                        pltpu.VMEM((RET_DK, RET_DV), F32)],
        compiler_params=_cparams(("parallel", "arbitrary")),
        name="retention",
    )(dec, hcat3, hcat3, hcat3, hcat3, cos_t, sin_t, gn_g)


def _attention_kernel(q_ref, k_ref, v_ref, bias_ref, o_ref, lse_ref, *, n):
    nq = n // ATT_BQ
    kw = min(2 * ATT_BQ, n)
    nt_dims = (((1,), (1,)), ((), ()))
    lane = lax.broadcasted_iota(jnp.int32, (ATT_BQ, LANES), 1)

    def block(i, carry):
        q0 = pl.multiple_of(i * ATT_BQ, ATT_BQ)
        if nq == 1:
            kstart = 0
            var = 0
        else:
            kstart = pl.multiple_of(jnp.clip(q0 - ATT_RADIUS, 0, n - kw), ATT_RADIUS)
            var = jnp.where(i == 0, 0, jnp.where(i == nq - 1, 2, 1))
        outs = []
        lse_tile = jnp.zeros((ATT_BQ, LANES), F32)
        for h in range(ATT_HEADS):
            cs = slice(h * ATT_DH, (h + 1) * ATT_DH)
            qh = q_ref[pl.ds(q0, ATT_BQ), cs]
            kh = k_ref[pl.ds(kstart, kw), cs]
            vh = v_ref[pl.ds(kstart, kw), cs]
            s = lax.dot_general(qh, kh, nt_dims, preferred_element_type=F32) * (ATT_DH ** -0.5)
            s = s + bias_ref[var, h]
            m = jnp.max(s, axis=-1, keepdims=True)
            p = jnp.exp(s - m)
            l = jnp.sum(p, axis=-1, keepdims=True)
            o = jnp.dot(p.astype(BF16), vh, preferred_element_type=F32) / l
            outs.append(o)
            lse_tile = jnp.where(lane == h, m + jnp.log(l), lse_tile)
        o_ref[pl.ds(q0, ATT_BQ), :] = jnp.concatenate(outs, axis=1).astype(o_ref.dtype)
        lse_ref[pl.ds(q0, ATT_BQ), :] = lse_tile
        return carry

    lax.fori_loop(0, nq, block, 0)


def _attention_pattern(hcat3, bias, dil):
    b, s, _ = hcat3.shape
    n = s // dil
    view = hcat3.reshape(b, n, dil * IN_COLS)
    blocks_per_row = IN_COLS // ATT_W
    kern = functools.partial(_attention_kernel, n=n)

    def col(off):
        return lambda i, r: (i, 0, r * blocks_per_row + off // ATT_W)

    nvar, _, _, kw = bias.shape
    o, lse = pl.pallas_call(
        kern,
        grid=(b, dil),
        in_specs=[pl.BlockSpec((None, n, ATT_W), col(COL_QA)),
                  pl.BlockSpec((None, n, ATT_W), col(COL_KA)),
                  pl.BlockSpec((None, n, ATT_W), col(COL_VA)),
                  pl.BlockSpec((nvar, ATT_HEADS, ATT_BQ, kw), lambda i, r: (0, 0, 0, 0))],
        out_specs=[pl.BlockSpec((None, n, ATT_W), lambda i, r: (i, 0, r)),
                   pl.BlockSpec((None, n, LANES), lambda i, r: (i, 0, r))],
        out_shape=[jax.ShapeDtypeStruct((b, n, dil * ATT_W), BF16),
                   jax.ShapeDtypeStruct((b, n, dil * LANES), F32)],
        compiler_params=_cparams(("parallel", "parallel")),
        name=f"attention_d{dil}",
    )(view, view, view, bias)
    return o.reshape(b * s, ATT_W), lse.reshape(b * s, LANES)


def _t5_bucket_np(rel):
    half = NUM_BUCKETS // 2
    max_exact = half // 2
    side = np.where(rel > 0, half, 0)
    n = np.abs(rel)
    large = max_exact + (np.log(np.maximum(n, 1).astype(np.float32) / max_exact)
                         / math.log(MAX_DISTANCE / max_exact) * (half - max_exact)).astype(np.int32)
    large = np.minimum(large, half - 1)
    return side + np.where(n < max_exact, n, large)


def _attention_bias(rel_bias, dil, n):
    kw = min(2 * ATT_BQ, n)
    offsets = (0,) if n == ATT_BQ else (0, ATT_RADIUS, ATT_BQ)
    rows = np.arange(ATT_BQ)[:, None]
    cols = np.arange(kw)[None, :]
    tiles = []
    for off in offsets:
        delta = cols - off - rows
        inside = np.abs(delta) <= ATT_RADIUS
        bucket = _t5_bucket_np(np.clip(delta, -ATT_RADIUS, ATT_RADIUS) * dil)
        vals = rel_bias.astype(F32)[jnp.asarray(bucket)]
        vals = jnp.where(jnp.asarray(inside)[:, :, None], vals, NEG_INF)
        tiles.append(jnp.transpose(vals, (2, 0, 1)))
    return jnp.stack(tiles, axis=0)


def _layer_norm(z, g, b):
    mu = jnp.mean(z, axis=-1, keepdims=True)
    d = z - mu
    var = jnp.mean(d * d, axis=-1, keepdims=True)
    return d * lax.rsqrt(var + LN_EPS) * g + b


def _split_bf16(a):
    hi = a.astype(BF16)
    lo = (a - hi.astype(F32)).astype(BF16)
    return hi, lo


def _merge_kernel(ret_ref, o1_ref, o2_ref, o3_ref, l1_ref, l2_ref, l3_ref, br_ref, ba_ref, x_ref,
                  wpr_ref, wpa_ref, wout_ref, lng_ref, lnb_ref, wrh_ref, wrl_ref, rb_ref, hexp_ref,
                  x1_ref, route_ref):
    y_r = jnp.dot(ret_ref[...], wpr_ref[...], preferred_element_type=F32)
    l1, l2, l3 = l1_ref[...], l2_ref[...], l3_ref[...]
    m = jnp.maximum(jnp.maximum(l1, l2), l3)
    e1, e2, e3 = jnp.exp(l1 - m), jnp.exp(l2 - m), jnp.exp(l3 - m)
    inv = 1.0 / (e1 + e2 + e3)

    def per_head_to_lanes(w):
        hi, lo = _split_bf16(w)
        return (jnp.dot(hi, hexp_ref[...], preferred_element_type=F32)
                + jnp.dot(lo, hexp_ref[...], preferred_element_type=F32))

    attn = (per_head_to_lanes(e1 * inv) * o1_ref[...].astype(F32)
            + per_head_to_lanes(e2 * inv) * o2_ref[...].astype(F32)
            + per_head_to_lanes(e3 * inv) * o3_ref[...].astype(F32))
    y_a = jnp.dot(attn.astype(BF16), wpa_ref[...], preferred_element_type=F32)
    merged = (jax.nn.sigmoid(br_ref[...].astype(F32)) * y_r
              + jax.nn.sigmoid(ba_ref[...].astype(F32)) * y_a)
    mix = jnp.dot(merged.astype(BF16), wout_ref[...], preferred_element_type=F32)
    x1 = _layer_norm(DEEPNORM_ALPHA * x_ref[...] + mix, lng_ref[...], lnb_ref[...])
    x1_ref[...] = x1

    xh, xl = _split_bf16(x1)
    logits = (jnp.dot(xh, wrh_ref[...], preferred_element_type=F32)
              + jnp.dot(xh, wrl_ref[...], preferred_element_type=F32)
              + jnp.dot(xl, wrh_ref[...], preferred_element_type=F32)) + rb_ref[...]
    tm = logits.shape[0]
    lane = lax.broadcasted_iota(jnp.int32, (tm, LANES), 1)
    lane_f = lane.astype(F32)
    big = float(LANES)
    gmask = lane < N_GROUPS
    gl = jnp.where(gmask, logits, -jnp.inf)
    gmax = jnp.max(gl, axis=-1, keepdims=True)
    g_idx = jnp.min(jnp.where(gl == gmax, lane_f, big), axis=-1, keepdims=True)
    g_w = 1.0 / jnp.sum(jnp.where(gmask, jnp.exp(gl - gmax), 0.0), axis=-1, keepdims=True)
    in_group = jnp.floor((lane_f - N_GROUPS) * (1.0 / EXPERTS_PER_GROUP)) == g_idx
    emask = (lane >= N_GROUPS) & (lane < N_GROUPS + N_EXPERTS) & in_group
    el = jnp.where(emask, logits, -jnp.inf)
    v1 = jnp.max(el, axis=-1, keepdims=True)
    i1 = jnp.min(jnp.where(el == v1, lane_f, big), axis=-1, keepdims=True)
    el2 = jnp.where(lane_f == i1, -jnp.inf, el)
    v2 = jnp.max(el2, axis=-1, keepdims=True)
    i2 = jnp.min(jnp.where(el2 == v2, lane_f, big), axis=-1, keepdims=True)
    t = jnp.exp(v2 - v1)
    p1 = 1.0 / (1.0 + t)
    w1 = g_w * p1
    w2 = g_w * (t * p1)
    route = jnp.where(lane == 0, i1 - N_GROUPS,
                      jnp.where(lane == 1, i2 - N_GROUPS,
                                jnp.where(lane == 2, w1, jnp.where(lane == 3, w2, 0.0))))
    route_ref[...] = route


def _merge(ret_act, o_list, lse_list, hcat, x2d, wpr, wpa, wout, lng, lnb, wrh, wrl, rb, hexp):
    t = x2d.shape[0]
    tm = min(MERGE_TM, t)
    row = lambda i: (i, 0)
    const = lambda i: (0, 0)
    in_specs = (
        [pl.BlockSpec((tm, D_MODEL), row)]
        + [pl.BlockSpec((tm, ATT_W), row)] * 3
        + [pl.BlockSpec((tm, LANES), row)] * 3
        + [pl.BlockSpec((tm, D_MODEL), lambda i: (i, COL_BR // D_MODEL)),
           pl.BlockSpec((tm, D_MODEL), lambda i: (i, COL_BA // D_MODEL)),
           pl.BlockSpec((tm, D_MODEL), row),
           pl.BlockSpec((D_MODEL, D_MODEL), const),
           pl.BlockSpec((ATT_W, D_MODEL), const),
           pl.BlockSpec((D_MODEL, D_MODEL), const),
           pl.BlockSpec((1, D_MODEL), const),
           pl.BlockSpec((1, D_MODEL), const),
           pl.BlockSpec((D_MODEL, LANES), const),
           pl.BlockSpec((D_MODEL, LANES), const),
           pl.BlockSpec((1, LANES), const),
           pl.BlockSpec((LANES, ATT_W), const)])
    return pl.pallas_call(
        _merge_kernel,
        grid=(t // tm,),
        in_specs=in_specs,
        out_specs=[pl.BlockSpec((tm, D_MODEL), row), pl.BlockSpec((tm, LANES), row)],
        out_shape=[jax.ShapeDtypeStruct((t, D_MODEL), F32), jax.ShapeDtypeStruct((t, LANES), F32)],
        compiler_params=_cparams(("parallel",)),
        name="merge_ln1_router",
    )(ret_act, *o_list, *lse_list, hcat, hcat, x2d, wpr, wpa, wout, lng, lnb, wrh, wrl, rb, hexp)


def _row_gather_start(idx_ref, n_rows, src_hbm, dst_buf, slot, sem):
    def body(r, carry):
        src_row = idx_ref[0, r]
        pltpu.make_async_copy(src_hbm.at[pl.ds(src_row, 1), :],
                              dst_buf.at[slot, pl.ds(r, 1), :], sem.at[slot]).start()
        return carry

    lax.fori_loop(0, n_rows, body, 0, unroll=8)


def _row_gather_wait(dst_buf, slot, sem):
    pltpu.make_async_copy(dst_buf.at[slot], dst_buf.at[slot], sem.at[slot]).wait()


def _experts_kernel(be_ref, tok_ref, tok_next_ref, x_hbm, wg_ref, wu_ref, wd_ref, y_ref, xbuf, sem):
    del be_ref
    j = pl.program_id(0)
    nb = pl.num_programs(0)
    slot = lax.rem(j, 2)
    rows = xbuf.shape[1]

    @pl.when(j == 0)
    def _():
        _row_gather_start(tok_ref, rows, x_hbm, xbuf, 0, sem)

    @pl.when(j + 1 < nb)
    def _():
        _row_gather_start(tok_next_ref, rows, x_hbm, xbuf, 1 - slot, sem)

    _row_gather_wait(xbuf, slot, sem)
    xb = xbuf[slot].astype(BF16)
    gate = jnp.dot(xb, wg_ref[...], preferred_element_type=F32)
    up = jnp.dot(xb, wu_ref[...], preferred_element_type=F32)
    hid = (gate * jax.nn.sigmoid(gate) * up).astype(BF16)
    y_ref[...] = jnp.dot(hid, wd_ref[...], preferred_element_type=F32)


def _experts(x1, slot_tok, block_expert, wg, wu, wd):
    n_blocks = block_expert.shape[0]
    r = EXPERT_ROWS
    tok3 = slot_tok.reshape(n_blocks, 1, r)
    grid_spec = pltpu.PrefetchScalarGridSpec(
        num_scalar_prefetch=1,
        grid=(n_blocks,),
        in_specs=[
            pl.BlockSpec((None, 1, r), lambda j, be: (j, 0, 0), memory_space=pltpu.SMEM),
            pl.BlockSpec((None, 1, r), lambda j, be: (jnp.minimum(j + 1, n_blocks - 1), 0, 0),
                         memory_space=pltpu.SMEM),
            pl.BlockSpec(memory_space=pl.ANY),
            pl.BlockSpec((None, D_MODEL, D_EXPERT), lambda j, be: (be[j], 0, 0)),
            pl.BlockSpec((None, D_MODEL, D_EXPERT), lambda j, be: (be[j], 0, 0)),
            pl.BlockSpec((None, D_EXPERT, D_MODEL), lambda j, be: (be[j], 0, 0)),
        ],
        out_specs=pl.BlockSpec((r, D_MODEL), lambda j, be: (j, 0)),
        scratch_shapes=[pltpu.VMEM((2, r, D_MODEL), F32), pltpu.SemaphoreType.DMA((2,))],
    )
    return pl.pallas_call(
        _experts_kernel,
        grid_spec=grid_spec,
        out_shape=jax.ShapeDtypeStruct((n_blocks * r, D_MODEL), F32),
        compiler_params=_cparams(("arbitrary",)),
        name="experts",
    )(block_expert, tok3, tok3, x1, wg, wu, wd)


def _combine_kernel(pos_ref, pos_next_ref, x1_ref, route_ref, y_hbm, lng_ref, lnb_ref, o_ref, ybuf, sem):
    i = pl.program_id(0)
    n = pl.num_programs(0)
    slot = lax.rem(i, 2)
    rows = ybuf.shape[1]
    tm = rows // TOP_K

    @pl.when(i == 0)
    def _():
        _row_gather_start(pos_ref, rows, y_hbm, ybuf, 0, sem)

    @pl.when(i + 1 < n)
    def _():
        _row_gather_start(pos_next_ref, rows, y_hbm, ybuf, 1 - slot, sem)

    _row_gather_wait(ybuf, slot, sem)
    route = route_ref[...]
    w1 = route[:, 2:3]
    w2 = route[:, 3:4]
    ffn = ybuf[slot, 0:tm, :] * w1 + ybuf[slot, tm:rows, :] * w2
    o_ref[...] = _layer_norm(DEEPNORM_ALPHA * x1_ref[...] + ffn, lng_ref[...], lnb_ref[...])


def _combine(x1, route, y_sorted, pos, lng, lnb):
    t = x1.shape[0]
    tm = min(COMBINE_TM, t)
    n = t // tm
    pos3 = pos.reshape(n, tm, TOP_K).transpose(0, 2, 1).reshape(n, 1, TOP_K * tm)
    row = lambda i: (i, 0)
    const = lambda i: (0, 0)
    return pl.pallas_call(
        _combine_kernel,
        grid=(n,),
        in_specs=[
            pl.BlockSpec((None, 1, TOP_K * tm), lambda i: (i, 0, 0), memory_space=pltpu.SMEM),
            pl.BlockSpec((None, 1, TOP_K * tm), lambda i: (jnp.minimum(i + 1, n - 1), 0, 0),
                         memory_space=pltpu.SMEM),
            pl.BlockSpec((tm, D_MODEL), row),
            pl.BlockSpec((tm, LANES), row),
            pl.BlockSpec(memory_space=pl.ANY),
            pl.BlockSpec((1, D_MODEL), const),
            pl.BlockSpec((1, D_MODEL), const),
        ],
        out_specs=pl.BlockSpec((tm, D_MODEL), row),
        out_shape=jax.ShapeDtypeStruct((t, D_MODEL), F32),
        scratch_shapes=[pltpu.VMEM((2, TOP_K * tm, D_MODEL), F32), pltpu.SemaphoreType.DMA((2,))],
        compiler_params=_cparams(("arbitrary",)),
        name="combine_ln2",
    )(pos3, pos3, x1, route, y_sorted, lng, lnb)


def _routing_plan(expert_id, n_tokens):
    r = EXPERT_ROWS
    a = n_tokens * TOP_K
    flat_e = expert_id.reshape(a)
    onehot = (flat_e[:, None] == jnp.arange(N_EXPERTS, dtype=jnp.int32)[None, :]).astype(jnp.int32)
    csum = jnp.cumsum(onehot, axis=0)
    counts = csum[-1]
    rank = jnp.sum(csum * onehot, axis=1) - 1
    padded = (counts + r - 1) // r * r
    ends = jnp.cumsum(padded)
    pstart = ends - padded
    dest = (jnp.sum(onehot * pstart[None, :], axis=1) + rank).astype(jnp.int32)
    n_blocks = -(-a // r) + N_EXPERTS
    slot_tok = jnp.zeros((n_blocks * r,), jnp.int32).at[dest].set(jnp.arange(a, dtype=jnp.int32) // TOP_K)
    block_start = jnp.arange(n_blocks, dtype=jnp.int32) * r
    block_expert = jnp.minimum(jnp.sum(ends[None, :] <= block_start[:, None], axis=1),
                               N_EXPERTS - 1).astype(jnp.int32)
    return slot_tok, block_expert, dest.reshape(n_tokens, TOP_K)


def _rotary_tables(s):
    half = RET_DK // 2
    inv = ROPE_BASE ** (-jnp.arange(half, dtype=F32) / half)
    ang = jnp.arange(s, dtype=F32)[:, None] * inv[None, :]
    cos, sin = jnp.cos(ang), jnp.sin(ang)
    return jnp.concatenate([cos, cos], axis=1), jnp.concatenate([-sin, sin], axis=1)


def _layer(x, w_in, dec_f, dec_b, gn_g, w_proj_ret, rel_bias, w_proj_attn, w_out, ln1_g, ln1_b,
           rw_g, rb_g, rw_e, rb_e, w_gate, w_up, w_down, ln2_g, ln2_b):
    b, s, d = x.shape
    t = b * s
    x2d = x.reshape(t, d)
    w_in_b = w_in[:, _COL_PERM].astype(BF16)
    hcat = _proj_in(x2d, w_in_b)
    hcat3 = hcat.reshape(b, s, IN_COLS)

    dec = jnp.stack([jax.nn.log_sigmoid(dec_f.astype(F32)), jax.nn.log_sigmoid(dec_b.astype(F32))], axis=0)
    cos_t, sin_t = _rotary_tables(s)
    ret_act = _retention(hcat3, dec, cos_t, sin_t, gn_g.reshape(1, -1).astype(F32)).reshape(t, -1)

    o_list, lse_list = [], []
    for window, dil in DILATED_PATTERNS:
        assert window // (2 * dil) == ATT_RADIUS and (s // dil) % ATT_BQ == 0
        bias = _attention_bias(rel_bias, dil, s // dil)
        o, lse = _attention_pattern(hcat3, bias, dil)
        o_list.append(o)
        lse_list.append(lse)

    wr = jnp.zeros((d, LANES), F32).at[:, :N_GROUPS].set(rw_g.astype(F32))
    wr = wr.at[:, N_GROUPS:N_GROUPS + N_EXPERTS].set(rw_e.astype(F32))
    rb = jnp.zeros((1, LANES), F32).at[0, :N_GROUPS].set(rb_g.astype(F32))
    rb = rb.at[0, N_GROUPS:N_GROUPS + N_EXPERTS].set(rb_e.astype(F32))
    wrh = wr.astype(BF16)
    wrl = (wr - wrh.astype(F32)).astype(BF16)
    hexp = np.zeros((LANES, ATT_W), np.float32)
    for h in range(ATT_HEADS):
        hexp[h, h * ATT_DH:(h + 1) * ATT_DH] = 1.0
    x1, route = _merge(ret_act, o_list, lse_list, hcat, x2d,
                       w_proj_ret.astype(BF16), w_proj_attn.astype(BF16), w_out.astype(BF16),
                       ln1_g.reshape(1, -1).astype(F32), ln1_b.reshape(1, -1).astype(F32),
                       wrh, wrl, rb, jnp.asarray(hexp, BF16))

    expert_id = route[:, 0:TOP_K].astype(jnp.int32)
    slot_tok, block_expert, pos = _routing_plan(expert_id, t)
    y_sorted = _experts(x1, slot_tok, block_expert,
                        w_gate.astype(BF16), w_up.astype(BF16), w_down.astype(BF16))
    out = _combine(x1, route, y_sorted, pos,
                   ln2_g.reshape(1, -1).astype(F32), ln2_b.reshape(1, -1).astype(F32))
    return out.reshape(b, s, d)


def kernel(x, w_in, ret_decay_fwd, ret_decay_bwd, ret_gn_g, w_proj_ret, rel_bias, w_proj_attn, w_out,
           ln1_g, ln1_b, router_w_group, router_b_group, router_w_expert, router_b_expert,
           w_gate, w_up, w_down, ln2_g, ln2_b):
    for l in range(DEPTH):
        x = _layer(x, w_in[l], ret_decay_fwd[l], ret_decay_bwd[l], ret_gn_g[l], w_proj_ret[l], rel_bias,
                   w_proj_attn[l], w_out[l], ln1_g[l], ln1_b[l], router_w_group[l], router_b_group[l],
                   router_w_expert[l], router_b_expert[l], w_gate[l], w_up[l], w_down[l],
                   ln2_g[l], ln2_b[l])
    return x
```

```python
import functools
import math

import numpy as np
import jax
import jax.numpy as jnp
from jax import lax
from jax.experimental import pallas as pl
from jax.experimental.pallas import tpu as pltpu

F32 = jnp.float32
BF16 = jnp.bfloat16

D_MODEL = 1024
RET_HEADS = 4
RET_DK = 128
RET_DV = 256
RET_CHUNK = 128
ROPE_BASE = 10000.0
ATT_HEADS = 8
ATT_DH = 64
ATT_W = ATT_HEADS * ATT_DH
DILATED_PATTERNS = ((128, 1), (512, 4), (2048, 16))
NUM_BUCKETS = 32
MAX_DISTANCE = 1024
N_GROUPS = 4
EXPERTS_PER_GROUP = 4
N_EXPERTS = N_GROUPS * EXPERTS_PER_GROUP
TOP_K = 2
D_EXPERT = 512
DEPTH = 1
DEEPNORM_ALPHA = (2.0 * DEPTH) ** 0.25
LN_EPS = 1e-5
NEG_INF = -1e30

REF_QR, REF_KR, REF_VR, REF_GR, REF_QA, REF_KA, REF_VA, REF_BR, REF_BA = (
    0, 512, 1024, 2048, 3072, 3584, 4096, 4608, 5632)
COL_QR, COL_KR, COL_VR, COL_GR, COL_BR, COL_BA = 0, 512, 1024, 2048, 3072, 4096
MAIN_COLS = 5120

V7X_VMEM_LIMIT_BYTES = 56 * 1024 * 1024
LANES = 128

PROJ_TM = 1024
PROJ_TN = 1280
ATT_BQ = 128
ATT_RADIUS = 64
ATT_KW = 2 * ATT_BQ
ATT_PAIR_W = 2 * ATT_DH
N_PAIRS = ATT_HEADS // 2
N_VARIANTS = 3
ATT_UNROLL = 4
MERGE_TM = 512
EXPERT_ROWS = 256
DMA_ISSUE_UNROLL = 8
PAIRS_PER_GROUP = EXPERTS_PER_GROUP * (EXPERTS_PER_GROUP - 1) // 2
N_CLASSES = N_GROUPS * PAIRS_PER_GROUP
X1R_COLS = D_MODEL + LANES
ROUTE_W_LO, ROUTE_W_HI, ROUTE_CLS = 0, 1, 2


def _cparams(sem):
    return pltpu.CompilerParams(dimension_semantics=sem, vmem_limit_bytes=V7X_VMEM_LIMIT_BYTES)


def _proj_in_kernel(x_ref, w_ref, o_ref, xb_ref):
    @pl.when(pl.program_id(1) == 0)
    def _():
        xb_ref[...] = x_ref[...].astype(BF16)

    o_ref[...] = jnp.dot(xb_ref[...], w_ref[...], preferred_element_type=F32).astype(o_ref.dtype)


def _proj_in(x2d, w_bf16):
    t = x2d.shape[0]
    n_cols = w_bf16.shape[1]
    tm = min(PROJ_TM, t)
    return pl.pallas_call(
        _proj_in_kernel,
        grid=(t // tm, n_cols // PROJ_TN),
        in_specs=[pl.BlockSpec((tm, D_MODEL), lambda i, j: (i, 0)),
                  pl.BlockSpec((D_MODEL, PROJ_TN), lambda i, j: (0, j))],
        out_specs=pl.BlockSpec((tm, PROJ_TN), lambda i, j: (i, j)),
        out_shape=jax.ShapeDtypeStruct((t, n_cols), BF16),
        scratch_shapes=[pltpu.VMEM((tm, D_MODEL), BF16)],
        compiler_params=_cparams(("parallel", "arbitrary")),
        name="proj_in",
    )(x2d, w_bf16)


def _retention_kernel(dec_ref, q_ref, k_ref, v_ref, g_ref, cos_ref, sin_ref, gn_ref, o_ref,
                      kr_ref, kvf_ref, kvb_ref, pf_ref, nb_ref, *, n_chunks):
    c_len = RET_CHUNK
    h = pl.program_id(1)
    lgf = dec_ref[0, h]
    lgb = dec_ref[1, h]
    row = lax.broadcasted_iota(jnp.int32, (c_len, c_len), 0).astype(F32)
    col = lax.broadcasted_iota(jnp.int32, (c_len, c_len), 1).astype(F32)
    diff = row - col
    dmat = jnp.where(diff > 0, jnp.exp(lgf * jnp.maximum(diff, 0.0)),
                     jnp.where(diff < 0, jnp.exp(lgb * jnp.maximum(-diff, 0.0)), 2.0))
    zeta_f = jnp.exp(lgf * (c_len - 1.0 - row))
    zeta_b = jnp.exp(lgb * row)
    xi_f = jnp.exp(lgf * (row + 1.0))
    xi_b = jnp.exp(lgb * (c_len - row))
    zero_row = jnp.zeros((1, RET_DV), F32)
    decay_f = jnp.exp(zero_row + lgf * c_len)
    decay_b = jnp.exp(zero_row + lgb * c_len)
    k_scale = RET_DK ** -0.5
    tn_dims = (((0,), (0,)), ((), ()))
    nt_dims = (((1,), (1,)), ((), ()))

    def rows_of(c):
        return pl.ds(pl.multiple_of(c * c_len, c_len), c_len)

    def rotary(t, rows):
        return t * cos_ref[rows, :] + pltpu.roll(t, RET_DK // 2, 1) * sin_ref[rows, :]

    def kv_body(c, carry):
        rows = rows_of(c)
        kr = rotary(k_ref[rows, :].astype(F32), rows) * k_scale
        kr_ref[rows, :] = kr.astype(BF16)
        vb = v_ref[rows, :]
        kvf_ref[c] = lax.dot_general((kr * zeta_f).astype(BF16), vb, tn_dims, preferred_element_type=F32)
        kvb_ref[c] = lax.dot_general((kr * zeta_b).astype(BF16), vb, tn_dims, preferred_element_type=F32)
        return carry

    lax.fori_loop(0, n_chunks, kv_body, 0, unroll=4)

    def scan_f(c, st):
        pf_ref[c] = st.astype(BF16)
        return st * decay_f + kvf_ref[c]

    lax.fori_loop(0, n_chunks, scan_f, jnp.zeros((RET_DK, RET_DV), F32))

    def scan_b(i, st):
        c = n_chunks - 1 - i
        nb_ref[c] = st.astype(BF16)
        return st * decay_b + kvb_ref[c]

    lax.fori_loop(0, n_chunks, scan_b, jnp.zeros((RET_DK, RET_DV), F32))

    def out_body(c, carry):
        rows = rows_of(c)
        qr = rotary(q_ref[rows, :].astype(F32), rows)
        vb = v_ref[rows, :]
        scores = lax.dot_general(qr.astype(BF16), kr_ref[rows, :], nt_dims, preferred_element_type=F32)
        o = jnp.dot((scores * dmat).astype(BF16), vb, preferred_element_type=F32)
        o += jnp.dot((qr * xi_f).astype(BF16), pf_ref[c], preferred_element_type=F32)
        o += jnp.dot((qr * xi_b).astype(BF16), nb_ref[c], preferred_element_type=F32)
        mu = jnp.mean(o, axis=-1, keepdims=True)
        d = o - mu
        var = jnp.mean(d * d, axis=-1, keepdims=True)
        y = d * lax.rsqrt(var + LN_EPS) * gn_ref[...]
        gate = g_ref[rows, :].astype(F32)
        o_ref[rows, :] = (gate * jax.nn.sigmoid(gate) * y).astype(o_ref.dtype)
        return carry

    lax.fori_loop(0, n_chunks, out_body, 0, unroll=4)


def _retention(hcat3, dec, cos_t, sin_t, gn_g):
    b, s, _ = hcat3.shape
    n_chunks = s // RET_CHUNK
    kern = functools.partial(_retention_kernel, n_chunks=n_chunks)
    return pl.pallas_call(
        kern,
        grid=(b, RET_HEADS),
        in_specs=[
            pl.BlockSpec(memory_space=pltpu.SMEM),
            pl.BlockSpec((None, s, RET_DK), lambda i, h: (i, 0, COL_QR // RET_DK + h)),
            pl.BlockSpec((None, s, RET_DK), lambda i, h: (i, 0, COL_KR // RET_DK + h)),
            pl.BlockSpec((None, s, RET_DV), lambda i, h: (i, 0, COL_VR // RET_DV + h)),
            pl.BlockSpec((None, s, RET_DV), lambda i, h: (i, 0, COL_GR // RET_DV + h)),
            pl.BlockSpec((s, RET_DK), lambda i, h: (0, 0)),
            pl.BlockSpec((s, RET_DK), lambda i, h: (0, 0)),
            pl.BlockSpec((1, RET_DV), lambda i, h: (0, h)),
        ],
        out_specs=pl.BlockSpec((None, s, RET_DV), lambda i, h: (i, 0, h)),
        out_shape=jax.ShapeDtypeStruct((b, s, RET_HEADS * RET_DV), BF16),
        scratch_shapes=[pltpu.VMEM((s, RET_DK), BF16),
                        pltpu.VMEM((n_chunks, RET_DK, RET_DV), F32),
                        pltpu.VMEM((n_chunks, RET_DK, RET_DV), F32),
                        pltpu.VMEM((n_chunks, RET_DK, RET_DV), BF16),
                        pltpu.VMEM((n_chunks, RET_DK, RET_DV), BF16)],
        compiler_params=_cparams(("parallel", "arbitrary")),
        name="retention",
    )(dec, hcat3, hcat3, hcat3, hcat3, cos_t, sin_t, gn_g)


def _attention_kernel(x_ref, w_ref, bias_ref, o_ref,
                      xb_ref, qkv_ref, qd_ref, kd_ref, vd_ref, op_ref, lp_ref, *, s):
    nt_dims = (((1,), (1,)), ((), ()))

    @pl.when(pl.program_id(1) == 0)
    def _():
        xb_ref[...] = x_ref[...].astype(BF16)

    proj_rows = 512
    for c in range(s // proj_rows):
        rows = slice(c * proj_rows, (c + 1) * proj_rows)
        acc = jnp.dot(xb_ref[rows, :], w_ref[...], preferred_element_type=F32)
        for t in range(3):
            qkv_ref[t, rows, :] = acc[:, t * ATT_PAIR_W:(t + 1) * ATT_PAIR_W]

    first_head = lax.broadcasted_iota(jnp.int32, (ATT_BQ, ATT_PAIR_W), 1) < ATT_DH

    def pattern(p, dil):
        n = s // dil
        nq = n // ATT_BQ
        kw = min(ATT_KW, n)
        n_blocks = s // ATT_BQ

        copy_rows = min(n, 512)

        def gather(r, carry):
            for c in range(n // copy_rows):
                if dil == 1:
                    src = pl.ds(c * copy_rows, copy_rows)
                else:
                    src = pl.ds(r + dil * c * copy_rows, copy_rows, stride=dil)
                dst = pl.ds(pl.multiple_of(r * n + c * copy_rows, copy_rows), copy_rows)
                qf = qkv_ref[0, src, :] * (ATT_DH ** -0.5)
                keep = lax.broadcasted_iota(jnp.int32, qf.shape, 1) < ATT_DH
                qd_ref[0, dst, :] = jnp.where(keep, qf, 0.0).astype(BF16)
                qd_ref[1, dst, :] = jnp.where(keep, 0.0, qf).astype(BF16)
                kd_ref[dst, :] = qkv_ref[1, src, :].astype(BF16)
                vd_ref[dst, :] = qkv_ref[2, src, :].astype(BF16)
            return carry

        lax.fori_loop(0, dil, gather, 0)

        def block(blk, carry):
            r = blk // nq
            i = blk - r * nq
            base = r * n
            q0 = i * ATT_BQ
            if nq == 1:
                kstart, var = 0, 0
            else:
                kstart = jnp.clip(q0 - ATT_RADIUS, 0, n - kw)
                var = jnp.where(i == 0, 0, jnp.where(i == nq - 1, 2, 1))
            q_rows = pl.ds(pl.multiple_of(base + q0, ATT_BQ), ATT_BQ)
            k_rows = pl.ds(pl.multiple_of(base + kstart, ATT_RADIUS), kw)
            kb = kd_ref[k_rows, :]
            vb = vd_ref[k_rows, :]
            o_heads, l_heads = [], []
            for hh in range(2):
                sc = lax.dot_general(qd_ref[hh, q_rows, :], kb, nt_dims, preferred_element_type=F32)
                sc = sc + bias_ref[p, var, hh, :, 0:kw]
                m = jnp.max(sc, axis=-1, keepdims=True)
                pe = jnp.exp(sc - m)
                l = jnp.sum(pe, axis=-1, keepdims=True)
                o_heads.append(jnp.dot(pe.astype(BF16), vb, preferred_element_type=F32) / l)
                l_heads.append(m + jnp.log(l))
            if dil == 1:
                dst = q_rows
            else:
                dst = pl.ds(r + dil * q0, ATT_BQ, stride=dil)
            op_ref[p, dst, :] = jnp.where(first_head, o_heads[0], o_heads[1])
            lp_ref[p, dst, :] = jnp.where(first_head, l_heads[0], l_heads[1])
            return carry

        lax.fori_loop(0, n_blocks, block, 0, unroll=ATT_UNROLL)

    for p, (_, dil) in enumerate(DILATED_PATTERNS):
        pattern(p, dil)

    mix_rows = 256
    for c in range(s // mix_rows):
        rows = slice(c * mix_rows, (c + 1) * mix_rows)
        l0, l1, l2 = lp_ref[0, rows, :], lp_ref[1, rows, :], lp_ref[2, rows, :]
        m = jnp.maximum(jnp.maximum(l0, l1), l2)
        e0, e1, e2 = jnp.exp(l0 - m), jnp.exp(l1 - m), jnp.exp(l2 - m)
        mixed = (e0 * op_ref[0, rows, :] + e1 * op_ref[1, rows, :] + e2 * op_ref[2, rows, :]) / (e0 + e1 + e2)
        o_ref[rows, :] = mixed.astype(o_ref.dtype)


def _attention(x3, w_pairs, bias):
    b, s, d = x3.shape
    kern = functools.partial(_attention_kernel, s=s)
    n_pat = len(DILATED_PATTERNS)
    return pl.pallas_call(
        kern,
        grid=(b, N_PAIRS),
        in_specs=[pl.BlockSpec((None, s, d), lambda i, j: (i, 0, 0)),
                  pl.BlockSpec((d, 3 * ATT_PAIR_W), lambda i, j: (0, j)),
                  pl.BlockSpec((n_pat, N_VARIANTS, 2, ATT_BQ, ATT_KW), lambda i, j: (0, 0, j, 0, 0))],
        out_specs=pl.BlockSpec((None, s, ATT_PAIR_W), lambda i, j: (i, 0, j)),
        out_shape=jax.ShapeDtypeStruct((b, s, ATT_W), BF16),
        scratch_shapes=[pltpu.VMEM((s, d), BF16),
                        pltpu.VMEM((3, s, ATT_PAIR_W), F32),
                        pltpu.VMEM((2, s, ATT_PAIR_W), BF16),
                        pltpu.VMEM((s, ATT_PAIR_W), BF16),
                        pltpu.VMEM((s, ATT_PAIR_W), BF16),
                        pltpu.VMEM((n_pat, s, ATT_PAIR_W), F32),
                        pltpu.VMEM((n_pat, s, ATT_PAIR_W), F32)],
        compiler_params=_cparams(("parallel", "arbitrary")),
        name="attention",
    )(x3, w_pairs, bias)


def _t5_bucket_np(rel):
    half = NUM_BUCKETS // 2
    max_exact = half // 2
    side = np.where(rel > 0, half, 0)
    n = np.abs(rel)
    large = max_exact + (np.log(np.maximum(n, 1).astype(np.float32) / max_exact)
                         / math.log(MAX_DISTANCE / max_exact) * (half - max_exact)).astype(np.int32)
    large = np.minimum(large, half - 1)
    return side + np.where(n < max_exact, n, large)


def _attention_bias(rel_bias, s):
    n_pat = len(DILATED_PATTERNS)
    bucket = np.zeros((n_pat, N_VARIANTS, ATT_BQ, ATT_KW), np.int32)
    inside = np.zeros((n_pat, N_VARIANTS, ATT_BQ, ATT_KW), bool)
    rows = np.arange(ATT_BQ)[:, None]
    cols = np.arange(ATT_KW)[None, :]
    for p, (_, dil) in enumerate(DILATED_PATTERNS):
        n = s // dil
        kw = min(ATT_KW, n)
        offsets = (0,) if n == ATT_BQ else (0, ATT_RADIUS, ATT_BQ)
        for v, off in enumerate(offsets):
            delta = cols - off - rows
            inside[p, v] = (np.abs(delta) <= ATT_RADIUS) & (cols < kw)
            bucket[p, v] = _t5_bucket_np(np.clip(delta, -ATT_RADIUS, ATT_RADIUS) * dil)
    bucket_j = jnp.asarray(bucket)[:, :, None]
    table = rel_bias.astype(F32)
    bias = jnp.zeros((n_pat, N_VARIANTS, ATT_HEADS, ATT_BQ, ATT_KW), F32)
    for bk in range(NUM_BUCKETS):
        bias = jnp.where(bucket_j == bk, table[bk][None, None, :, None, None], bias)
    return jnp.where(jnp.asarray(inside)[:, :, None], bias, NEG_INF)


def _layer_norm(z, g, b):
    mu = jnp.mean(z, axis=-1, keepdims=True)
    d = z - mu
    var = jnp.mean(d * d, axis=-1, keepdims=True)
    return d * lax.rsqrt(var + LN_EPS) * g + b


def _split_bf16(a):
    hi = a.astype(BF16)
    lo = (a - hi.astype(F32)).astype(BF16)
    return hi, lo


def _merge_kernel(ret_ref, att_ref, br_ref, ba_ref, x_ref,
                  wpr_ref, wpa_ref, wout_ref, lng_ref, lnb_ref, wrh_ref, wrl_ref, rb_ref,
                  x1r_ref):
    y_r = jnp.dot(ret_ref[...], wpr_ref[...], preferred_element_type=F32)
    y_a = jnp.dot(att_ref[...], wpa_ref[...], preferred_element_type=F32)
    merged = (jax.nn.sigmoid(br_ref[...].astype(F32)) * y_r
              + jax.nn.sigmoid(ba_ref[...].astype(F32)) * y_a)
    mix = jnp.dot(merged.astype(BF16), wout_ref[...], preferred_element_type=F32)
    x1 = _layer_norm(DEEPNORM_ALPHA * x_ref[...] + mix, lng_ref[...], lnb_ref[...])
    x1r_ref[:, 0:D_MODEL] = x1

    xh, xl = _split_bf16(x1)
    logits = (jnp.dot(xh, wrh_ref[...], preferred_element_type=F32)
              + jnp.dot(xh, wrl_ref[...], preferred_element_type=F32)
              + jnp.dot(xl, wrh_ref[...], preferred_element_type=F32)) + rb_ref[...]
    tm = logits.shape[0]
    lane = lax.broadcasted_iota(jnp.int32, (tm, LANES), 1)
    lane_f = lane.astype(F32)
    big = float(LANES)
    gmask = lane < N_GROUPS
    gl = jnp.where(gmask, logits, -jnp.inf)
    gmax = jnp.max(gl, axis=-1, keepdims=True)
    g_idx = jnp.min(jnp.where(gl == gmax, lane_f, big), axis=-1, keepdims=True)
    g_w = 1.0 / jnp.sum(jnp.where(gmask, jnp.exp(gl - gmax), 0.0), axis=-1, keepdims=True)
    in_group = jnp.floor((lane_f - N_GROUPS) * (1.0 / EXPERTS_PER_GROUP)) == g_idx
    emask = (lane >= N_GROUPS) & (lane < N_GROUPS + N_EXPERTS) & in_group
    el = jnp.where(emask, logits, -jnp.inf)
    v1 = jnp.max(el, axis=-1, keepdims=True)
    i1 = jnp.min(jnp.where(el == v1, lane_f, big), axis=-1, keepdims=True)
    el2 = jnp.where(lane_f == i1, -jnp.inf, el)
    v2 = jnp.max(el2, axis=-1, keepdims=True)
    i2 = jnp.min(jnp.where(el2 == v2, lane_f, big), axis=-1, keepdims=True)
    t = jnp.exp(v2 - v1)
    p1 = 1.0 / (1.0 + t)
    w1 = g_w * p1
    w2 = g_w * (t * p1)
    first_lo = i1 < i2
    e_lo = jnp.where(first_lo, i1, i2) - N_GROUPS
    e_hi = jnp.where(first_lo, i2, i1) - N_GROUPS
    a_loc = e_lo - EXPERTS_PER_GROUP * g_idx
    b_loc = e_hi - EXPERTS_PER_GROUP * g_idx
    pair = a_loc * (2 * EXPERTS_PER_GROUP - 1 - a_loc) * 0.5 + (b_loc - a_loc - 1.0)
    cls = g_idx * PAIRS_PER_GROUP + pair
    route = jnp.where(lane == ROUTE_W_LO, jnp.where(first_lo, w1, w2),
                      jnp.where(lane == ROUTE_W_HI, jnp.where(first_lo, w2, w1),
                                jnp.where(lane == ROUTE_CLS, cls, 0.0)))
    x1r_ref[:, D_MODEL:X1R_COLS] = route


def _merge(ret_act, attn, hcat, x2d, wpr, wpa, wout, lng, lnb, wrh, wrl, rb):
    t = x2d.shape[0]
    tm = min(MERGE_TM, t)
    row = lambda i: (i, 0)
    const = lambda i: (0, 0)
    in_specs = [
        pl.BlockSpec((tm, D_MODEL), row),
        pl.BlockSpec((tm, ATT_W), row),
        pl.BlockSpec((tm, D_MODEL), lambda i: (i, COL_BR // D_MODEL)),
        pl.BlockSpec((tm, D_MODEL), lambda i: (i, COL_BA // D_MODEL)),
        pl.BlockSpec((tm, D_MODEL), row),
        pl.BlockSpec((D_MODEL, D_MODEL), const),
        pl.BlockSpec((ATT_W, D_MODEL), const),
        pl.BlockSpec((D_MODEL, D_MODEL), const),
        pl.BlockSpec((1, D_MODEL), const),
        pl.BlockSpec((1, D_MODEL), const),
        pl.BlockSpec((D_MODEL, LANES), const),
        pl.BlockSpec((D_MODEL, LANES), const),
        pl.BlockSpec((1, LANES), const)]
    return pl.pallas_call(
        _merge_kernel,
        grid=(t // tm,),
        in_specs=in_specs,
        out_specs=pl.BlockSpec((tm, X1R_COLS), row),
        out_shape=jax.ShapeDtypeStruct((t, X1R_COLS), F32),
        compiler_params=_cparams(("parallel",)),
        name="merge_ln1_router",
    )(ret_act, attn, hcat, hcat, x2d, wpr, wpa, wout, lng, lnb, wrh, wrl, rb)


def _row_copies_start(idx_ref, n_rows, src_of, dst_of, sem):
    def start(r):
        row = idx_ref[0, r]
        pltpu.make_async_copy(src_of(r, row), dst_of(r, row), sem).start()

    def group(g, carry):
        for u in range(DMA_ISSUE_UNROLL):
            start(g * DMA_ISSUE_UNROLL + u)
        return carry

    n_groups = n_rows // DMA_ISSUE_UNROLL
    lax.fori_loop(0, n_groups, group, 0)

    def tail(r, carry):
        start(r)
        return carry

    lax.fori_loop(n_groups * DMA_ISSUE_UNROLL, n_rows, tail, 0)


def _rows_wait(buf, slot, n_rows, sem):
    m = buf.shape[1]
    assert m & (m - 1) == 0
    while m >= 1:
        @pl.when((n_rows & m) != 0)
        def _(m=m):
            pltpu.make_async_copy(buf.at[slot, pl.ds(0, m), :], buf.at[slot, pl.ds(0, m), :],
                                  sem.at[slot]).wait()
        m //= 2


def _swiglu(xb, wg_ref, wu_ref, wd_ref):
    gate = jnp.dot(xb, wg_ref[...], preferred_element_type=F32)
    up = jnp.dot(xb, wu_ref[...], preferred_element_type=F32)
    hid = (gate * jax.nn.sigmoid(gate) * up).astype(BF16)
    return jnp.dot(hid, wd_ref[...], preferred_element_type=F32)


def _moe_kernel(ea_ref, eb_ref, nv_ref, tok_ref, tok_next_ref, x_hbm,
                wga_ref, wua_ref, wda_ref, wgb_ref, wub_ref, wdb_ref, lng_ref, lnb_ref,
                out_hbm, xbuf, obuf, gsem, ssem):
    del ea_ref, eb_ref
    j = pl.program_id(0)
    nb = pl.num_programs(0)
    slot = lax.rem(j, 2)
    other = 1 - slot
    nv = nv_ref[j]

    def gather_start(idx_ref, n_rows, slot_):
        _row_copies_start(idx_ref, n_rows,
                          lambda r, row: x_hbm.at[pl.ds(row, 1), :],
                          lambda r, row: xbuf.at[slot_, pl.ds(r, 1), :], gsem.at[slot_])

    @pl.when(j == 0)
    def _():
        xbuf[...] = jnp.zeros_like(xbuf)
        gather_start(tok_ref, nv, 0)

    @pl.when(j + 1 < nb)
    def _():
        gather_start(tok_next_ref, nv_ref[jnp.minimum(j + 1, nb - 1)], other)

    _rows_wait(xbuf, slot, nv, gsem)

    @pl.when(j >= 2)
    def _():
        _rows_wait(obuf, slot, nv_ref[jnp.maximum(j - 2, 0)], ssem)

    @pl.when(nv > 0)
    def _():
        xr = xbuf[slot]
        x1 = xr[:, 0:D_MODEL]
        w_a = xr[:, D_MODEL + ROUTE_W_LO:D_MODEL + ROUTE_W_LO + 1]
        w_b = xr[:, D_MODEL + ROUTE_W_HI:D_MODEL + ROUTE_W_HI + 1]
        xb = x1.astype(BF16)
        ffn = w_a * _swiglu(xb, wga_ref, wua_ref, wda_ref) + w_b * _swiglu(xb, wgb_ref, wub_ref, wdb_ref)
        obuf[slot] = _layer_norm(DEEPNORM_ALPHA * x1 + ffn, lng_ref[...], lnb_ref[...])
        _row_copies_start(tok_ref, nv,
                          lambda r, row: obuf.at[slot, pl.ds(r, 1), :],
                          lambda r, row: out_hbm.at[pl.ds(row, 1), :], ssem.at[slot])

    @pl.when(j == nb - 1)
    def _():
        @pl.when(nb >= 2)
        def _():
            _rows_wait(obuf, other, nv_ref[jnp.maximum(j - 1, 0)], ssem)
        _rows_wait(obuf, slot, nv, ssem)


def _moe(x1r, slot_tok, block_ea, block_eb, block_nv, wg, wu, wd, lng, lnb):
    t = x1r.shape[0]
    n_blocks = block_ea.shape[0]
    r = EXPERT_ROWS
    tok3 = slot_tok.reshape(n_blocks, 1, r)
    const = lambda j, ea, eb, nv: (0, 0)
    grid_spec = pltpu.PrefetchScalarGridSpec(
        num_scalar_prefetch=3,
        grid=(n_blocks,),
        in_specs=[
            pl.BlockSpec((None, 1, r), lambda j, ea, eb, nv: (j, 0, 0), memory_space=pltpu.SMEM),
            pl.BlockSpec((None, 1, r), lambda j, ea, eb, nv: (jnp.minimum(j + 1, n_blocks - 1), 0, 0),
                         memory_space=pltpu.SMEM),
            pl.BlockSpec(memory_space=pl.ANY),
            pl.BlockSpec((None, D_MODEL, D_EXPERT), lambda j, ea, eb, nv: (ea[j], 0, 0)),
            pl.BlockSpec((None, D_MODEL, D_EXPERT), lambda j, ea, eb, nv: (ea[j], 0, 0)),
            pl.BlockSpec((None, D_EXPERT, D_MODEL), lambda j, ea, eb, nv: (ea[j], 0, 0)),
            pl.BlockSpec((None, D_MODEL, D_EXPERT), lambda j, ea, eb, nv: (eb[j], 0, 0)),
            pl.BlockSpec((None, D_MODEL, D_EXPERT), lambda j, ea, eb, nv: (eb[j], 0, 0)),
            pl.BlockSpec((None, D_EXPERT, D_MODEL), lambda j, ea, eb, nv: (eb[j], 0, 0)),
            pl.BlockSpec((1, D_MODEL), const),
            pl.BlockSpec((1, D_MODEL), const),
        ],
        out_specs=pl.BlockSpec(memory_space=pl.ANY),
        scratch_shapes=[pltpu.VMEM((2, r, X1R_COLS), F32), pltpu.VMEM((2, r, D_MODEL), F32),
                        pltpu.SemaphoreType.DMA((2,)), pltpu.SemaphoreType.DMA((2,))],
    )
    return pl.pallas_call(
        _moe_kernel,
        grid_spec=grid_spec,
        out_shape=jax.ShapeDtypeStruct((t, D_MODEL), F32),
        compiler_params=_cparams(("arbitrary",)),
        name="experts_combine_ln2",
    )(block_ea, block_eb, block_nv, tok3, tok3, x1r, wg, wu, wd, wg, wu, wd, lng, lnb)


def _pair_tables():
    ea, eb = [], []
    for g in range(N_GROUPS):
        for a in range(EXPERTS_PER_GROUP):
            for b in range(a + 1, EXPERTS_PER_GROUP):
                ea.append(g * EXPERTS_PER_GROUP + a)
                eb.append(g * EXPERTS_PER_GROUP + b)
    return np.asarray(ea, np.int32), np.asarray(eb, np.int32)


def _routing_plan(cls, n_tokens):
    r = EXPERT_ROWS
    onehot = (cls[:, None] == jnp.arange(N_CLASSES, dtype=jnp.int32)[None, :]).astype(jnp.int32)
    csum = jnp.cumsum(onehot, axis=0)
    counts = csum[-1]
    rank = jnp.sum(csum * onehot, axis=1) - 1
    padded = (counts + r - 1) // r * r
    ends = jnp.cumsum(padded)
    pstart = ends - padded
    dest = (jnp.sum(onehot * pstart[None, :], axis=1) + rank).astype(jnp.int32)
    n_blocks = n_tokens // r + N_CLASSES
    slot_tok = jnp.zeros((n_blocks * r,), jnp.int32).at[dest].set(jnp.arange(n_tokens, dtype=jnp.int32))
    block_start = jnp.arange(n_blocks, dtype=jnp.int32) * r
    block_cls = jnp.minimum(jnp.sum(ends[None, :] <= block_start[:, None], axis=1), N_CLASSES - 1)
    onehot_b = (block_cls[:, None] == jnp.arange(N_CLASSES, dtype=jnp.int32)[None, :]).astype(jnp.int32)
    ea_tab, eb_tab = _pair_tables()
    block_ea = jnp.sum(onehot_b * jnp.asarray(ea_tab)[None, :], axis=1).astype(jnp.int32)
    block_eb = jnp.sum(onehot_b * jnp.asarray(eb_tab)[None, :], axis=1).astype(jnp.int32)
    used = jnp.sum(onehot_b * (pstart + counts)[None, :], axis=1)
    block_nv = jnp.clip(used - block_start, 0, r).astype(jnp.int32)
    return slot_tok, block_ea, block_eb, block_nv


def _rotary_tables(s):
    half = RET_DK // 2
    inv = ROPE_BASE ** (-jnp.arange(half, dtype=F32) / half)
    ang = jnp.arange(s, dtype=F32)[:, None] * inv[None, :]
    cos, sin = jnp.cos(ang), jnp.sin(ang)
    return jnp.concatenate([cos, cos], axis=1), jnp.concatenate([-sin, sin], axis=1)


def _cols(w, off, width):
    return lax.slice_in_dim(w, off, off + width, axis=1)


def _layer(x, w_in, dec_f, dec_b, gn_g, w_proj_ret, rel_bias, w_proj_attn, w_out, ln1_g, ln1_b,
           rw_g, rb_g, rw_e, rb_e, w_gate, w_up, w_down, ln2_g, ln2_b):
    b, s, d = x.shape
    t = b * s
    x2d = x.reshape(t, d)
    for window, dil in DILATED_PATTERNS:
        assert window // (2 * dil) == ATT_RADIUS and (s // dil) % ATT_BQ == 0

    w_main = jnp.concatenate([_cols(w_in, REF_QR, 512), _cols(w_in, REF_KR, 512), _cols(w_in, REF_VR, 1024),
                              _cols(w_in, REF_GR, 1024), _cols(w_in, REF_BR, 1024), _cols(w_in, REF_BA, 1024)],
                             axis=1).astype(BF16)
    hcat = _proj_in(x2d, w_main)
    hcat3 = hcat.reshape(b, s, MAIN_COLS)

    dec = jnp.stack([jax.nn.log_sigmoid(dec_f.astype(F32)), jax.nn.log_sigmoid(dec_b.astype(F32))], axis=0)
    cos_t, sin_t = _rotary_tables(s)
    ret_act = _retention(hcat3, dec, cos_t, sin_t, gn_g.reshape(1, -1).astype(F32)).reshape(t, -1)

    w_pairs = jnp.concatenate(
        [_cols(w_in, off + j * ATT_PAIR_W, ATT_PAIR_W) for j in range(N_PAIRS) for off in (REF_QA, REF_KA, REF_VA)],
        axis=1).astype(BF16)
    attn = _attention(x, w_pairs, _attention_bias(rel_bias, s)).reshape(t, ATT_W)

    wr = jnp.zeros((d, LANES), F32).at[:, :N_GROUPS].set(rw_g.astype(F32))
    wr = wr.at[:, N_GROUPS:N_GROUPS + N_EXPERTS].set(rw_e.astype(F32))
    rb = jnp.zeros((1, LANES), F32).at[0, :N_GROUPS].set(rb_g.astype(F32))
    rb = rb.at[0, N_GROUPS:N_GROUPS + N_EXPERTS].set(rb_e.astype(F32))
    wrh = wr.astype(BF16)
    wrl = (wr - wrh.astype(F32)).astype(BF16)
    x1r = _merge(ret_act, attn, hcat, x2d,
                 w_proj_ret.astype(BF16), w_proj_attn.astype(BF16), w_out.astype(BF16),
                 ln1_g.reshape(1, -1).astype(F32), ln1_b.reshape(1, -1).astype(F32), wrh, wrl, rb)

    cls = x1r[:, D_MODEL + ROUTE_CLS].astype(jnp.int32)
    slot_tok, block_ea, block_eb, block_nv = _routing_plan(cls, t)
    out = _moe(x1r, slot_tok, block_ea, block_eb, block_nv,
               w_gate.astype(BF16), w_up.astype(BF16), w_down.astype(BF16),
               ln2_g.reshape(1, -1).astype(F32), ln2_b.reshape(1, -1).astype(F32))
    return out.reshape(b, s, d)


def kernel(x, w_in, ret_decay_fwd, ret_decay_bwd, ret_gn_g, w_proj_ret, rel_bias, w_proj_attn, w_out,
           ln1_g, ln1_b, router_w_group, router_b_group, router_w_expert, router_b_expert,
           w_gate, w_up, w_down, ln2_g, ln2_b):
    for l in range(DEPTH):
        x = _layer(x, w_in[l], ret_decay_fwd[l], ret_decay_bwd[l], ret_gn_g[l], w_proj_ret[l], rel_bias,
                   w_proj_attn[l], w_out[l], ln1_g[l], ln1_b[l], router_w_group[l], router_b_group[l],
                   router_w_expert[l], router_b_expert[l], w_gate[l], w_up[l], w_down[l],
                   ln2_g[l], ln2_b[l])
    return x
```

```python
import functools
import math

import numpy as np
import jax
import jax.numpy as jnp
from jax import lax
from jax.experimental import pallas as pl
from jax.experimental.pallas import tpu as pltpu

F32 = jnp.float32
BF16 = jnp.bfloat16

D_MODEL = 1024
RET_HEADS = 4
RET_DK = 128
RET_DV = 256
RET_CHUNK = 128
ROPE_BASE = 10000.0
ATT_HEADS = 8
ATT_DH = 64
ATT_W = ATT_HEADS * ATT_DH
DILATED_PATTERNS = ((128, 1), (512, 4), (2048, 16))
NUM_BUCKETS = 32
MAX_DISTANCE = 1024
N_GROUPS = 4
EXPERTS_PER_GROUP = 4
N_EXPERTS = N_GROUPS * EXPERTS_PER_GROUP
TOP_K = 2
D_EXPERT = 512
DEPTH = 1
DEEPNORM_ALPHA = (2.0 * DEPTH) ** 0.25
LN_EPS = 1e-5
NEG_INF = -1e30

REF_QR, REF_KR, REF_VR, REF_GR, REF_QA, REF_KA, REF_VA, REF_BR, REF_BA = (
    0, 512, 1024, 2048, 3072, 3584, 4096, 4608, 5632)
COL_QR, COL_KR, COL_VR, COL_GR, COL_BR, COL_BA = 0, 512, 1024, 2048, 3072, 4096
MAIN_COLS = 5120

V7X_VMEM_LIMIT_BYTES = 56 * 1024 * 1024
LANES = 128

PROJ_TM = 1024
PROJ_TN = 1280
ATT_BQ = 128
ATT_RADIUS = 64
ATT_KW = 2 * ATT_BQ
ATT_PAIR_W = 2 * ATT_DH
N_PAIRS = ATT_HEADS // 2
N_VARIANTS = 3
ATT_UNROLL = 8
MERGE_TM = 512
EXPERT_ROWS = 256
DMA_ISSUE_UNROLL = 8
PAIRS_PER_GROUP = EXPERTS_PER_GROUP * (EXPERTS_PER_GROUP - 1) // 2
N_CLASSES = N_GROUPS * PAIRS_PER_GROUP
X1R_COLS = D_MODEL + LANES
ROUTE_W_LO, ROUTE_W_HI, ROUTE_CLS = 0, 1, 2


def _cparams(sem):
    return pltpu.CompilerParams(dimension_semantics=sem, vmem_limit_bytes=V7X_VMEM_LIMIT_BYTES)


def _proj_in_kernel(x_ref, w_ref, o_ref, xb_ref):
    @pl.when(pl.program_id(1) == 0)
    def _():
        xb_ref[...] = x_ref[...].astype(BF16)

    o_ref[...] = jnp.dot(xb_ref[...], w_ref[...], preferred_element_type=F32).astype(o_ref.dtype)


def _proj_in(x2d, w_bf16):
    t = x2d.shape[0]
    n_cols = w_bf16.shape[1]
    tm = min(PROJ_TM, t)
    return pl.pallas_call(
        _proj_in_kernel,
        grid=(t // tm, n_cols // PROJ_TN),
        in_specs=[pl.BlockSpec((tm, D_MODEL), lambda i, j: (i, 0)),
                  pl.BlockSpec((D_MODEL, PROJ_TN), lambda i, j: (0, j))],
        out_specs=pl.BlockSpec((tm, PROJ_TN), lambda i, j: (i, j)),
        out_shape=jax.ShapeDtypeStruct((t, n_cols), BF16),
        scratch_shapes=[pltpu.VMEM((tm, D_MODEL), BF16)],
        compiler_params=_cparams(("parallel", "arbitrary")),
        name="proj_in",
    )(x2d, w_bf16)


def _retention_kernel(dec_ref, q_ref, k_ref, v_ref, g_ref, cos_ref, sin_ref, gn_ref, o_ref,
                      kr_ref, kvf_ref, kvb_ref, pf_ref, nb_ref, *, n_chunks):
    c_len = RET_CHUNK
    h = pl.program_id(1)
    lgf = dec_ref[0, h]
    lgb = dec_ref[1, h]
    row = lax.broadcasted_iota(jnp.int32, (c_len, c_len), 0).astype(F32)
    col = lax.broadcasted_iota(jnp.int32, (c_len, c_len), 1).astype(F32)
    diff = row - col
    dmat = jnp.where(diff > 0, jnp.exp(lgf * jnp.maximum(diff, 0.0)),
                     jnp.where(diff < 0, jnp.exp(lgb * jnp.maximum(-diff, 0.0)), 2.0))
    zeta_f = jnp.exp(lgf * (c_len - 1.0 - row))
    zeta_b = jnp.exp(lgb * row)
    xi_f = jnp.exp(lgf * (row + 1.0))
    xi_b = jnp.exp(lgb * (c_len - row))
    zero_row = jnp.zeros((1, RET_DV), F32)
    decay_f = jnp.exp(zero_row + lgf * c_len)
    decay_b = jnp.exp(zero_row + lgb * c_len)
    k_scale = RET_DK ** -0.5
    tn_dims = (((0,), (0,)), ((), ()))
    nt_dims = (((1,), (1,)), ((), ()))

    def rows_of(c):
        return pl.ds(pl.multiple_of(c * c_len, c_len), c_len)

    def rotary(t, rows):
        return t * cos_ref[rows, :] + pltpu.roll(t, RET_DK // 2, 1) * sin_ref[rows, :]

    def kv_body(c, carry):
        rows = rows_of(c)
        kr = rotary(k_ref[rows, :].astype(F32), rows) * k_scale
        kr_ref[rows, :] = kr.astype(BF16)
        vb = v_ref[rows, :]
        kvf_ref[c] = lax.dot_general((kr * zeta_f).astype(BF16), vb, tn_dims, preferred_element_type=F32)
        kvb_ref[c] = lax.dot_general((kr * zeta_b).astype(BF16), vb, tn_dims, preferred_element_type=F32)
        return carry

    lax.fori_loop(0, n_chunks, kv_body, 0, unroll=4)

    def scan_f(c, st):
        pf_ref[c] = st.astype(BF16)
        return st * decay_f + kvf_ref[c]

    lax.fori_loop(0, n_chunks, scan_f, jnp.zeros((RET_DK, RET_DV), F32))

    def scan_b(i, st):
        c = n_chunks - 1 - i
        nb_ref[c] = st.astype(BF16)
        return st * decay_b + kvb_ref[c]

    lax.fori_loop(0, n_chunks, scan_b, jnp.zeros((RET_DK, RET_DV), F32))

    def out_body(c, carry):
        rows = rows_of(c)
        qr = rotary(q_ref[rows, :].astype(F32), rows)
        vb = v_ref[rows, :]
        scores = lax.dot_general(qr.astype(BF16), kr_ref[rows, :], nt_dims, preferred_element_type=F32)
        o = jnp.dot((scores * dmat).astype(BF16), vb, preferred_element_type=F32)
        o += jnp.dot((qr * xi_f).astype(BF16), pf_ref[c], preferred_element_type=F32)
        o += jnp.dot((qr * xi_b).astype(BF16), nb_ref[c], preferred_element_type=F32)
        mu = jnp.mean(o, axis=-1, keepdims=True)
        d = o - mu
        var = jnp.mean(d * d, axis=-1, keepdims=True)
        y = d * lax.rsqrt(var + LN_EPS) * gn_ref[...]
        gate = g_ref[rows, :].astype(F32)
        o_ref[rows, :] = (gate * jax.nn.sigmoid(gate) * y).astype(o_ref.dtype)
        return carry

    lax.fori_loop(0, n_chunks, out_body, 0, unroll=4)


def _retention(hcat3, dec, cos_t, sin_t, gn_g):
    b, s, _ = hcat3.shape
    n_chunks = s // RET_CHUNK
    kern = functools.partial(_retention_kernel, n_chunks=n_chunks)
    return pl.pallas_call(
        kern,
        grid=(b, RET_HEADS),
        in_specs=[
            pl.BlockSpec(memory_space=pltpu.SMEM),
            pl.BlockSpec((None, s, RET_DK), lambda i, h: (i, 0, COL_QR // RET_DK + h)),
            pl.BlockSpec((None, s, RET_DK), lambda i, h: (i, 0, COL_KR // RET_DK + h)),
            pl.BlockSpec((None, s, RET_DV), lambda i, h: (i, 0, COL_VR // RET_DV + h)),
            pl.BlockSpec((None, s, RET_DV), lambda i, h: (i, 0, COL_GR // RET_DV + h)),
            pl.BlockSpec((s, RET_DK), lambda i, h: (0, 0)),
            pl.BlockSpec((s, RET_DK), lambda i, h: (0, 0)),
            pl.BlockSpec((1, RET_DV), lambda i, h: (0, h)),
        ],
        out_specs=pl.BlockSpec((None, s, RET_DV), lambda i, h: (i, 0, h)),
        out_shape=jax.ShapeDtypeStruct((b, s, RET_HEADS * RET_DV), BF16),
        scratch_shapes=[pltpu.VMEM((s, RET_DK), BF16),
                        pltpu.VMEM((n_chunks, RET_DK, RET_DV), F32),
                        pltpu.VMEM((n_chunks, RET_DK, RET_DV), F32),
                        pltpu.VMEM((n_chunks, RET_DK, RET_DV), BF16),
                        pltpu.VMEM((n_chunks, RET_DK, RET_DV), BF16)],
        compiler_params=_cparams(("parallel", "arbitrary")),
        name="retention",
    )(dec, hcat3, hcat3, hcat3, hcat3, cos_t, sin_t, gn_g)


def _attention_kernel(x_ref, w_ref, bias_ref, o_ref,
                      xb_ref, qkv_ref, qd_ref, kd_ref, vd_ref, op_ref, lp_ref, s_ref, p_ref, m_ref, *, s):
    nt_dims = (((1,), (1,)), ((), ()))

    @pl.when(pl.program_id(1) == 0)
    def _():
        xb_ref[...] = x_ref[...].astype(BF16)

    vd_ref[:, ATT_PAIR_W:2 * ATT_PAIR_W] = jnp.ones((s, ATT_PAIR_W), BF16)

    proj_rows = 512
    for c in range(s // proj_rows):
        rows = slice(c * proj_rows, (c + 1) * proj_rows)
        acc = jnp.dot(xb_ref[rows, :], w_ref[...], preferred_element_type=F32)
        for t in range(3):
            qkv_ref[t, rows, :] = acc[:, t * ATT_PAIR_W:(t + 1) * ATT_PAIR_W]

    first_head = lax.broadcasted_iota(jnp.int32, (ATT_BQ, ATT_PAIR_W), 1) < ATT_DH

    def pattern(p, dil):
        n = s // dil
        nq = n // ATT_BQ
        kw = min(ATT_KW, n)
        n_blocks = s // ATT_BQ

        copy_rows = min(n, 512)

        def gather(r, carry):
            for c in range(n // copy_rows):
                if dil == 1:
                    src = pl.ds(c * copy_rows, copy_rows)
                else:
                    src = pl.ds(r + dil * c * copy_rows, copy_rows, stride=dil)
                row0 = pl.multiple_of(r * n + c * copy_rows, copy_rows)
                dst = pl.ds(row0, copy_rows)
                qf = qkv_ref[0, src, :] * (ATT_DH ** -0.5)
                keep = lax.broadcasted_iota(jnp.int32, qf.shape, 1) < ATT_DH
                q_first = jnp.where(keep, qf, 0.0).astype(BF16)
                q_second = jnp.where(keep, 0.0, qf).astype(BF16)
                for k in range(copy_rows // ATT_BQ):
                    blk_rows = slice(k * ATT_BQ, (k + 1) * ATT_BQ)
                    tile0 = pl.multiple_of(2 * (row0 + k * ATT_BQ), 2 * ATT_BQ)
                    qd_ref[pl.ds(tile0, ATT_BQ), :] = q_first[blk_rows]
                    qd_ref[pl.ds(tile0 + ATT_BQ, ATT_BQ), :] = q_second[blk_rows]
                kd_ref[dst, :] = qkv_ref[1, src, :].astype(BF16)
                vd_ref[dst, 0:ATT_PAIR_W] = qkv_ref[2, src, :].astype(BF16)
            return carry

        lax.fori_loop(0, dil, gather, 0)

        def place(blk):
            r = blk // nq
            i = blk - r * nq
            q0 = i * ATT_BQ
            if nq == 1:
                kstart, var = 0, 0
            else:
                kstart = jnp.clip(q0 - ATT_RADIUS, 0, n - kw)
                var = jnp.where(i == 0, 0, jnp.where(i == nq - 1, 2, 1))
            k_rows = pl.ds(pl.multiple_of(r * n + kstart, ATT_RADIUS), kw)
            if dil == 1:
                dst = pl.ds(pl.multiple_of(q0, ATT_BQ), ATT_BQ)
            else:
                dst = pl.ds(r + dil * q0, ATT_BQ, stride=dil)
            return k_rows, var, dst

        def tile_rows(t):
            return pl.ds(pl.multiple_of(t * ATT_BQ, ATT_BQ), ATT_BQ)

        def pair_rows(blk):
            return pl.ds(pl.multiple_of(blk * 2 * ATT_BQ, 2 * ATT_BQ), 2 * ATT_BQ)

        def scores(blk, carry):
            k_rows, var, _ = place(blk)
            sc = lax.dot_general(qd_ref[pair_rows(blk), :], kd_ref[k_rows, :], nt_dims,
                                 preferred_element_type=F32)
            for hh in range(2):
                s_ref[tile_rows(2 * blk + hh), 0:kw] = (sc[hh * ATT_BQ:(hh + 1) * ATT_BQ]
                                                        + bias_ref[p, var, hh, :, 0:kw])
            return carry

        lax.fori_loop(0, n_blocks, scores, 0, unroll=ATT_UNROLL)

        def numerators(t, carry):
            rows = tile_rows(t)
            m = jnp.max(s_ref[rows, 0:kw], axis=-1, keepdims=True)
            p_ref[rows, 0:kw] = jnp.exp(s_ref[rows, 0:kw] - m).astype(BF16)
            m_ref[rows, :] = jnp.broadcast_to(m, (ATT_BQ, ATT_PAIR_W))
            return carry

        lax.fori_loop(0, 2 * n_blocks, numerators, 0, unroll=ATT_UNROLL)

        def values(blk, carry):
            k_rows, _, dst = place(blk)
            pv = jnp.dot(p_ref[pair_rows(blk), 0:kw], vd_ref[k_rows, :], preferred_element_type=F32)
            l = pv[:, ATT_PAIR_W:2 * ATT_PAIR_W]
            o = pv[:, 0:ATT_PAIR_W] / l
            lse = m_ref[pair_rows(blk), :] + jnp.log(l)
            op_ref[p, dst, :] = jnp.where(first_head, o[0:ATT_BQ], o[ATT_BQ:2 * ATT_BQ])
            lp_ref[p, dst, :] = jnp.where(first_head, lse[0:ATT_BQ], lse[ATT_BQ:2 * ATT_BQ])
            return carry

        lax.fori_loop(0, n_blocks, values, 0, unroll=ATT_UNROLL)

    for p, (_, dil) in enumerate(DILATED_PATTERNS):
        pattern(p, dil)

    mix_rows = 256
    for c in range(s // mix_rows):
        rows = slice(c * mix_rows, (c + 1) * mix_rows)
        l0, l1, l2 = lp_ref[0, rows, :], lp_ref[1, rows, :], lp_ref[2, rows, :]
        m = jnp.maximum(jnp.maximum(l0, l1), l2)
        e0, e1, e2 = jnp.exp(l0 - m), jnp.exp(l1 - m), jnp.exp(l2 - m)
        mixed = (e0 * op_ref[0, rows, :] + e1 * op_ref[1, rows, :] + e2 * op_ref[2, rows, :]) / (e0 + e1 + e2)
        o_ref[rows, :] = mixed.astype(o_ref.dtype)


def _attention(x3, w_pairs, bias):
    b, s, d = x3.shape
    kern = functools.partial(_attention_kernel, s=s)
    n_pat = len(DILATED_PATTERNS)
    return pl.pallas_call(
        kern,
        grid=(b, N_PAIRS),
        in_specs=[pl.BlockSpec((None, s, d), lambda i, j: (i, 0, 0)),
                  pl.BlockSpec((d, 3 * ATT_PAIR_W), lambda i, j: (0, j)),
                  pl.BlockSpec((n_pat, N_VARIANTS, 2, ATT_BQ, ATT_KW), lambda i, j: (0, 0, j, 0, 0))],
        out_specs=pl.BlockSpec((None, s, ATT_PAIR_W), lambda i, j: (i, 0, j)),
        out_shape=jax.ShapeDtypeStruct((b, s, ATT_W), BF16),
        scratch_shapes=[pltpu.VMEM((s, d), BF16),
                        pltpu.VMEM((3, s, ATT_PAIR_W), F32),
                        pltpu.VMEM((2 * s, ATT_PAIR_W), BF16),
                        pltpu.VMEM((s, ATT_PAIR_W), BF16),
                        pltpu.VMEM((s, 2 * ATT_PAIR_W), BF16),
                        pltpu.VMEM((n_pat, s, ATT_PAIR_W), F32),
                        pltpu.VMEM((n_pat, s, ATT_PAIR_W), F32),
                        pltpu.VMEM((2 * s, ATT_KW), F32),
                        pltpu.VMEM((2 * s, ATT_KW), BF16),
                        pltpu.VMEM((2 * s, ATT_PAIR_W), F32)],
        compiler_params=_cparams(("parallel", "arbitrary")),
        name="attention",
    )(x3, w_pairs, bias)


def _t5_bucket_np(rel):
    half = NUM_BUCKETS // 2
    max_exact = half // 2
    side = np.where(rel > 0, half, 0)
    n = np.abs(rel)
    large = max_exact + (np.log(np.maximum(n, 1).astype(np.float32) / max_exact)
                         / math.log(MAX_DISTANCE / max_exact) * (half - max_exact)).astype(np.int32)
    large = np.minimum(large, half - 1)
    return side + np.where(n < max_exact, n, large)


def _attention_bias(rel_bias, s):
    n_pat = len(DILATED_PATTERNS)
    bucket = np.zeros((n_pat, N_VARIANTS, ATT_BQ, ATT_KW), np.int32)
    inside = np.zeros((n_pat, N_VARIANTS, ATT_BQ, ATT_KW), bool)
    rows = np.arange(ATT_BQ)[:, None]
    cols = np.arange(ATT_KW)[None, :]
    for p, (_, dil) in enumerate(DILATED_PATTERNS):
        n = s // dil
        kw = min(ATT_KW, n)
        offsets = (0,) if n == ATT_BQ else (0, ATT_RADIUS, ATT_BQ)
        for v, off in enumerate(offsets):
            delta = cols - off - rows
            inside[p, v] = (np.abs(delta) <= ATT_RADIUS) & (cols < kw)
            bucket[p, v] = _t5_bucket_np(np.clip(delta, -ATT_RADIUS, ATT_RADIUS) * dil)
    bucket_j = jnp.asarray(bucket)[:, :, None]
    table = rel_bias.astype(F32)
    bias = jnp.zeros((n_pat, N_VARIANTS, ATT_HEADS, ATT_BQ, ATT_KW), F32)
    for bk in range(NUM_BUCKETS):
        bias = jnp.where(bucket_j == bk, table[bk][None, None, :, None, None], bias)
    return jnp.where(jnp.asarray(inside)[:, :, None], bias, NEG_INF)


def _layer_norm(z, g, b):
    mu = jnp.mean(z, axis=-1, keepdims=True)
    d = z - mu
    var = jnp.mean(d * d, axis=-1, keepdims=True)
    return d * lax.rsqrt(var + LN_EPS) * g + b


def _split_bf16(a):
    hi = a.astype(BF16)
    lo = (a - hi.astype(F32)).astype(BF16)
    return hi, lo


def _merge_kernel(ret_ref, att_ref, br_ref, ba_ref, x_ref,
                  wpr_ref, wpa_ref, wout_ref, lng_ref, lnb_ref, wrh_ref, wrl_ref, rb_ref,
                  x1r_ref):
    y_r = jnp.dot(ret_ref[...], wpr_ref[...], preferred_element_type=F32)
    y_a = jnp.dot(att_ref[...], wpa_ref[...], preferred_element_type=F32)
    merged = (jax.nn.sigmoid(br_ref[...].astype(F32)) * y_r
              + jax.nn.sigmoid(ba_ref[...].astype(F32)) * y_a)
    mix = jnp.dot(merged.astype(BF16), wout_ref[...], preferred_element_type=F32)
    x1 = _layer_norm(DEEPNORM_ALPHA * x_ref[...] + mix, lng_ref[...], lnb_ref[...])
    x1r_ref[:, 0:D_MODEL] = x1

    xh, xl = _split_bf16(x1)
    logits = (jnp.dot(xh, wrh_ref[...], preferred_element_type=F32)
              + jnp.dot(xh, wrl_ref[...], preferred_element_type=F32)
              + jnp.dot(xl, wrh_ref[...], preferred_element_type=F32)) + rb_ref[...]
    tm = logits.shape[0]
    lane = lax.broadcasted_iota(jnp.int32, (tm, LANES), 1)
    lane_f = lane.astype(F32)
    big = float(LANES)
    gmask = lane < N_GROUPS
    gl = jnp.where(gmask, logits, -jnp.inf)
    gmax = jnp.max(gl, axis=-1, keepdims=True)
    g_idx = jnp.min(jnp.where(gl == gmax, lane_f, big), axis=-1, keepdims=True)
    g_w = 1.0 / jnp.sum(jnp.where(gmask, jnp.exp(gl - gmax), 0.0), axis=-1, keepdims=True)
    in_group = jnp.floor((lane_f - N_GROUPS) * (1.0 / EXPERTS_PER_GROUP)) == g_idx
    emask = (lane >= N_GROUPS) & (lane < N_GROUPS + N_EXPERTS) & in_group
    el = jnp.where(emask, logits, -jnp.inf)
    v1 = jnp.max(el, axis=-1, keepdims=True)
    i1 = jnp.min(jnp.where(el == v1, lane_f, big), axis=-1, keepdims=True)
    el2 = jnp.where(lane_f == i1, -jnp.inf, el)
    v2 = jnp.max(el2, axis=-1, keepdims=True)
    i2 = jnp.min(jnp.where(el2 == v2, lane_f, big), axis=-1, keepdims=True)
    t = jnp.exp(v2 - v1)
    p1 = 1.0 / (1.0 + t)
    w1 = g_w * p1
    w2 = g_w * (t * p1)
    first_lo = i1 < i2
    e_lo = jnp.where(first_lo, i1, i2) - N_GROUPS
    e_hi = jnp.where(first_lo, i2, i1) - N_GROUPS
    a_loc = e_lo - EXPERTS_PER_GROUP * g_idx
    b_loc = e_hi - EXPERTS_PER_GROUP * g_idx
    pair = a_loc * (2 * EXPERTS_PER_GROUP - 1 - a_loc) * 0.5 + (b_loc - a_loc - 1.0)
    cls = g_idx * PAIRS_PER_GROUP + pair
    route = jnp.where(lane == ROUTE_W_LO, jnp.where(first_lo, w1, w2),
                      jnp.where(lane == ROUTE_W_HI, jnp.where(first_lo, w2, w1),
                                jnp.where(lane == ROUTE_CLS, cls, 0.0)))
    x1r_ref[:, D_MODEL:X1R_COLS] = route


def _merge(ret_act, attn, hcat, x2d, wpr, wpa, wout, lng, lnb, wrh, wrl, rb):
    t = x2d.shape[0]
    tm = min(MERGE_TM, t)
    row = lambda i: (i, 0)
    const = lambda i: (0, 0)
    in_specs = [
        pl.BlockSpec((tm, D_MODEL), row),
        pl.BlockSpec((tm, ATT_W), row),
        pl.BlockSpec((tm, D_MODEL), lambda i: (i, COL_BR // D_MODEL)),
        pl.BlockSpec((tm, D_MODEL), lambda i: (i, COL_BA // D_MODEL)),
        pl.BlockSpec((tm, D_MODEL), row),
        pl.BlockSpec((D_MODEL, D_MODEL), const),
        pl.BlockSpec((ATT_W, D_MODEL), const),
        pl.BlockSpec((D_MODEL, D_MODEL), const),
        pl.BlockSpec((1, D_MODEL), const),
        pl.BlockSpec((1, D_MODEL), const),
        pl.BlockSpec((D_MODEL, LANES), const),
        pl.BlockSpec((D_MODEL, LANES), const),
        pl.BlockSpec((1, LANES), const)]
    return pl.pallas_call(
        _merge_kernel,
        grid=(t // tm,),
        in_specs=in_specs,
        out_specs=pl.BlockSpec((tm, X1R_COLS), row),
        out_shape=jax.ShapeDtypeStruct((t, X1R_COLS), F32),
        compiler_params=_cparams(("parallel",)),
        name="merge_ln1_router",
    )(ret_act, attn, hcat, hcat, x2d, wpr, wpa, wout, lng, lnb, wrh, wrl, rb)


def _row_copies_start(idx_ref, n_rows, src_of, dst_of, sem):
    def start(r):
        row = idx_ref[0, r]
        pltpu.make_async_copy(src_of(r, row), dst_of(r, row), sem).start()

    def group(g, carry):
        for u in range(DMA_ISSUE_UNROLL):
            start(g * DMA_ISSUE_UNROLL + u)
        return carry

    n_groups = n_rows // DMA_ISSUE_UNROLL
    lax.fori_loop(0, n_groups, group, 0)

    def tail(r, carry):
        start(r)
        return carry

    lax.fori_loop(n_groups * DMA_ISSUE_UNROLL, n_rows, tail, 0)


def _rows_wait(buf, slot, n_rows, sem):
    m = buf.shape[1]
    assert m & (m - 1) == 0
    while m >= 1:
        @pl.when((n_rows & m) != 0)
        def _(m=m):
            pltpu.make_async_copy(buf.at[slot, pl.ds(0, m), :], buf.at[slot, pl.ds(0, m), :],
                                  sem.at[slot]).wait()
        m //= 2


def _swiglu(xb, wg_ref, wu_ref, wd_ref):
    gate = jnp.dot(xb, wg_ref[...], preferred_element_type=F32)
    up = jnp.dot(xb, wu_ref[...], preferred_element_type=F32)
    hid = (gate * jax.nn.sigmoid(gate) * up).astype(BF16)
    return jnp.dot(hid, wd_ref[...], preferred_element_type=F32)


def _moe_kernel(ea_ref, eb_ref, nv_ref, tok_ref, tok_next_ref, x_hbm,
                wga_ref, wua_ref, wda_ref, wgb_ref, wub_ref, wdb_ref, lng_ref, lnb_ref,
                out_hbm, xbuf, obuf, gsem, ssem):
    del ea_ref, eb_ref
    j = pl.program_id(0)
    nb = pl.num_programs(0)
    slot = lax.rem(j, 2)
    other = 1 - slot
    nv = nv_ref[j]

    def gather_start(idx_ref, n_rows, slot_):
        _row_copies_start(idx_ref, n_rows,
                          lambda r, row: x_hbm.at[pl.ds(row, 1), :],
                          lambda r, row: xbuf.at[slot_, pl.ds(r, 1), :], gsem.at[slot_])

    @pl.when(j == 0)
    def _():
        xbuf[...] = jnp.zeros_like(xbuf)
        gather_start(tok_ref, nv, 0)

    @pl.when(j + 1 < nb)
    def _():
        gather_start(tok_next_ref, nv_ref[jnp.minimum(j + 1, nb - 1)], other)

    _rows_wait(xbuf, slot, nv, gsem)

    @pl.when(j >= 2)
    def _():
        _rows_wait(obuf, slot, nv_ref[jnp.maximum(j - 2, 0)], ssem)

    @pl.when(nv > 0)
    def _():
        xr = xbuf[slot]
        x1 = xr[:, 0:D_MODEL]
        w_a = xr[:, D_MODEL + ROUTE_W_LO:D_MODEL + ROUTE_W_LO + 1]
        w_b = xr[:, D_MODEL + ROUTE_W_HI:D_MODEL + ROUTE_W_HI + 1]
        xb = x1.astype(BF16)
        ffn = w_a * _swiglu(xb, wga_ref, wua_ref, wda_ref) + w_b * _swiglu(xb, wgb_ref, wub_ref, wdb_ref)
        obuf[slot] = _layer_norm(DEEPNORM_ALPHA * x1 + ffn, lng_ref[...], lnb_ref[...])
        _row_copies_start(tok_ref, nv,
                          lambda r, row: obuf.at[slot, pl.ds(r, 1), :],
                          lambda r, row: out_hbm.at[pl.ds(row, 1), :], ssem.at[slot])

    @pl.when(j == nb - 1)
    def _():
        @pl.when(nb >= 2)
        def _():
            _rows_wait(obuf, other, nv_ref[jnp.maximum(j - 1, 0)], ssem)
        _rows_wait(obuf, slot, nv, ssem)


def _moe(x1r, slot_tok, block_ea, block_eb, block_nv, wg, wu, wd, lng, lnb):
    t = x1r.shape[0]
    n_blocks = block_ea.shape[0]
    r = EXPERT_ROWS
    tok3 = slot_tok.reshape(n_blocks, 1, r)
    const = lambda j, ea, eb, nv: (0, 0)
    grid_spec = pltpu.PrefetchScalarGridSpec(
        num_scalar_prefetch=3,
        grid=(n_blocks,),
        in_specs=[
            pl.BlockSpec((None, 1, r), lambda j, ea, eb, nv: (j, 0, 0), memory_space=pltpu.SMEM),
            pl.BlockSpec((None, 1, r), lambda j, ea, eb, nv: (jnp.minimum(j + 1, n_blocks - 1), 0, 0),
                         memory_space=pltpu.SMEM),
            pl.BlockSpec(memory_space=pl.ANY),
            pl.BlockSpec((None, D_MODEL, D_EXPERT), lambda j, ea, eb, nv: (ea[j], 0, 0)),
            pl.BlockSpec((None, D_MODEL, D_EXPERT), lambda j, ea, eb, nv: (ea[j], 0, 0)),
            pl.BlockSpec((None, D_EXPERT, D_MODEL), lambda j, ea, eb, nv: (ea[j], 0, 0)),
            pl.BlockSpec((None, D_MODEL, D_EXPERT), lambda j, ea, eb, nv: (eb[j], 0, 0)),
            pl.BlockSpec((None, D_MODEL, D_EXPERT), lambda j, ea, eb, nv: (eb[j], 0, 0)),
            pl.BlockSpec((None, D_EXPERT, D_MODEL), lambda j, ea, eb, nv: (eb[j], 0, 0)),
            pl.BlockSpec((1, D_MODEL), const),
            pl.BlockSpec((1, D_MODEL), const),
        ],
        out_specs=pl.BlockSpec(memory_space=pl.ANY),
        scratch_shapes=[pltpu.VMEM((2, r, X1R_COLS), F32), pltpu.VMEM((2, r, D_MODEL), F32),
                        pltpu.SemaphoreType.DMA((2,)), pltpu.SemaphoreType.DMA((2,))],
    )
    return pl.pallas_call(
        _moe_kernel,
        grid_spec=grid_spec,
        out_shape=jax.ShapeDtypeStruct((t, D_MODEL), F32),
        compiler_params=_cparams(("arbitrary",)),
        name="experts_combine_ln2",
    )(block_ea, block_eb, block_nv, tok3, tok3, x1r, wg, wu, wd, wg, wu, wd, lng, lnb)


def _pair_tables():
    ea, eb = [], []
    for g in range(N_GROUPS):
        for a in range(EXPERTS_PER_GROUP):
            for b in range(a + 1, EXPERTS_PER_GROUP):
                ea.append(g * EXPERTS_PER_GROUP + a)
                eb.append(g * EXPERTS_PER_GROUP + b)
    return np.asarray(ea, np.int32), np.asarray(eb, np.int32)


def _routing_plan(cls, n_tokens):
    r = EXPERT_ROWS
    onehot = (cls[:, None] == jnp.arange(N_CLASSES, dtype=jnp.int32)[None, :]).astype(jnp.int32)
    csum = jnp.cumsum(onehot, axis=0)
    counts = csum[-1]
    rank = jnp.sum(csum * onehot, axis=1) - 1
    padded = (counts + r - 1) // r * r
    ends = jnp.cumsum(padded)
    pstart = ends - padded
    dest = (jnp.sum(onehot * pstart[None, :], axis=1) + rank).astype(jnp.int32)
    n_blocks = n_tokens // r + N_CLASSES
    slot_tok = jnp.zeros((n_blocks * r,), jnp.int32).at[dest].set(jnp.arange(n_tokens, dtype=jnp.int32))
    block_start = jnp.arange(n_blocks, dtype=jnp.int32) * r
    block_cls = jnp.minimum(jnp.sum(ends[None, :] <= block_start[:, None], axis=1), N_CLASSES - 1)
    onehot_b = (block_cls[:, None] == jnp.arange(N_CLASSES, dtype=jnp.int32)[None, :]).astype(jnp.int32)
    ea_tab, eb_tab = _pair_tables()
    block_ea = jnp.sum(onehot_b * jnp.asarray(ea_tab)[None, :], axis=1).astype(jnp.int32)
    block_eb = jnp.sum(onehot_b * jnp.asarray(eb_tab)[None, :], axis=1).astype(jnp.int32)
    used = jnp.sum(onehot_b * (pstart + counts)[None, :], axis=1)
    block_nv = jnp.clip(used - block_start, 0, r).astype(jnp.int32)
    return slot_tok, block_ea, block_eb, block_nv


def _rotary_tables(s):
    half = RET_DK // 2
    inv = ROPE_BASE ** (-jnp.arange(half, dtype=F32) / half)
    ang = jnp.arange(s, dtype=F32)[:, None] * inv[None, :]
    cos, sin = jnp.cos(ang), jnp.sin(ang)
    return jnp.concatenate([cos, cos], axis=1), jnp.concatenate([-sin, sin], axis=1)


def _cols(w, off, width):
    return lax.slice_in_dim(w, off, off + width, axis=1)


def _layer(x, w_in, dec_f, dec_b, gn_g, w_proj_ret, rel_bias, w_proj_attn, w_out, ln1_g, ln1_b,
           rw_g, rb_g, rw_e, rb_e, w_gate, w_up, w_down, ln2_g, ln2_b):
    b, s, d = x.shape
    t = b * s
    x2d = x.reshape(t, d)
    for window, dil in DILATED_PATTERNS:
        assert window // (2 * dil) == ATT_RADIUS and (s // dil) % ATT_BQ == 0

    w_main = jnp.concatenate([_cols(w_in, REF_QR, 512), _cols(w_in, REF_KR, 512), _cols(w_in, REF_VR, 1024),
                              _cols(w_in, REF_GR, 1024), _cols(w_in, REF_BR, 1024), _cols(w_in, REF_BA, 1024)],
                             axis=1).astype(BF16)
    hcat = _proj_in(x2d, w_main)
    hcat3 = hcat.reshape(b, s, MAIN_COLS)

    dec = jnp.stack([jax.nn.log_sigmoid(dec_f.astype(F32)), jax.nn.log_sigmoid(dec_b.astype(F32))], axis=0)
    cos_t, sin_t = _rotary_tables(s)
    ret_act = _retention(hcat3, dec, cos_t, sin_t, gn_g.reshape(1, -1).astype(F32)).reshape(t, -1)

    w_pairs = jnp.concatenate(
        [_cols(w_in, off + j * ATT_PAIR_W, ATT_PAIR_W) for j in range(N_PAIRS) for off in (REF_QA, REF_KA, REF_VA)],
        axis=1).astype(BF16)
    attn = _attention(x, w_pairs, _attention_bias(rel_bias, s)).reshape(t, ATT_W)

    wr = jnp.zeros((d, LANES), F32).at[:, :N_GROUPS].set(rw_g.astype(F32))
    wr = wr.at[:, N_GROUPS:N_GROUPS + N_EXPERTS].set(rw_e.astype(F32))
    rb = jnp.zeros((1, LANES), F32).at[0, :N_GROUPS].set(rb_g.astype(F32))
    rb = rb.at[0, N_GROUPS:N_GROUPS + N_EXPERTS].set(rb_e.astype(F32))
    wrh = wr.astype(BF16)
    wrl = (wr - wrh.astype(F32)).astype(BF16)
    x1r = _merge(ret_act, attn, hcat, x2d,
                 w_proj_ret.astype(BF16), w_proj_attn.astype(BF16), w_out.astype(BF16),
                 ln1_g.reshape(1, -1).astype(F32), ln1_b.reshape(1, -1).astype(F32), wrh, wrl, rb)

    cls = x1r[:, D_MODEL + ROUTE_CLS].astype(jnp.int32)
    slot_tok, block_ea, block_eb, block_nv = _routing_plan(cls, t)
    out = _moe(x1r, slot_tok, block_ea, block_eb, block_nv,
               w_gate.astype(BF16), w_up.astype(BF16), w_down.astype(BF16),
               ln2_g.reshape(1, -1).astype(F32), ln2_b.reshape(1, -1).astype(F32))
    return out.reshape(b, s, d)


def kernel(x, w_in, ret_decay_fwd, ret_decay_bwd, ret_gn_g, w_proj_ret, rel_bias, w_proj_attn, w_out,
           ln1_g, ln1_b, router_w_group, router_b_group, router_w_expert, router_b_expert,
           w_gate, w_up, w_down, ln2_g, ln2_b):
    for l in range(DEPTH):
        x = _layer(x, w_in[l], ret_decay_fwd[l], ret_decay_bwd[l], ret_gn_g[l], w_proj_ret[l], rel_bias,
                   w_proj_attn[l], w_out[l], ln1_g[l], ln1_b[l], router_w_group[l], router_b_group[l],
                   router_w_expert[l], router_b_expert[l], w_gate[l], w_up[l], w_down[l],
                   ln2_g[l], ln2_b[l])
    return x
```

```python
import functools
import math

import numpy as np
import jax
import jax.numpy as jnp
from jax import lax
from jax.experimental import pallas as pl
from jax.experimental.pallas import tpu as pltpu

F32 = jnp.float32
BF16 = jnp.bfloat16

D_MODEL = 1024
RET_HEADS = 4
RET_DK = 128
RET_DV = 256
RET_CHUNK = 128
ROPE_BASE = 10000.0
ATT_HEADS = 8
ATT_DH = 64
ATT_W = ATT_HEADS * ATT_DH
DILATED_PATTERNS = ((128, 1), (512, 4), (2048, 16))
NUM_BUCKETS = 32
MAX_DISTANCE = 1024
N_GROUPS = 4
EXPERTS_PER_GROUP = 4
N_EXPERTS = N_GROUPS * EXPERTS_PER_GROUP
TOP_K = 2
D_EXPERT = 512
DEPTH = 1
DEEPNORM_ALPHA = (2.0 * DEPTH) ** 0.25
LN_EPS = 1e-5
NEG_INF = -1e30

REF_QR, REF_KR, REF_VR, REF_GR, REF_QA, REF_KA, REF_VA, REF_BR, REF_BA = (
    0, 512, 1024, 2048, 3072, 3584, 4096, 4608, 5632)
RET_HEAD_COLS = 2 * RET_DK + 2 * RET_DV
COL_BR = RET_HEADS * RET_HEAD_COLS
COL_BA = COL_BR + D_MODEL
MAIN_COLS = COL_BA + D_MODEL

V7X_VMEM_LIMIT_BYTES = 56 * 1024 * 1024
LANES = 128
SUBLANES = 8

RET_UNROLL = 8
PROJ_TM = 1024
PROJ_TN = 1280
ATT_BQ = 128
ATT_RADIUS = 64
ATT_KW = 2 * ATT_BQ
ATT_PAIR_W = 2 * ATT_DH
N_PAIRS = ATT_HEADS // 2
N_VARIANTS = 3
ATT_UNROLL = 8
MERGE_TM = 512
EXPERT_ROWS = 256
PAIRS_PER_GROUP = EXPERTS_PER_GROUP * (EXPERTS_PER_GROUP - 1) // 2
N_CLASSES = N_GROUPS * PAIRS_PER_GROUP
X1R_COLS = D_MODEL + LANES
ROUTE_W_LO, ROUTE_W_HI, ROUTE_CLS = 0, 1, 2


def _cparams(sem):
    return pltpu.CompilerParams(dimension_semantics=sem, vmem_limit_bytes=V7X_VMEM_LIMIT_BYTES)


def _proj_in_kernel(x_ref, w_ref, o_ref, xb_ref):
    @pl.when(pl.program_id(1) == 0)
    def _():
        xb_ref[...] = x_ref[...].astype(BF16)

    o_ref[...] = jnp.dot(xb_ref[...], w_ref[...], preferred_element_type=F32).astype(o_ref.dtype)


def _proj_in(x2d, w_bf16):
    t = x2d.shape[0]
    n_cols = w_bf16.shape[1]
    tm = min(PROJ_TM, t)
    return pl.pallas_call(
        _proj_in_kernel,
        grid=(t // tm, n_cols // PROJ_TN),
        in_specs=[pl.BlockSpec((tm, D_MODEL), lambda i, j: (i, 0)),
                  pl.BlockSpec((D_MODEL, PROJ_TN), lambda i, j: (0, j))],
        out_specs=pl.BlockSpec((tm, PROJ_TN), lambda i, j: (i, j)),
        out_shape=jax.ShapeDtypeStruct((t, n_cols), BF16),
        scratch_shapes=[pltpu.VMEM((tm, D_MODEL), BF16)],
        compiler_params=_cparams(("parallel", "arbitrary")),
        name="proj_in",
    )(x2d, w_bf16)


def _retention_kernel(dec_ref, qkvg_ref, cos_ref, sin_ref, gn_ref, o_ref,
                      kr_ref, kvf_ref, kvb_ref, pf_ref, nb_ref, *, n_chunks):
    c_len = RET_CHUNK
    q_cols = slice(0, RET_DK)
    k_cols = slice(RET_DK, 2 * RET_DK)
    v_cols = slice(2 * RET_DK, 2 * RET_DK + RET_DV)
    g_cols = slice(2 * RET_DK + RET_DV, RET_HEAD_COLS)
    h = pl.program_id(1)
    lgf = dec_ref[0, h]
    lgb = dec_ref[1, h]
    row = lax.broadcasted_iota(jnp.int32, (c_len, c_len), 0).astype(F32)
    col = lax.broadcasted_iota(jnp.int32, (c_len, c_len), 1).astype(F32)
    diff = row - col
    dmat = jnp.where(diff > 0, jnp.exp(lgf * jnp.maximum(diff, 0.0)),
                     jnp.where(diff < 0, jnp.exp(lgb * jnp.maximum(-diff, 0.0)), 2.0))
    zeta_f = jnp.exp(lgf * (c_len - 1.0 - row))
    zeta_b = jnp.exp(lgb * row)
    xi_f = jnp.exp(lgf * (row + 1.0))
    xi_b = jnp.exp(lgb * (c_len - row))
    zero_row = jnp.zeros((1, RET_DV), F32)
    decay_f = jnp.exp(zero_row + lgf * c_len)
    decay_b = jnp.exp(zero_row + lgb * c_len)
    k_scale = RET_DK ** -0.5
    tn_dims = (((0,), (0,)), ((), ()))
    nt_dims = (((1,), (1,)), ((), ()))

    def rows_of(c):
        return pl.ds(pl.multiple_of(c * c_len, c_len), c_len)

    def rotary(t, rows):
        return t * cos_ref[rows, :] + pltpu.roll(t, RET_DK // 2, 1) * sin_ref[rows, :]

    def kv_body(c, carry):
        rows = rows_of(c)
        kr = rotary(qkvg_ref[rows, k_cols].astype(F32), rows) * k_scale
        kr_ref[rows, :] = kr.astype(BF16)
        vb = qkvg_ref[rows, v_cols]
        kvf_ref[c] = lax.dot_general((kr * zeta_f).astype(BF16), vb, tn_dims, preferred_element_type=F32)
        kvb_ref[c] = lax.dot_general((kr * zeta_b).astype(BF16), vb, tn_dims, preferred_element_type=F32)
        return carry

    lax.fori_loop(0, n_chunks, kv_body, 0, unroll=RET_UNROLL)

    def scan_f(c, st):
        pf_ref[c] = st.astype(BF16)
        return st * decay_f + kvf_ref[c]

    lax.fori_loop(0, n_chunks, scan_f, jnp.zeros((RET_DK, RET_DV), F32))

    def scan_b(i, st):
        c = n_chunks - 1 - i
        nb_ref[c] = st.astype(BF16)
        return st * decay_b + kvb_ref[c]

    lax.fori_loop(0, n_chunks, scan_b, jnp.zeros((RET_DK, RET_DV), F32))

    def out_body(c, carry):
        rows = rows_of(c)
        qr = rotary(qkvg_ref[rows, q_cols].astype(F32), rows)
        vb = qkvg_ref[rows, v_cols]
        scores = lax.dot_general(qr.astype(BF16), kr_ref[rows, :], nt_dims, preferred_element_type=F32)
        o = jnp.dot((scores * dmat).astype(BF16), vb, preferred_element_type=F32)
        o += jnp.dot((qr * xi_f).astype(BF16), pf_ref[c], preferred_element_type=F32)
        o += jnp.dot((qr * xi_b).astype(BF16), nb_ref[c], preferred_element_type=F32)
        mu = jnp.mean(o, axis=-1, keepdims=True)
        d = o - mu
        var = jnp.mean(d * d, axis=-1, keepdims=True)
        y = d * lax.rsqrt(var + LN_EPS) * gn_ref[...]
        gate = qkvg_ref[rows, g_cols].astype(F32)
        o_ref[rows, :] = (gate * jax.nn.sigmoid(gate) * y).astype(o_ref.dtype)
        return carry

    lax.fori_loop(0, n_chunks, out_body, 0, unroll=RET_UNROLL)


def _retention(hcat3, dec, cos_t, sin_t, gn_g):
    b, s, _ = hcat3.shape
    n_chunks = s // RET_CHUNK
    kern = functools.partial(_retention_kernel, n_chunks=n_chunks)
    return pl.pallas_call(
        kern,
        grid=(b, RET_HEADS),
        in_specs=[
            pl.BlockSpec(memory_space=pltpu.SMEM),
            pl.BlockSpec((None, s, RET_HEAD_COLS), lambda i, h: (i, 0, h)),
            pl.BlockSpec((s, RET_DK), lambda i, h: (0, 0)),
            pl.BlockSpec((s, RET_DK), lambda i, h: (0, 0)),
            pl.BlockSpec((1, RET_DV), lambda i, h: (0, h)),
        ],
        out_specs=pl.BlockSpec((None, s, RET_DV), lambda i, h: (i, 0, h)),
        out_shape=jax.ShapeDtypeStruct((b, s, RET_HEADS * RET_DV), BF16),
        scratch_shapes=[pltpu.VMEM((s, RET_DK), BF16),
                        pltpu.VMEM((n_chunks, RET_DK, RET_DV), F32),
                        pltpu.VMEM((n_chunks, RET_DK, RET_DV), F32),
                        pltpu.VMEM((n_chunks, RET_DK, RET_DV), BF16),
                        pltpu.VMEM((n_chunks, RET_DK, RET_DV), BF16)],
        compiler_params=_cparams(("parallel", "arbitrary")),
        name="retention",
    )(dec, hcat3, cos_t, sin_t, gn_g)


def _attention_kernel(x_ref, w_ref, bias_ref, o_ref,
                      xb_ref, qkv_ref, qd_ref, kd_ref, vd_ref, op_ref, lp_ref, s_ref, p_ref, m_ref, *, s):
    nt_dims = (((1,), (1,)), ((), ()))

    @pl.when(pl.program_id(1) == 0)
    def _():
        xb_ref[...] = x_ref[...].astype(BF16)

    vd_ref[:, ATT_PAIR_W:2 * ATT_PAIR_W] = jnp.ones((s, ATT_PAIR_W), BF16)

    proj_rows = 512
    for c in range(s // proj_rows):
        rows = slice(c * proj_rows, (c + 1) * proj_rows)
        acc = jnp.dot(xb_ref[rows, :], w_ref[...], preferred_element_type=F32)
        for t in range(3):
            qkv_ref[t, rows, :] = acc[:, t * ATT_PAIR_W:(t + 1) * ATT_PAIR_W]

    first_head = lax.broadcasted_iota(jnp.int32, (ATT_BQ, ATT_PAIR_W), 1) < ATT_DH

    def pattern(p, dil):
        n = s // dil
        nq = n // ATT_BQ
        kw = min(ATT_KW, n)
        n_blocks = s // ATT_BQ

        copy_rows = min(n, 512)

        def gather(r, carry):
            for c in range(n // copy_rows):
                if dil == 1:
                    src = pl.ds(c * copy_rows, copy_rows)
                else:
                    src = pl.ds(r + dil * c * copy_rows, copy_rows, stride=dil)
                row0 = pl.multiple_of(r * n + c * copy_rows, copy_rows)
                dst = pl.ds(row0, copy_rows)
                qf = qkv_ref[0, src, :] * (ATT_DH ** -0.5)
                keep = lax.broadcasted_iota(jnp.int32, qf.shape, 1) < ATT_DH
                q_first = jnp.where(keep, qf, 0.0).astype(BF16)
                q_second = jnp.where(keep, 0.0, qf).astype(BF16)
                for k in range(copy_rows // ATT_BQ):
                    blk_rows = slice(k * ATT_BQ, (k + 1) * ATT_BQ)
                    tile0 = pl.multiple_of(2 * (row0 + k * ATT_BQ), 2 * ATT_BQ)
                    qd_ref[pl.ds(tile0, ATT_BQ), :] = q_first[blk_rows]
                    qd_ref[pl.ds(tile0 + ATT_BQ, ATT_BQ), :] = q_second[blk_rows]
                kd_ref[dst, :] = qkv_ref[1, src, :].astype(BF16)
                vd_ref[dst, 0:ATT_PAIR_W] = qkv_ref[2, src, :].astype(BF16)
            return carry

        lax.fori_loop(0, dil, gather, 0)

        def place(blk):
            r = blk // nq
            i = blk - r * nq
            q0 = i * ATT_BQ
            if nq == 1:
                kstart, var = 0, 0
            else:
                kstart = jnp.clip(q0 - ATT_RADIUS, 0, n - kw)
                var = jnp.where(i == 0, 0, jnp.where(i == nq - 1, 2, 1))
            k_rows = pl.ds(pl.multiple_of(r * n + kstart, ATT_RADIUS), kw)
            if dil == 1:
                dst = pl.ds(pl.multiple_of(q0, ATT_BQ), ATT_BQ)
            else:
                dst = pl.ds(r + dil * q0, ATT_BQ, stride=dil)
            return k_rows, var, dst

        def tile_rows(t):
            return pl.ds(pl.multiple_of(t * ATT_BQ, ATT_BQ), ATT_BQ)

        def pair_rows(blk):
            return pl.ds(pl.multiple_of(blk * 2 * ATT_BQ, 2 * ATT_BQ), 2 * ATT_BQ)

        def scores(blk, carry):
            k_rows, var, _ = place(blk)
            sc = lax.dot_general(qd_ref[pair_rows(blk), :], kd_ref[k_rows, :], nt_dims,
                                 preferred_element_type=F32)
            for hh in range(2):
                s_ref[tile_rows(2 * blk + hh), 0:kw] = (sc[hh * ATT_BQ:(hh + 1) * ATT_BQ]
                                                        + bias_ref[p, var, hh, :, 0:kw])
            return carry

        lax.fori_loop(0, n_blocks, scores, 0, unroll=ATT_UNROLL)

        def numerators(t, carry):
            rows = tile_rows(t)
            m = jnp.max(s_ref[rows, 0:kw], axis=-1, keepdims=True)
            p_ref[rows, 0:kw] = jnp.exp(s_ref[rows, 0:kw] - m).astype(BF16)
            m_ref[rows, :] = jnp.broadcast_to(m, (ATT_BQ, ATT_PAIR_W))
            return carry

        lax.fori_loop(0, 2 * n_blocks, numerators, 0, unroll=ATT_UNROLL)

        def values(blk, carry):
            k_rows, _, dst = place(blk)
            pv = jnp.dot(p_ref[pair_rows(blk), 0:kw], vd_ref[k_rows, :], preferred_element_type=F32)
            l = pv[:, ATT_PAIR_W:2 * ATT_PAIR_W]
            o = pv[:, 0:ATT_PAIR_W] / l
            lse = m_ref[pair_rows(blk), :] + jnp.log(l)
            op_ref[p, dst, :] = jnp.where(first_head, o[0:ATT_BQ], o[ATT_BQ:2 * ATT_BQ])
            lp_ref[p, dst, :] = jnp.where(first_head, lse[0:ATT_BQ], lse[ATT_BQ:2 * ATT_BQ])
            return carry

        lax.fori_loop(0, n_blocks, values, 0, unroll=ATT_UNROLL)

    for p, (_, dil) in enumerate(DILATED_PATTERNS):
        pattern(p, dil)

    mix_rows = 256
    for c in range(s // mix_rows):
        rows = slice(c * mix_rows, (c + 1) * mix_rows)
        l0, l1, l2 = lp_ref[0, rows, :], lp_ref[1, rows, :], lp_ref[2, rows, :]
        m = jnp.maximum(jnp.maximum(l0, l1), l2)
        e0, e1, e2 = jnp.exp(l0 - m), jnp.exp(l1 - m), jnp.exp(l2 - m)
        mixed = (e0 * op_ref[0, rows, :] + e1 * op_ref[1, rows, :] + e2 * op_ref[2, rows, :]) / (e0 + e1 + e2)
        o_ref[rows, :] = mixed.astype(o_ref.dtype)


def _attention(x3, w_pairs, bias):
    b, s, d = x3.shape
    kern = functools.partial(_attention_kernel, s=s)
    n_pat = len(DILATED_PATTERNS)
    return pl.pallas_call(
        kern,
        grid=(b, N_PAIRS),
        in_specs=[pl.BlockSpec((None, s, d), lambda i, j: (i, 0, 0)),
                  pl.BlockSpec((d, 3 * ATT_PAIR_W), lambda i, j: (0, j)),
                  pl.BlockSpec((n_pat, N_VARIANTS, 2, ATT_BQ, ATT_KW), lambda i, j: (0, 0, j, 0, 0))],
        out_specs=pl.BlockSpec((None, s, ATT_PAIR_W), lambda i, j: (i, 0, j)),
        out_shape=jax.ShapeDtypeStruct((b, s, ATT_W), BF16),
        scratch_shapes=[pltpu.VMEM((s, d), BF16),
                        pltpu.VMEM((3, s, ATT_PAIR_W), F32),
                        pltpu.VMEM((2 * s, ATT_PAIR_W), BF16),
                        pltpu.VMEM((s, ATT_PAIR_W), BF16),
                        pltpu.VMEM((s, 2 * ATT_PAIR_W), BF16),
                        pltpu.VMEM((n_pat, s, ATT_PAIR_W), F32),
                        pltpu.VMEM((n_pat, s, ATT_PAIR_W), F32),
                        pltpu.VMEM((2 * s, ATT_KW), F32),
                        pltpu.VMEM((2 * s, ATT_KW), BF16),
                        pltpu.VMEM((2 * s, ATT_PAIR_W), F32)],
        compiler_params=_cparams(("parallel", "arbitrary")),
        name="attention",
    )(x3, w_pairs, bias)


def _t5_bucket_np(rel):
    half = NUM_BUCKETS // 2
    max_exact = half // 2
    side = np.where(rel > 0, half, 0)
    n = np.abs(rel)
    large = max_exact + (np.log(np.maximum(n, 1).astype(np.float32) / max_exact)
                         / math.log(MAX_DISTANCE / max_exact) * (half - max_exact)).astype(np.int32)
    large = np.minimum(large, half - 1)
    return side + np.where(n < max_exact, n, large)


def _attention_bias(rel_bias, s):
    n_pat = len(DILATED_PATTERNS)
    bucket = np.zeros((n_pat, N_VARIANTS, ATT_BQ, ATT_KW), np.int32)
    inside = np.zeros((n_pat, N_VARIANTS, ATT_BQ, ATT_KW), bool)
    rows = np.arange(ATT_BQ)[:, None]
    cols = np.arange(ATT_KW)[None, :]
    for p, (_, dil) in enumerate(DILATED_PATTERNS):
        n = s // dil
        kw = min(ATT_KW, n)
        offsets = (0,) if n == ATT_BQ else (0, ATT_RADIUS, ATT_BQ)
        for v, off in enumerate(offsets):
            delta = cols - off - rows
            inside[p, v] = (np.abs(delta) <= ATT_RADIUS) & (cols < kw)
            bucket[p, v] = _t5_bucket_np(np.clip(delta, -ATT_RADIUS, ATT_RADIUS) * dil)
    bucket_j = jnp.asarray(bucket)[:, :, None]
    table = rel_bias.astype(F32)
    bias = jnp.zeros((n_pat, N_VARIANTS, ATT_HEADS, ATT_BQ, ATT_KW), F32)
    for bk in range(NUM_BUCKETS):
        bias = jnp.where(bucket_j == bk, table[bk][None, None, :, None, None], bias)
    return jnp.where(jnp.asarray(inside)[:, :, None], bias, NEG_INF)


def _layer_norm(z, g, b):
    mu = jnp.mean(z, axis=-1, keepdims=True)
    d = z - mu
    var = jnp.mean(d * d, axis=-1, keepdims=True)
    return d * lax.rsqrt(var + LN_EPS) * g + b


def _split_bf16(a):
    hi = a.astype(BF16)
    lo = (a - hi.astype(F32)).astype(BF16)
    return hi, lo


def _merge_kernel(ret_ref, att_ref, br_ref, ba_ref, x_ref,
                  wpr_ref, wpa_ref, wout_ref, lng_ref, lnb_ref, wrh_ref, wrl_ref, rb_ref,
                  x1r_ref):
    y_r = jnp.dot(ret_ref[...], wpr_ref[...], preferred_element_type=F32)
    y_a = jnp.dot(att_ref[...], wpa_ref[...], preferred_element_type=F32)
    merged = (jax.nn.sigmoid(br_ref[...].astype(F32)) * y_r
              + jax.nn.sigmoid(ba_ref[...].astype(F32)) * y_a)
    mix = jnp.dot(merged.astype(BF16), wout_ref[...], preferred_element_type=F32)
    x1 = _layer_norm(DEEPNORM_ALPHA * x_ref[...] + mix, lng_ref[...], lnb_ref[...])
    x1r_ref[:, 0:D_MODEL] = x1

    xh, xl = _split_bf16(x1)
    logits = (jnp.dot(xh, wrh_ref[...], preferred_element_type=F32)
              + jnp.dot(xh, wrl_ref[...], preferred_element_type=F32)
              + jnp.dot(xl, wrh_ref[...], preferred_element_type=F32)) + rb_ref[...]
    tm = logits.shape[0]
    lane = lax.broadcasted_iota(jnp.int32, (tm, LANES), 1)
    lane_f = lane.astype(F32)
    big = float(LANES)
    gmask = lane < N_GROUPS
    gl = jnp.where(gmask, logits, -jnp.inf)
    gmax = jnp.max(gl, axis=-1, keepdims=True)
    g_idx = jnp.min(jnp.where(gl == gmax, lane_f, big), axis=-1, keepdims=True)
    g_w = 1.0 / jnp.sum(jnp.where(gmask, jnp.exp(gl - gmax), 0.0), axis=-1, keepdims=True)
    in_group = jnp.floor((lane_f - N_GROUPS) * (1.0 / EXPERTS_PER_GROUP)) == g_idx
    emask = (lane >= N_GROUPS) & (lane < N_GROUPS + N_EXPERTS) & in_group
    el = jnp.where(emask, logits, -jnp.inf)
    v1 = jnp.max(el, axis=-1, keepdims=True)
    i1 = jnp.min(jnp.where(el == v1, lane_f, big), axis=-1, keepdims=True)
    el2 = jnp.where(lane_f == i1, -jnp.inf, el)
    v2 = jnp.max(el2, axis=-1, keepdims=True)
    i2 = jnp.min(jnp.where(el2 == v2, lane_f, big), axis=-1, keepdims=True)
    t = jnp.exp(v2 - v1)
    p1 = 1.0 / (1.0 + t)
    w1 = g_w * p1
    w2 = g_w * (t * p1)
    first_lo = i1 < i2
    e_lo = jnp.where(first_lo, i1, i2) - N_GROUPS
    e_hi = jnp.where(first_lo, i2, i1) - N_GROUPS
    a_loc = e_lo - EXPERTS_PER_GROUP * g_idx
    b_loc = e_hi - EXPERTS_PER_GROUP * g_idx
    pair = a_loc * (2 * EXPERTS_PER_GROUP - 1 - a_loc) * 0.5 + (b_loc - a_loc - 1.0)
    cls = g_idx * PAIRS_PER_GROUP + pair
    route = jnp.where(lane == ROUTE_W_LO, jnp.where(first_lo, w1, w2),
                      jnp.where(lane == ROUTE_W_HI, jnp.where(first_lo, w2, w1),
                                jnp.where(lane == ROUTE_CLS, cls, 0.0)))
    x1r_ref[:, D_MODEL:X1R_COLS] = route


def _merge(ret_act, attn, hcat, x2d, wpr, wpa, wout, lng, lnb, wrh, wrl, rb):
    t = x2d.shape[0]
    tm = min(MERGE_TM, t)
    row = lambda i: (i, 0)
    const = lambda i: (0, 0)
    in_specs = [
        pl.BlockSpec((tm, D_MODEL), row),
        pl.BlockSpec((tm, ATT_W), row),
        pl.BlockSpec((tm, D_MODEL), lambda i: (i, COL_BR // D_MODEL)),
        pl.BlockSpec((tm, D_MODEL), lambda i: (i, COL_BA // D_MODEL)),
        pl.BlockSpec((tm, D_MODEL), row),
        pl.BlockSpec((D_MODEL, D_MODEL), const),
        pl.BlockSpec((ATT_W, D_MODEL), const),
        pl.BlockSpec((D_MODEL, D_MODEL), const),
        pl.BlockSpec((1, D_MODEL), const),
        pl.BlockSpec((1, D_MODEL), const),
        pl.BlockSpec((D_MODEL, LANES), const),
        pl.BlockSpec((D_MODEL, LANES), const),
        pl.BlockSpec((1, LANES), const)]
    return pl.pallas_call(
        _merge_kernel,
        grid=(t // tm,),
        in_specs=in_specs,
        out_specs=pl.BlockSpec((tm, X1R_COLS), row),
        out_shape=jax.ShapeDtypeStruct((t, X1R_COLS), F32),
        compiler_params=_cparams(("parallel",)),
        name="merge_ln1_router",
    )(ret_act, attn, hcat, hcat, x2d, wpr, wpa, wout, lng, lnb, wrh, wrl, rb)


def _row_copies_start(idx_ref, n_rows, copy_of, sem):
    def start(g, u, r):
        src, dst = copy_of(g, u, idx_ref[0, r])
        pltpu.make_async_copy(src, dst, sem).start()

    def group(g, carry):
        for u in range(SUBLANES):
            start(g, u, g * SUBLANES + u)
        return carry

    n_groups = n_rows // SUBLANES
    lax.fori_loop(0, n_groups, group, 0)

    def tail(r, carry):
        start(n_groups, r - n_groups * SUBLANES, r)
        return carry

    lax.fori_loop(n_groups * SUBLANES, n_rows, tail, 0)


def _rows_wait(buf, slot, n_rows, sem):
    m = buf.shape[1] * SUBLANES
    assert m & (m - 1) == 0
    while m >= 1:
        @pl.when((n_rows & m) != 0)
        def _(m=m):
            if m >= SUBLANES:
                view = buf.at[slot, pl.ds(0, m // SUBLANES)]
            else:
                view = buf.at[slot, 0, pl.ds(0, m)]
            pltpu.make_async_copy(view, view, sem.at[slot]).wait()
        m //= 2


def _swiglu(xb, wg_ref, wu_ref, wd_ref):
    gate = jnp.dot(xb, wg_ref[...], preferred_element_type=F32)
    up = jnp.dot(xb, wu_ref[...], preferred_element_type=F32)
    hid = (gate * jax.nn.sigmoid(gate) * up).astype(BF16)
    return jnp.dot(hid, wd_ref[...], preferred_element_type=F32)


def _moe_kernel(ea_ref, eb_ref, nv_ref, tok_ref, tok_next_ref, x_hbm,
                wga_ref, wua_ref, wda_ref, wgb_ref, wub_ref, wdb_ref, lng_ref, lnb_ref,
                out_hbm, xbuf, obuf, gsem, ssem):
    del ea_ref, eb_ref
    j = pl.program_id(0)
    nb = pl.num_programs(0)
    slot = lax.rem(j, 2)
    other = 1 - slot
    nv = nv_ref[j]

    def gather_start(idx_ref, n_rows, slot_):
        _row_copies_start(idx_ref, n_rows,
                          lambda g, u, row: (x_hbm.at[pl.ds(row, 1), :], xbuf.at[slot_, g, pl.ds(u, 1), :]),
                          gsem.at[slot_])

    @pl.when(j == 0)
    def _():
        xbuf[...] = jnp.zeros_like(xbuf)
        gather_start(tok_ref, nv, 0)

    @pl.when(j + 1 < nb)
    def _():
        gather_start(tok_next_ref, nv_ref[jnp.minimum(j + 1, nb - 1)], other)

    _rows_wait(xbuf, slot, nv, gsem)

    @pl.when(j >= 2)
    def _():
        _rows_wait(obuf, slot, nv_ref[jnp.maximum(j - 2, 0)], ssem)

    @pl.when(nv > 0)
    def _():
        xr = xbuf[slot].reshape(EXPERT_ROWS, X1R_COLS)
        x1 = xr[:, 0:D_MODEL]
        w_a = xr[:, D_MODEL + ROUTE_W_LO:D_MODEL + ROUTE_W_LO + 1]
        w_b = xr[:, D_MODEL + ROUTE_W_HI:D_MODEL + ROUTE_W_HI + 1]
        xb = x1.astype(BF16)
        ffn = w_a * _swiglu(xb, wga_ref, wua_ref, wda_ref) + w_b * _swiglu(xb, wgb_ref, wub_ref, wdb_ref)
        out = _layer_norm(DEEPNORM_ALPHA * x1 + ffn, lng_ref[...], lnb_ref[...])
        obuf[slot] = out.reshape(EXPERT_ROWS // SUBLANES, SUBLANES, D_MODEL)
        _row_copies_start(tok_ref, nv,
                          lambda g, u, row: (obuf.at[slot, g, pl.ds(u, 1), :], out_hbm.at[pl.ds(row, 1), :]),
                          ssem.at[slot])

    @pl.when(j == nb - 1)
    def _():
        @pl.when(nb >= 2)
        def _():
            _rows_wait(obuf, other, nv_ref[jnp.maximum(j - 1, 0)], ssem)
        _rows_wait(obuf, slot, nv, ssem)


def _moe(x1r, slot_tok, block_ea, block_eb, block_nv, wg, wu, wd, lng, lnb):
    t = x1r.shape[0]
    n_blocks = block_ea.shape[0]
    r = EXPERT_ROWS
    tok3 = slot_tok.reshape(n_blocks, 1, r)
    const = lambda j, ea, eb, nv: (0, 0)
    grid_spec = pltpu.PrefetchScalarGridSpec(
        num_scalar_prefetch=3,
        grid=(n_blocks,),
        in_specs=[
            pl.BlockSpec((None, 1, r), lambda j, ea, eb, nv: (j, 0, 0), memory_space=pltpu.SMEM),
            pl.BlockSpec((None, 1, r), lambda j, ea, eb, nv: (jnp.minimum(j + 1, n_blocks - 1), 0, 0),
                         memory_space=pltpu.SMEM),
            pl.BlockSpec(memory_space=pl.ANY),
            pl.BlockSpec((None, D_MODEL, D_EXPERT), lambda j, ea, eb, nv: (ea[j], 0, 0)),
            pl.BlockSpec((None, D_MODEL, D_EXPERT), lambda j, ea, eb, nv: (ea[j], 0, 0)),
            pl.BlockSpec((None, D_EXPERT, D_MODEL), lambda j, ea, eb, nv: (ea[j], 0, 0)),
            pl.BlockSpec((None, D_MODEL, D_EXPERT), lambda j, ea, eb, nv: (eb[j], 0, 0)),
            pl.BlockSpec((None, D_MODEL, D_EXPERT), lambda j, ea, eb, nv: (eb[j], 0, 0)),
            pl.BlockSpec((None, D_EXPERT, D_MODEL), lambda j, ea, eb, nv: (eb[j], 0, 0)),
            pl.BlockSpec((1, D_MODEL), const),
            pl.BlockSpec((1, D_MODEL), const),
        ],
        out_specs=pl.BlockSpec(memory_space=pl.ANY),
        scratch_shapes=[pltpu.VMEM((2, r // SUBLANES, SUBLANES, X1R_COLS), F32),
                        pltpu.VMEM((2, r // SUBLANES, SUBLANES, D_MODEL), F32),
                        pltpu.SemaphoreType.DMA((2,)), pltpu.SemaphoreType.DMA((2,))],
    )
    return pl.pallas_call(
        _moe_kernel,
        grid_spec=grid_spec,
        out_shape=jax.ShapeDtypeStruct((t, D_MODEL), F32),
        compiler_params=_cparams(("arbitrary",)),
        name="experts_combine_ln2",
    )(block_ea, block_eb, block_nv, tok3, tok3, x1r, wg, wu, wd, wg, wu, wd, lng, lnb)


def _pair_tables():
    ea, eb = [], []
    for g in range(N_GROUPS):
        for a in range(EXPERTS_PER_GROUP):
            for b in range(a + 1, EXPERTS_PER_GROUP):
                ea.append(g * EXPERTS_PER_GROUP + a)
                eb.append(g * EXPERTS_PER_GROUP + b)
    return np.asarray(ea, np.int32), np.asarray(eb, np.int32)


def _routing_plan(cls, n_tokens):
    r = EXPERT_ROWS
    onehot = (cls[:, None] == jnp.arange(N_CLASSES, dtype=jnp.int32)[None, :]).astype(jnp.int32)
    csum = jnp.cumsum(onehot, axis=0)
    counts = csum[-1]
    rank = jnp.sum(csum * onehot, axis=1) - 1
    padded = (counts + r - 1) // r * r
    ends = jnp.cumsum(padded)
    pstart = ends - padded
    dest = (jnp.sum(onehot * pstart[None, :], axis=1) + rank).astype(jnp.int32)
    n_blocks = n_tokens // r + N_CLASSES
    slot_tok = jnp.zeros((n_blocks * r,), jnp.int32).at[dest].set(jnp.arange(n_tokens, dtype=jnp.int32))
    block_start = jnp.arange(n_blocks, dtype=jnp.int32) * r
    block_cls = jnp.minimum(jnp.sum(ends[None, :] <= block_start[:, None], axis=1), N_CLASSES - 1)
    onehot_b = (block_cls[:, None] == jnp.arange(N_CLASSES, dtype=jnp.int32)[None, :]).astype(jnp.int32)
    ea_tab, eb_tab = _pair_tables()
    block_ea = jnp.sum(onehot_b * jnp.asarray(ea_tab)[None, :], axis=1).astype(jnp.int32)
    block_eb = jnp.sum(onehot_b * jnp.asarray(eb_tab)[None, :], axis=1).astype(jnp.int32)
    used = jnp.sum(onehot_b * (pstart + counts)[None, :], axis=1)
    block_nv = jnp.clip(used - block_start, 0, r).astype(jnp.int32)
    return slot_tok, block_ea, block_eb, block_nv


def _rotary_tables(s):
    half = RET_DK // 2
    inv = ROPE_BASE ** (-jnp.arange(half, dtype=F32) / half)
    ang = jnp.arange(s, dtype=F32)[:, None] * inv[None, :]
    cos, sin = jnp.cos(ang), jnp.sin(ang)
    return jnp.concatenate([cos, cos], axis=1), jnp.concatenate([-sin, sin], axis=1)


def _cols(w, off, width):
    return lax.slice_in_dim(w, off, off + width, axis=1)


def _layer(x, w_in, dec_f, dec_b, gn_g, w_proj_ret, rel_bias, w_proj_attn, w_out, ln1_g, ln1_b,
           rw_g, rb_g, rw_e, rb_e, w_gate, w_up, w_down, ln2_g, ln2_b):
    b, s, d = x.shape
    t = b * s
    x2d = x.reshape(t, d)
    for window, dil in DILATED_PATTERNS:
        assert window // (2 * dil) == ATT_RADIUS and (s // dil) % ATT_BQ == 0

    head_cols = [_cols(w_in, off + h * width, width) for h in range(RET_HEADS)
                 for off, width in ((REF_QR, RET_DK), (REF_KR, RET_DK), (REF_VR, RET_DV), (REF_GR, RET_DV))]
    w_main = jnp.concatenate(head_cols + [_cols(w_in, REF_BR, D_MODEL), _cols(w_in, REF_BA, D_MODEL)],
                             axis=1).astype(BF16)
    hcat = _proj_in(x2d, w_main)
    hcat3 = hcat.reshape(b, s, MAIN_COLS)

    dec = jnp.stack([jax.nn.log_sigmoid(dec_f.astype(F32)), jax.nn.log_sigmoid(dec_b.astype(F32))], axis=0)
    cos_t, sin_t = _rotary_tables(s)
    ret_act = _retention(hcat3, dec, cos_t, sin_t, gn_g.reshape(1, -1).astype(F32)).reshape(t, -1)

    w_pairs = jnp.concatenate(
        [_cols(w_in, off + j * ATT_PAIR_W, ATT_PAIR_W) for j in range(N_PAIRS) for off in (REF_QA, REF_KA, REF_VA)],
        axis=1).astype(BF16)
    attn = _attention(x, w_pairs, _attention_bias(rel_bias, s)).reshape(t, ATT_W)

    wr = jnp.zeros((d, LANES), F32).at[:, :N_GROUPS].set(rw_g.astype(F32))
    wr = wr.at[:, N_GROUPS:N_GROUPS + N_EXPERTS].set(rw_e.astype(F32))
    rb = jnp.zeros((1, LANES), F32).at[0, :N_GROUPS].set(rb_g.astype(F32))
    rb = rb.at[0, N_GROUPS:N_GROUPS + N_EXPERTS].set(rb_e.astype(F32))
    wrh = wr.astype(BF16)
    wrl = (wr - wrh.astype(F32)).astype(BF16)
    x1r = _merge(ret_act, attn, hcat, x2d,
                 w_proj_ret.astype(BF16), w_proj_attn.astype(BF16), w_out.astype(BF16),
                 ln1_g.reshape(1, -1).astype(F32), ln1_b.reshape(1, -1).astype(F32), wrh, wrl, rb)

    cls = x1r[:, D_MODEL + ROUTE_CLS].astype(jnp.int32)
    slot_tok, block_ea, block_eb, block_nv = _routing_plan(cls, t)
    out = _moe(x1r, slot_tok, block_ea, block_eb, block_nv,
               w_gate.astype(BF16), w_up.astype(BF16), w_down.astype(BF16),
               ln2_g.reshape(1, -1).astype(F32), ln2_b.reshape(1, -1).astype(F32))
    return out.reshape(b, s, d)


def kernel(x, w_in, ret_decay_fwd, ret_decay_bwd, ret_gn_g, w_proj_ret, rel_bias, w_proj_attn, w_out,
           ln1_g, ln1_b, router_w_group, router_b_group, router_w_expert, router_b_expert,
           w_gate, w_up, w_down, ln2_g, ln2_b):
    for l in range(DEPTH):
        x = _layer(x, w_in[l], ret_decay_fwd[l], ret_decay_bwd[l], ret_gn_g[l], w_proj_ret[l], rel_bias,
                   w_proj_attn[l], w_out[l], ln1_g[l], ln1_b[l], router_w_group[l], router_b_group[l],
                   router_w_expert[l], router_b_expert[l], w_gate[l], w_up[l], w_down[l],
                   ln2_g[l], ln2_b[l])
    return x
```

```python
import functools
import math

import numpy as np
import jax
import jax.numpy as jnp
from jax import lax
from jax.experimental import pallas as pl
from jax.experimental.pallas import tpu as pltpu

F32 = jnp.float32
BF16 = jnp.bfloat16

D_MODEL = 1024
RET_HEADS = 4
RET_DK = 128
RET_DV = 256
RET_CHUNK = 128
ROPE_BASE = 10000.0
ATT_HEADS = 8
ATT_DH = 64
ATT_W = ATT_HEADS * ATT_DH
DILATED_PATTERNS = ((128, 1), (512, 4), (2048, 16))
NUM_BUCKETS = 32
MAX_DISTANCE = 1024
N_GROUPS = 4
EXPERTS_PER_GROUP = 4
N_EXPERTS = N_GROUPS * EXPERTS_PER_GROUP
TOP_K = 2
D_EXPERT = 512
DEPTH = 1
DEEPNORM_ALPHA = (2.0 * DEPTH) ** 0.25
LN_EPS = 1e-5
NEG_INF = -1e30

REF_QR, REF_KR, REF_VR, REF_GR, REF_QA, REF_KA, REF_VA, REF_BR, REF_BA = (
    0, 512, 1024, 2048, 3072, 3584, 4096, 4608, 5632)
RET_HEAD_COLS = 2 * RET_DK + 2 * RET_DV
COL_BR = RET_HEADS * RET_HEAD_COLS
COL_BA = COL_BR + D_MODEL
MAIN_COLS = COL_BA + D_MODEL

V7X_VMEM_LIMIT_BYTES = 56 * 1024 * 1024
LANES = 128
SUBLANES = 8

RET_UNROLL = 8
PROJ_TM = 1024
PROJ_TN = 1280
ATT_BQ = 128
ATT_RADIUS = 64
ATT_KW = 2 * ATT_BQ
ATT_PAIR_W = 2 * ATT_DH
N_PAIRS = ATT_HEADS // 2
N_VARIANTS = 3
ATT_UNROLL = 16
MERGE_TM = 512
EXPERT_ROWS = 256
PAIRS_PER_GROUP = EXPERTS_PER_GROUP * (EXPERTS_PER_GROUP - 1) // 2
N_CLASSES = N_GROUPS * PAIRS_PER_GROUP
X1R_COLS = D_MODEL + LANES
ROUTE_W_LO, ROUTE_W_HI, ROUTE_CLS = 0, 1, 2


def _cparams(sem):
    return pltpu.CompilerParams(dimension_semantics=sem, vmem_limit_bytes=V7X_VMEM_LIMIT_BYTES)


def _proj_in_kernel(x_ref, w_ref, o_ref, xb_ref):
    @pl.when(pl.program_id(1) == 0)
    def _():
        xb_ref[...] = x_ref[...].astype(BF16)

    o_ref[...] = jnp.dot(xb_ref[...], w_ref[...], preferred_element_type=F32).astype(o_ref.dtype)


def _proj_in(x2d, w_bf16):
    t = x2d.shape[0]
    n_cols = w_bf16.shape[1]
    tm = min(PROJ_TM, t)
    return pl.pallas_call(
        _proj_in_kernel,
        grid=(t // tm, n_cols // PROJ_TN),
        in_specs=[pl.BlockSpec((tm, D_MODEL), lambda i, j: (i, 0)),
                  pl.BlockSpec((D_MODEL, PROJ_TN), lambda i, j: (0, j))],
        out_specs=pl.BlockSpec((tm, PROJ_TN), lambda i, j: (i, j)),
        out_shape=jax.ShapeDtypeStruct((t, n_cols), BF16),
        scratch_shapes=[pltpu.VMEM((tm, D_MODEL), BF16)],
        compiler_params=_cparams(("parallel", "arbitrary")),
        name="proj_in",
    )(x2d, w_bf16)


def _retention_kernel(dec_ref, qkvg_ref, cos_ref, sin_ref, gn_ref, o_ref,
                      kr_ref, kvf_ref, kvb_ref, pf_ref, nb_ref, *, n_chunks):
    c_len = RET_CHUNK
    q_cols = slice(0, RET_DK)
    k_cols = slice(RET_DK, 2 * RET_DK)
    v_cols = slice(2 * RET_DK, 2 * RET_DK + RET_DV)
    g_cols = slice(2 * RET_DK + RET_DV, RET_HEAD_COLS)
    h = pl.program_id(1)
    lgf = dec_ref[0, h]
    lgb = dec_ref[1, h]
    row = lax.broadcasted_iota(jnp.int32, (c_len, c_len), 0).astype(F32)
    col = lax.broadcasted_iota(jnp.int32, (c_len, c_len), 1).astype(F32)
    diff = row - col
    dmat = jnp.where(diff > 0, jnp.exp(lgf * jnp.maximum(diff, 0.0)),
                     jnp.where(diff < 0, jnp.exp(lgb * jnp.maximum(-diff, 0.0)), 2.0))
    zeta_f = jnp.exp(lgf * (c_len - 1.0 - row))
    zeta_b = jnp.exp(lgb * row)
    xi_f = jnp.exp(lgf * (row + 1.0))
    xi_b = jnp.exp(lgb * (c_len - row))
    zero_row = jnp.zeros((1, RET_DV), F32)
    decay_f = jnp.exp(zero_row + lgf * c_len)
    decay_b = jnp.exp(zero_row + lgb * c_len)
    k_scale = RET_DK ** -0.5
    tn_dims = (((0,), (0,)), ((), ()))
    nt_dims = (((1,), (1,)), ((), ()))

    def rows_of(c):
        return pl.ds(pl.multiple_of(c * c_len, c_len), c_len)

    def rotary(t, rows):
        return t * cos_ref[rows, :] + pltpu.roll(t, RET_DK // 2, 1) * sin_ref[rows, :]

    def kv_body(c, carry):
        rows = rows_of(c)
        kr = rotary(qkvg_ref[rows, k_cols].astype(F32), rows) * k_scale
        kr_ref[rows, :] = kr.astype(BF16)
        vb = qkvg_ref[rows, v_cols]
        kvf_ref[c] = lax.dot_general((kr * zeta_f).astype(BF16), vb, tn_dims, preferred_element_type=F32)
        kvb_ref[c] = lax.dot_general((kr * zeta_b).astype(BF16), vb, tn_dims, preferred_element_type=F32)
        return carry

    lax.fori_loop(0, n_chunks, kv_body, 0, unroll=RET_UNROLL)

    def scan_f(c, st):
        pf_ref[c] = st.astype(BF16)
        return st * decay_f + kvf_ref[c]

    lax.fori_loop(0, n_chunks, scan_f, jnp.zeros((RET_DK, RET_DV), F32))

    def scan_b(i, st):
        c = n_chunks - 1 - i
        nb_ref[c] = st.astype(BF16)
        return st * decay_b + kvb_ref[c]

    lax.fori_loop(0, n_chunks, scan_b, jnp.zeros((RET_DK, RET_DV), F32))

    def out_body(c, carry):
        rows = rows_of(c)
        qr = rotary(qkvg_ref[rows, q_cols].astype(F32), rows)
        vb = qkvg_ref[rows, v_cols]
        scores = lax.dot_general(qr.astype(BF16), kr_ref[rows, :], nt_dims, preferred_element_type=F32)
        o = jnp.dot((scores * dmat).astype(BF16), vb, preferred_element_type=F32)
        o += jnp.dot((qr * xi_f).astype(BF16), pf_ref[c], preferred_element_type=F32)
        o += jnp.dot((qr * xi_b).astype(BF16), nb_ref[c], preferred_element_type=F32)
        mu = jnp.mean(o, axis=-1, keepdims=True)
        d = o - mu
        var = jnp.mean(d * d, axis=-1, keepdims=True)
        y = d * lax.rsqrt(var + LN_EPS) * gn_ref[...]
        gate = qkvg_ref[rows, g_cols].astype(F32)
        o_ref[rows, :] = (gate * jax.nn.sigmoid(gate) * y).astype(o_ref.dtype)
        return carry

    lax.fori_loop(0, n_chunks, out_body, 0, unroll=RET_UNROLL)


def _retention(hcat3, dec, cos_t, sin_t, gn_g):
    b, s, _ = hcat3.shape
    n_chunks = s // RET_CHUNK
    kern = functools.partial(_retention_kernel, n_chunks=n_chunks)
    return pl.pallas_call(
        kern,
        grid=(b, RET_HEADS),
        in_specs=[
            pl.BlockSpec(memory_space=pltpu.SMEM),
            pl.BlockSpec((None, s, RET_HEAD_COLS), lambda i, h: (i, 0, h)),
            pl.BlockSpec((s, RET_DK), lambda i, h: (0, 0)),
            pl.BlockSpec((s, RET_DK), lambda i, h: (0, 0)),
            pl.BlockSpec((1, RET_DV), lambda i, h: (0, h)),
        ],
        out_specs=pl.BlockSpec((None, s, RET_DV), lambda i, h: (i, 0, h)),
        out_shape=jax.ShapeDtypeStruct((b, s, RET_HEADS * RET_DV), BF16),
        scratch_shapes=[pltpu.VMEM((s, RET_DK), BF16),
                        pltpu.VMEM((n_chunks, RET_DK, RET_DV), F32),
                        pltpu.VMEM((n_chunks, RET_DK, RET_DV), F32),
                        pltpu.VMEM((n_chunks, RET_DK, RET_DV), BF16),
                        pltpu.VMEM((n_chunks, RET_DK, RET_DV), BF16)],
        compiler_params=_cparams(("parallel", "arbitrary")),
        name="retention",
    )(dec, hcat3, cos_t, sin_t, gn_g)


def _attention_kernel(x_ref, w_ref, bias_ref, o_ref,
                      xb_ref, qkv_ref, qd_ref, kd_ref, vd_ref, op_ref, lp_ref, s_ref, p_ref, m_ref, *, s):
    nt_dims = (((1,), (1,)), ((), ()))

    @pl.when(pl.program_id(1) == 0)
    def _():
        xb_ref[...] = x_ref[...].astype(BF16)

    vd_ref[:, ATT_PAIR_W:2 * ATT_PAIR_W] = jnp.ones((s, ATT_PAIR_W), BF16)

    proj_rows = 512
    for c in range(s // proj_rows):
        rows = slice(c * proj_rows, (c + 1) * proj_rows)
        acc = jnp.dot(xb_ref[rows, :], w_ref[...], preferred_element_type=F32)
        for t in range(3):
            qkv_ref[t, rows, :] = acc[:, t * ATT_PAIR_W:(t + 1) * ATT_PAIR_W]

    first_head = lax.broadcasted_iota(jnp.int32, (ATT_BQ, ATT_PAIR_W), 1) < ATT_DH

    def pattern(p, dil):
        n = s // dil
        nq = n // ATT_BQ
        kw = min(ATT_KW, n)
        n_blocks = s // ATT_BQ

        copy_rows = min(n, 512)

        def gather(r, carry):
            for c in range(n // copy_rows):
                if dil == 1:
                    src = pl.ds(c * copy_rows, copy_rows)
                else:
                    src = pl.ds(r + dil * c * copy_rows, copy_rows, stride=dil)
                row0 = pl.multiple_of(r * n + c * copy_rows, copy_rows)
                dst = pl.ds(row0, copy_rows)
                qf = qkv_ref[0, src, :] * (ATT_DH ** -0.5)
                keep = lax.broadcasted_iota(jnp.int32, qf.shape, 1) < ATT_DH
                q_first = jnp.where(keep, qf, 0.0).astype(BF16)
                q_second = jnp.where(keep, 0.0, qf).astype(BF16)
                for k in range(copy_rows // ATT_BQ):
                    blk_rows = slice(k * ATT_BQ, (k + 1) * ATT_BQ)
                    tile0 = pl.multiple_of(2 * (row0 + k * ATT_BQ), 2 * ATT_BQ)
                    qd_ref[pl.ds(tile0, ATT_BQ), :] = q_first[blk_rows]
                    qd_ref[pl.ds(tile0 + ATT_BQ, ATT_BQ), :] = q_second[blk_rows]
                kd_ref[dst, :] = qkv_ref[1, src, :].astype(BF16)
                vd_ref[dst, 0:ATT_PAIR_W] = qkv_ref[2, src, :].astype(BF16)
            return carry

        lax.fori_loop(0, dil, gather, 0)

        def place(blk):
            r = blk // nq
            i = blk - r * nq
            q0 = i * ATT_BQ
            if nq == 1:
                kstart, var = 0, 0
            else:
                kstart = jnp.clip(q0 - ATT_RADIUS, 0, n - kw)
                var = jnp.where(i == 0, 0, jnp.where(i == nq - 1, 2, 1))
            k_rows = pl.ds(pl.multiple_of(r * n + kstart, ATT_RADIUS), kw)
            if dil == 1:
                dst = pl.ds(pl.multiple_of(q0, ATT_BQ), ATT_BQ)
            else:
                dst = pl.ds(r + dil * q0, ATT_BQ, stride=dil)
            return k_rows, var, dst

        def tile_rows(t):
            return pl.ds(pl.multiple_of(t * ATT_BQ, ATT_BQ), ATT_BQ)

        def pair_rows(blk):
            return pl.ds(pl.multiple_of(blk * 2 * ATT_BQ, 2 * ATT_BQ), 2 * ATT_BQ)

        def scores(blk, carry):
            k_rows, var, _ = place(blk)
            sc = lax.dot_general(qd_ref[pair_rows(blk), :], kd_ref[k_rows, :], nt_dims,
                                 preferred_element_type=F32)
            for hh in range(2):
                s_ref[tile_rows(2 * blk + hh), 0:kw] = (sc[hh * ATT_BQ:(hh + 1) * ATT_BQ]
                                                        + bias_ref[p, var, hh, :, 0:kw])
            return carry

        lax.fori_loop(0, n_blocks, scores, 0, unroll=ATT_UNROLL)

        def numerators(t, carry):
            rows = tile_rows(t)
            m = jnp.max(s_ref[rows, 0:kw], axis=-1, keepdims=True)
            p_ref[rows, 0:kw] = jnp.exp(s_ref[rows, 0:kw] - m).astype(BF16)
            m_ref[rows, :] = jnp.broadcast_to(m, (ATT_BQ, ATT_PAIR_W))
            return carry

        lax.fori_loop(0, 2 * n_blocks, numerators, 0, unroll=ATT_UNROLL)

        def values(blk, carry):
            k_rows, _, dst = place(blk)
            pv = jnp.dot(p_ref[pair_rows(blk), 0:kw], vd_ref[k_rows, :], preferred_element_type=F32)
            l = pv[:, ATT_PAIR_W:2 * ATT_PAIR_W]
            o = pv[:, 0:ATT_PAIR_W] / l
            lse = m_ref[pair_rows(blk), :] + jnp.log(l)
            op_ref[p, dst, :] = jnp.where(first_head, o[0:ATT_BQ], o[ATT_BQ:2 * ATT_BQ])
            lp_ref[p, dst, :] = jnp.where(first_head, lse[0:ATT_BQ], lse[ATT_BQ:2 * ATT_BQ])
            return carry

        lax.fori_loop(0, n_blocks, values, 0, unroll=ATT_UNROLL)

    for p, (_, dil) in enumerate(DILATED_PATTERNS):
        pattern(p, dil)

    mix_rows = 256
    for c in range(s // mix_rows):
        rows = slice(c * mix_rows, (c + 1) * mix_rows)
        l0, l1, l2 = lp_ref[0, rows, :], lp_ref[1, rows, :], lp_ref[2, rows, :]
        m = jnp.maximum(jnp.maximum(l0, l1), l2)
        e0, e1, e2 = jnp.exp(l0 - m), jnp.exp(l1 - m), jnp.exp(l2 - m)
        mixed = (e0 * op_ref[0, rows, :] + e1 * op_ref[1, rows, :] + e2 * op_ref[2, rows, :]) / (e0 + e1 + e2)
        o_ref[rows, :] = mixed.astype(o_ref.dtype)


def _attention(x3, w_pairs, bias):
    b, s, d = x3.shape
    kern = functools.partial(_attention_kernel, s=s)
    n_pat = len(DILATED_PATTERNS)
    return pl.pallas_call(
        kern,
        grid=(b, N_PAIRS),
        in_specs=[pl.BlockSpec((None, s, d), lambda i, j: (i, 0, 0)),
                  pl.BlockSpec((d, 3 * ATT_PAIR_W), lambda i, j: (0, j)),
                  pl.BlockSpec((n_pat, N_VARIANTS, 2, ATT_BQ, ATT_KW), lambda i, j: (0, 0, j, 0, 0))],
        out_specs=pl.BlockSpec((None, s, ATT_PAIR_W), lambda i, j: (i, 0, j)),
        out_shape=jax.ShapeDtypeStruct((b, s, ATT_W), BF16),
        scratch_shapes=[pltpu.VMEM((s, d), BF16),
                        pltpu.VMEM((3, s, ATT_PAIR_W), F32),
                        pltpu.VMEM((2 * s, ATT_PAIR_W), BF16),
                        pltpu.VMEM((s, ATT_PAIR_W), BF16),
                        pltpu.VMEM((s, 2 * ATT_PAIR_W), BF16),
                        pltpu.VMEM((n_pat, s, ATT_PAIR_W), F32),
                        pltpu.VMEM((n_pat, s, ATT_PAIR_W), F32),
                        pltpu.VMEM((2 * s, ATT_KW), F32),
                        pltpu.VMEM((2 * s, ATT_KW), BF16),
                        pltpu.VMEM((2 * s, ATT_PAIR_W), F32)],
        compiler_params=_cparams(("parallel", "arbitrary")),
        name="attention",
    )(x3, w_pairs, bias)


def _t5_bucket_np(rel):
    half = NUM_BUCKETS // 2
    max_exact = half // 2
    side = np.where(rel > 0, half, 0)
    n = np.abs(rel)
    large = max_exact + (np.log(np.maximum(n, 1).astype(np.float32) / max_exact)
                         / math.log(MAX_DISTANCE / max_exact) * (half - max_exact)).astype(np.int32)
    large = np.minimum(large, half - 1)
    return side + np.where(n < max_exact, n, large)


def _attention_bias(rel_bias, s):
    n_pat = len(DILATED_PATTERNS)
    bucket = np.zeros((n_pat, N_VARIANTS, ATT_BQ, ATT_KW), np.int32)
    inside = np.zeros((n_pat, N_VARIANTS, ATT_BQ, ATT_KW), bool)
    rows = np.arange(ATT_BQ)[:, None]
    cols = np.arange(ATT_KW)[None, :]
    for p, (_, dil) in enumerate(DILATED_PATTERNS):
        n = s // dil
        kw = min(ATT_KW, n)
        offsets = (0,) if n == ATT_BQ else (0, ATT_RADIUS, ATT_BQ)
        for v, off in enumerate(offsets):
            delta = cols - off - rows
            inside[p, v] = (np.abs(delta) <= ATT_RADIUS) & (cols < kw)
            bucket[p, v] = _t5_bucket_np(np.clip(delta, -ATT_RADIUS, ATT_RADIUS) * dil)
    bucket_j = jnp.asarray(bucket)[:, :, None]
    table = rel_bias.astype(F32)
    bias = jnp.zeros((n_pat, N_VARIANTS, ATT_HEADS, ATT_BQ, ATT_KW), F32)
    for bk in range(NUM_BUCKETS):
        bias = jnp.where(bucket_j == bk, table[bk][None, None, :, None, None], bias)
    return jnp.where(jnp.asarray(inside)[:, :, None], bias, NEG_INF)


def _layer_norm(z, g, b):
    mu = jnp.mean(z, axis=-1, keepdims=True)
    d = z - mu
    var = jnp.mean(d * d, axis=-1, keepdims=True)
    return d * lax.rsqrt(var + LN_EPS) * g + b


def _split_bf16(a):
    hi = a.astype(BF16)
    lo = (a - hi.astype(F32)).astype(BF16)
    return hi, lo


def _merge_kernel(ret_ref, att_ref, br_ref, ba_ref, x_ref,
                  wpr_ref, wpa_ref, wout_ref, lng_ref, lnb_ref, wrh_ref, wrl_ref, rb_ref,
                  x1r_ref):
    y_r = jnp.dot(ret_ref[...], wpr_ref[...], preferred_element_type=F32)
    y_a = jnp.dot(att_ref[...], wpa_ref[...], preferred_element_type=F32)
    merged = (jax.nn.sigmoid(br_ref[...].astype(F32)) * y_r
              + jax.nn.sigmoid(ba_ref[...].astype(F32)) * y_a)
    mix = jnp.dot(merged.astype(BF16), wout_ref[...], preferred_element_type=F32)
    x1 = _layer_norm(DEEPNORM_ALPHA * x_ref[...] + mix, lng_ref[...], lnb_ref[...])
    x1r_ref[:, 0:D_MODEL] = x1

    xh, xl = _split_bf16(x1)
    logits = (jnp.dot(xh, wrh_ref[...], preferred_element_type=F32)
              + jnp.dot(xh, wrl_ref[...], preferred_element_type=F32)
              + jnp.dot(xl, wrh_ref[...], preferred_element_type=F32)) + rb_ref[...]
    tm = logits.shape[0]
    lane = lax.broadcasted_iota(jnp.int32, (tm, LANES), 1)
    lane_f = lane.astype(F32)
    big = float(LANES)
    gmask = lane < N_GROUPS
    gl = jnp.where(gmask, logits, -jnp.inf)
    gmax = jnp.max(gl, axis=-1, keepdims=True)
    g_idx = jnp.min(jnp.where(gl == gmax, lane_f, big), axis=-1, keepdims=True)
    g_w = 1.0 / jnp.sum(jnp.where(gmask, jnp.exp(gl - gmax), 0.0), axis=-1, keepdims=True)
    in_group = jnp.floor((lane_f - N_GROUPS) * (1.0 / EXPERTS_PER_GROUP)) == g_idx
    emask = (lane >= N_GROUPS) & (lane < N_GROUPS + N_EXPERTS) & in_group
    el = jnp.where(emask, logits, -jnp.inf)
    v1 = jnp.max(el, axis=-1, keepdims=True)
    i1 = jnp.min(jnp.where(el == v1, lane_f, big), axis=-1, keepdims=True)
    el2 = jnp.where(lane_f == i1, -jnp.inf, el)
    v2 = jnp.max(el2, axis=-1, keepdims=True)
    i2 = jnp.min(jnp.where(el2 == v2, lane_f, big), axis=-1, keepdims=True)
    t = jnp.exp(v2 - v1)
    p1 = 1.0 / (1.0 + t)
    w1 = g_w * p1
    w2 = g_w * (t * p1)
    first_lo = i1 < i2
    e_lo = jnp.where(first_lo, i1, i2) - N_GROUPS
    e_hi = jnp.where(first_lo, i2, i1) - N_GROUPS
    a_loc = e_lo - EXPERTS_PER_GROUP * g_idx
    b_loc = e_hi - EXPERTS_PER_GROUP * g_idx
    pair = a_loc * (2 * EXPERTS_PER_GROUP - 1 - a_loc) * 0.5 + (b_loc - a_loc - 1.0)
    cls = g_idx * PAIRS_PER_GROUP + pair
    route = jnp.where(lane == ROUTE_W_LO, jnp.where(first_lo, w1, w2),
                      jnp.where(lane == ROUTE_W_HI, jnp.where(first_lo, w2, w1),
                                jnp.where(lane == ROUTE_CLS, cls, 0.0)))
    x1r_ref[:, D_MODEL:X1R_COLS] = route


def _merge(ret_act, attn, hcat, x2d, wpr, wpa, wout, lng, lnb, wrh, wrl, rb):
    t = x2d.shape[0]
    tm = min(MERGE_TM, t)
    row = lambda i: (i, 0)
    const = lambda i: (0, 0)
    in_specs = [
        pl.BlockSpec((tm, D_MODEL), row),
        pl.BlockSpec((tm, ATT_W), row),
        pl.BlockSpec((tm, D_MODEL), lambda i: (i, COL_BR // D_MODEL)),
        pl.BlockSpec((tm, D_MODEL), lambda i: (i, COL_BA // D_MODEL)),
        pl.BlockSpec((tm, D_MODEL), row),
        pl.BlockSpec((D_MODEL, D_MODEL), const),
        pl.BlockSpec((ATT_W, D_MODEL), const),
        pl.BlockSpec((D_MODEL, D_MODEL), const),
        pl.BlockSpec((1, D_MODEL), const),
        pl.BlockSpec((1, D_MODEL), const),
        pl.BlockSpec((D_MODEL, LANES), const),
        pl.BlockSpec((D_MODEL, LANES), const),
        pl.BlockSpec((1, LANES), const)]
    return pl.pallas_call(
        _merge_kernel,
        grid=(t // tm,),
        in_specs=in_specs,
        out_specs=pl.BlockSpec((tm, X1R_COLS), row),
        out_shape=jax.ShapeDtypeStruct((t, X1R_COLS), F32),
        compiler_params=_cparams(("parallel",)),
        name="merge_ln1_router",
    )(ret_act, attn, hcat, hcat, x2d, wpr, wpa, wout, lng, lnb, wrh, wrl, rb)


def _row_copies_start(idx_ref, n_rows, copy_of, sem):
    def start(g, u, r):
        src, dst = copy_of(g, u, idx_ref[0, r])
        pltpu.make_async_copy(src, dst, sem).start()

    def group(g, carry):
        for u in range(SUBLANES):
            start(g, u, g * SUBLANES + u)
        return carry

    n_groups = n_rows // SUBLANES
    lax.fori_loop(0, n_groups, group, 0)

    def tail(r, carry):
        start(n_groups, r - n_groups * SUBLANES, r)
        return carry

    lax.fori_loop(n_groups * SUBLANES, n_rows, tail, 0)


def _rows_wait(buf, slot, n_rows, sem):
    m = buf.shape[1] * SUBLANES
    assert m & (m - 1) == 0
    while m >= 1:
        @pl.when((n_rows & m) != 0)
        def _(m=m):
            if m >= SUBLANES:
                view = buf.at[slot, pl.ds(0, m // SUBLANES)]
            else:
                view = buf.at[slot, 0, pl.ds(0, m)]
            pltpu.make_async_copy(view, view, sem.at[slot]).wait()
        m //= 2


def _swiglu(xb, wg_ref, wu_ref, wd_ref):
    gate = jnp.dot(xb, wg_ref[...], preferred_element_type=F32)
    up = jnp.dot(xb, wu_ref[...], preferred_element_type=F32)
    hid = (gate * jax.nn.sigmoid(gate) * up).astype(BF16)
    return jnp.dot(hid, wd_ref[...], preferred_element_type=F32)


def _moe_kernel(ea_ref, eb_ref, nv_ref, tok_ref, tok_next_ref, x_hbm,
                wga_ref, wua_ref, wda_ref, wgb_ref, wub_ref, wdb_ref, lng_ref, lnb_ref,
                out_hbm, xbuf, obuf, gsem, ssem):
    del ea_ref, eb_ref
    j = pl.program_id(0)
    nb = pl.num_programs(0)
    slot = lax.rem(j, 2)
    other = 1 - slot
    nv = nv_ref[j]

    def gather_start(idx_ref, n_rows, slot_):
        _row_copies_start(idx_ref, n_rows,
                          lambda g, u, row: (x_hbm.at[pl.ds(row, 1), :], xbuf.at[slot_, g, pl.ds(u, 1), :]),
                          gsem.at[slot_])

    @pl.when(j == 0)
    def _():
        xbuf[...] = jnp.zeros_like(xbuf)
        gather_start(tok_ref, nv, 0)

    @pl.when(j + 1 < nb)
    def _():
        gather_start(tok_next_ref, nv_ref[jnp.minimum(j + 1, nb - 1)], other)

    _rows_wait(xbuf, slot, nv, gsem)

    @pl.when(j >= 2)
    def _():
        _rows_wait(obuf, slot, nv_ref[jnp.maximum(j - 2, 0)], ssem)

    @pl.when(nv > 0)
    def _():
        xr = xbuf[slot].reshape(EXPERT_ROWS, X1R_COLS)
        x1 = xr[:, 0:D_MODEL]
        w_a = xr[:, D_MODEL + ROUTE_W_LO:D_MODEL + ROUTE_W_LO + 1]
        w_b = xr[:, D_MODEL + ROUTE_W_HI:D_MODEL + ROUTE_W_HI + 1]
        xb = x1.astype(BF16)
        ffn = w_a * _swiglu(xb, wga_ref, wua_ref, wda_ref) + w_b * _swiglu(xb, wgb_ref, wub_ref, wdb_ref)
        out = _layer_norm(DEEPNORM_ALPHA * x1 + ffn, lng_ref[...], lnb_ref[...])
        obuf[slot] = out.reshape(EXPERT_ROWS // SUBLANES, SUBLANES, D_MODEL)
        _row_copies_start(tok_ref, nv,
                          lambda g, u, row: (obuf.at[slot, g, pl.ds(u, 1), :], out_hbm.at[pl.ds(row, 1), :]),
                          ssem.at[slot])

    @pl.when(j == nb - 1)
    def _():
        @pl.when(nb >= 2)
        def _():
            _rows_wait(obuf, other, nv_ref[jnp.maximum(j - 1, 0)], ssem)
        _rows_wait(obuf, slot, nv, ssem)


def _moe(x1r, slot_tok, block_ea, block_eb, block_nv, wg, wu, wd, lng, lnb):
    t = x1r.shape[0]
    n_blocks = block_ea.shape[0]
    r = EXPERT_ROWS
    tok3 = slot_tok.reshape(n_blocks, 1, r)
    const = lambda j, ea, eb, nv: (0, 0)
    grid_spec = pltpu.PrefetchScalarGridSpec(
        num_scalar_prefetch=3,
        grid=(n_blocks,),
        in_specs=[
            pl.BlockSpec((None, 1, r), lambda j, ea, eb, nv: (j, 0, 0), memory_space=pltpu.SMEM),
            pl.BlockSpec((None, 1, r), lambda j, ea, eb, nv: (jnp.minimum(j + 1, n_blocks - 1), 0, 0),
                         memory_space=pltpu.SMEM),
            pl.BlockSpec(memory_space=pl.ANY),
            pl.BlockSpec((None, D_MODEL, D_EXPERT), lambda j, ea, eb, nv: (ea[j], 0, 0)),
            pl.BlockSpec((None, D_MODEL, D_EXPERT), lambda j, ea, eb, nv: (ea[j], 0, 0)),
            pl.BlockSpec((None, D_EXPERT, D_MODEL), lambda j, ea, eb, nv: (ea[j], 0, 0)),
            pl.BlockSpec((None, D_MODEL, D_EXPERT), lambda j, ea, eb, nv: (eb[j], 0, 0)),
            pl.BlockSpec((None, D_MODEL, D_EXPERT), lambda j, ea, eb, nv: (eb[j], 0, 0)),
            pl.BlockSpec((None, D_EXPERT, D_MODEL), lambda j, ea, eb, nv: (eb[j], 0, 0)),
            pl.BlockSpec((1, D_MODEL), const),
            pl.BlockSpec((1, D_MODEL), const),
        ],
        out_specs=pl.BlockSpec(memory_space=pl.ANY),
        scratch_shapes=[pltpu.VMEM((2, r // SUBLANES, SUBLANES, X1R_COLS), F32),
                        pltpu.VMEM((2, r // SUBLANES, SUBLANES, D_MODEL), F32),
                        pltpu.SemaphoreType.DMA((2,)), pltpu.SemaphoreType.DMA((2,))],
    )
    return pl.pallas_call(
        _moe_kernel,
        grid_spec=grid_spec,
        out_shape=jax.ShapeDtypeStruct((t, D_MODEL), F32),
        compiler_params=_cparams(("arbitrary",)),
        name="experts_combine_ln2",
    )(block_ea, block_eb, block_nv, tok3, tok3, x1r, wg, wu, wd, wg, wu, wd, lng, lnb)


def _pair_tables():
    ea, eb = [], []
    for g in range(N_GROUPS):
        for a in range(EXPERTS_PER_GROUP):
            for b in range(a + 1, EXPERTS_PER_GROUP):
                ea.append(g * EXPERTS_PER_GROUP + a)
                eb.append(g * EXPERTS_PER_GROUP + b)
    return np.asarray(ea, np.int32), np.asarray(eb, np.int32)


def _routing_plan(cls, n_tokens):
    r = EXPERT_ROWS
    assert n_tokens & (n_tokens - 1) == 0
    classes = jnp.arange(N_CLASSES, dtype=jnp.int32)
    counts = jnp.sum((cls[:, None] == classes[None, :]).astype(jnp.int32), axis=0)
    keys = jnp.sort(cls * n_tokens + jnp.arange(n_tokens, dtype=jnp.int32))
    order = jnp.concatenate([keys & (n_tokens - 1), jnp.zeros((r,), jnp.int32)])
    cstart = jnp.cumsum(counts) - counts
    padded = (counts + r - 1) // r * r
    ends = jnp.cumsum(padded)
    pstart = ends - padded
    n_blocks = n_tokens // r + N_CLASSES
    block_start = jnp.arange(n_blocks, dtype=jnp.int32) * r
    block_cls = jnp.minimum(jnp.sum(ends[None, :] <= block_start[:, None], axis=1), N_CLASSES - 1)
    onehot_b = (block_cls[:, None] == classes[None, :]).astype(jnp.int32)
    ea_tab, eb_tab = _pair_tables()
    block_ea = jnp.sum(onehot_b * jnp.asarray(ea_tab)[None, :], axis=1).astype(jnp.int32)
    block_eb = jnp.sum(onehot_b * jnp.asarray(eb_tab)[None, :], axis=1).astype(jnp.int32)
    used = jnp.sum(onehot_b * (pstart + counts)[None, :], axis=1)
    block_nv = jnp.clip(used - block_start, 0, r).astype(jnp.int32)
    src = jnp.clip(jnp.sum(onehot_b * (cstart - pstart)[None, :], axis=1) + block_start, 0, n_tokens)
    slot_tok = jax.vmap(lambda o: lax.dynamic_slice(order, (o,), (r,)))(src)
    return slot_tok, block_ea, block_eb, block_nv


def _rotary_tables(s):
    half = RET_DK // 2
    inv = ROPE_BASE ** (-jnp.arange(half, dtype=F32) / half)
    ang = jnp.arange(s, dtype=F32)[:, None] * inv[None, :]
    cos, sin = jnp.cos(ang), jnp.sin(ang)
    return jnp.concatenate([cos, cos], axis=1), jnp.concatenate([-sin, sin], axis=1)


def _cols(w, off, width):
    return lax.slice_in_dim(w, off, off + width, axis=1)


def _layer(x, w_in, dec_f, dec_b, gn_g, w_proj_ret, rel_bias, w_proj_attn, w_out, ln1_g, ln1_b,
           rw_g, rb_g, rw_e, rb_e, w_gate, w_up, w_down, ln2_g, ln2_b):
    b, s, d = x.shape
    t = b * s
    x2d = x.reshape(t, d)
    for window, dil in DILATED_PATTERNS:
        assert window // (2 * dil) == ATT_RADIUS and (s // dil) % ATT_BQ == 0

    head_cols = [_cols(w_in, off + h * width, width) for h in range(RET_HEADS)
                 for off, width in ((REF_QR, RET_DK), (REF_KR, RET_DK), (REF_VR, RET_DV), (REF_GR, RET_DV))]
    w_main = jnp.concatenate(head_cols + [_cols(w_in, REF_BR, D_MODEL), _cols(w_in, REF_BA, D_MODEL)],
                             axis=1).astype(BF16)
    hcat = _proj_in(x2d, w_main)
    hcat3 = hcat.reshape(b, s, MAIN_COLS)

    dec = jnp.stack([jax.nn.log_sigmoid(dec_f.astype(F32)), jax.nn.log_sigmoid(dec_b.astype(F32))], axis=0)
    cos_t, sin_t = _rotary_tables(s)
    ret_act = _retention(hcat3, dec, cos_t, sin_t, gn_g.reshape(1, -1).astype(F32)).reshape(t, -1)

    w_pairs = jnp.concatenate(
        [_cols(w_in, off + j * ATT_PAIR_W, ATT_PAIR_W) for j in range(N_PAIRS) for off in (REF_QA, REF_KA, REF_VA)],
        axis=1).astype(BF16)
    attn = _attention(x, w_pairs, _attention_bias(rel_bias, s)).reshape(t, ATT_W)

    wr = jnp.zeros((d, LANES), F32).at[:, :N_GROUPS].set(rw_g.astype(F32))
    wr = wr.at[:, N_GROUPS:N_GROUPS + N_EXPERTS].set(rw_e.astype(F32))
    rb = jnp.zeros((1, LANES), F32).at[0, :N_GROUPS].set(rb_g.astype(F32))
    rb = rb.at[0, N_GROUPS:N_GROUPS + N_EXPERTS].set(rb_e.astype(F32))
    wrh = wr.astype(BF16)
    wrl = (wr - wrh.astype(F32)).astype(BF16)
    x1r = _merge(ret_act, attn, hcat, x2d,
                 w_proj_ret.astype(BF16), w_proj_attn.astype(BF16), w_out.astype(BF16),
                 ln1_g.reshape(1, -1).astype(F32), ln1_b.reshape(1, -1).astype(F32), wrh, wrl, rb)

    cls = x1r[:, D_MODEL + ROUTE_CLS].astype(jnp.int32)
    slot_tok, block_ea, block_eb, block_nv = _routing_plan(cls, t)
    out = _moe(x1r, slot_tok, block_ea, block_eb, block_nv,
               w_gate.astype(BF16), w_up.astype(BF16), w_down.astype(BF16),
               ln2_g.reshape(1, -1).astype(F32), ln2_b.reshape(1, -1).astype(F32))
    return out.reshape(b, s, d)


def kernel(x, w_in, ret_decay_fwd, ret_decay_bwd, ret_gn_g, w_proj_ret, rel_bias, w_proj_attn, w_out,
           ln1_g, ln1_b, router_w_group, router_b_group, router_w_expert, router_b_expert,
           w_gate, w_up, w_down, ln2_g, ln2_b):
    for l in range(DEPTH):
        x = _layer(x, w_in[l], ret_decay_fwd[l], ret_decay_bwd[l], ret_gn_g[l], w_proj_ret[l], rel_bias,
                   w_proj_attn[l], w_out[l], ln1_g[l], ln1_b[l], router_w_group[l], router_b_group[l],
                   router_w_expert[l], router_b_expert[l], w_gate[l], w_up[l], w_down[l],
                   ln2_g[l], ln2_b[l])
    return x
```

```python
import functools
import math

import numpy as np
import jax
import jax.numpy as jnp
from jax import lax
from jax.experimental import pallas as pl
from jax.experimental.pallas import tpu as pltpu

F32 = jnp.float32
BF16 = jnp.bfloat16

D_MODEL = 1024
RET_HEADS = 4
RET_DK = 128
RET_DV = 256
RET_CHUNK = 128
ROPE_BASE = 10000.0
ATT_HEADS = 8
ATT_DH = 64
ATT_W = ATT_HEADS * ATT_DH
DILATED_PATTERNS = ((128, 1), (512, 4), (2048, 16))
NUM_BUCKETS = 32
MAX_DISTANCE = 1024
N_GROUPS = 4
EXPERTS_PER_GROUP = 4
N_EXPERTS = N_GROUPS * EXPERTS_PER_GROUP
TOP_K = 2
D_EXPERT = 512
DEPTH = 1
DEEPNORM_ALPHA = (2.0 * DEPTH) ** 0.25
LN_EPS = 1e-5
NEG_INF = -1e30

REF_QR, REF_KR, REF_VR, REF_GR, REF_QA, REF_KA, REF_VA, REF_BR, REF_BA = (
    0, 512, 1024, 2048, 3072, 3584, 4096, 4608, 5632)
RET_HEAD_COLS = 2 * RET_DK + 2 * RET_DV
COL_BR = RET_HEADS * RET_HEAD_COLS
COL_BA = COL_BR + D_MODEL
MAIN_COLS = COL_BA + D_MODEL

V7X_VMEM_LIMIT_BYTES = 56 * 1024 * 1024
LANES = 128
SUBLANES = 8

RET_UNROLL = 8
PROJ_TM = 1024
PROJ_TN = 1280
ATT_BQ = 128
ATT_RADIUS = 64
ATT_KW = 2 * ATT_BQ
ATT_PAIR_W = 2 * ATT_DH
N_PAIRS = ATT_HEADS // 2
N_VARIANTS = 3
ATT_UNROLL = 16
MERGE_TM = 512
EXPERT_ROWS = 256
PAIRS_PER_GROUP = EXPERTS_PER_GROUP * (EXPERTS_PER_GROUP - 1) // 2
N_CLASSES = N_GROUPS * PAIRS_PER_GROUP
X1R_COLS = D_MODEL + LANES
ROUTE_W_LO, ROUTE_W_HI, ROUTE_CLS = 0, 1, 2


def _cparams(sem):
    return pltpu.CompilerParams(dimension_semantics=sem, vmem_limit_bytes=V7X_VMEM_LIMIT_BYTES)


def _proj_in_kernel(x_ref, w_ref, o_ref, xb_ref):
    @pl.when(pl.program_id(1) == 0)
    def _():
        xb_ref[...] = x_ref[...].astype(BF16)

    o_ref[...] = jnp.dot(xb_ref[...], w_ref[...], preferred_element_type=F32).astype(o_ref.dtype)


def _proj_in(x2d, w_bf16):
    t = x2d.shape[0]
    n_cols = w_bf16.shape[1]
    tm = min(PROJ_TM, t)
    return pl.pallas_call(
        _proj_in_kernel,
        grid=(t // tm, n_cols // PROJ_TN),
        in_specs=[pl.BlockSpec((tm, D_MODEL), lambda i, j: (i, 0)),
                  pl.BlockSpec((D_MODEL, PROJ_TN), lambda i, j: (0, j))],
        out_specs=pl.BlockSpec((tm, PROJ_TN), lambda i, j: (i, j)),
        out_shape=jax.ShapeDtypeStruct((t, n_cols), BF16),
        scratch_shapes=[pltpu.VMEM((tm, D_MODEL), BF16)],
        compiler_params=_cparams(("parallel", "arbitrary")),
        name="proj_in",
    )(x2d, w_bf16)


def _retention_kernel(dec_ref, qkvg_ref, cos_ref, sin_ref, gn_ref, o_ref,
                      kr_ref, kvf_ref, kvb_ref, pf_ref, nb_ref, *, n_chunks):
    c_len = RET_CHUNK
    q_cols = slice(0, RET_DK)
    k_cols = slice(RET_DK, 2 * RET_DK)
    v_cols = slice(2 * RET_DK, 2 * RET_DK + RET_DV)
    g_cols = slice(2 * RET_DK + RET_DV, RET_HEAD_COLS)
    h = pl.program_id(1)
    lgf = dec_ref[0, h]
    lgb = dec_ref[1, h]
    row = lax.broadcasted_iota(jnp.int32, (c_len, c_len), 0).astype(F32)
    col = lax.broadcasted_iota(jnp.int32, (c_len, c_len), 1).astype(F32)
    diff = row - col
    dmat = jnp.where(diff > 0, jnp.exp(lgf * jnp.maximum(diff, 0.0)),
                     jnp.where(diff < 0, jnp.exp(lgb * jnp.maximum(-diff, 0.0)), 2.0))
    zeta_f = jnp.exp(lgf * (c_len - 1.0 - row))
    zeta_b = jnp.exp(lgb * row)
    xi_f = jnp.exp(lgf * (row + 1.0))
    xi_b = jnp.exp(lgb * (c_len - row))
    zero_row = jnp.zeros((1, RET_DV), F32)
    decay_f = jnp.exp(zero_row + lgf * c_len)
    decay_b = jnp.exp(zero_row + lgb * c_len)
    k_scale = RET_DK ** -0.5
    tn_dims = (((0,), (0,)), ((), ()))
    nt_dims = (((1,), (1,)), ((), ()))

    def rows_of(c):
        return pl.ds(pl.multiple_of(c * c_len, c_len), c_len)

    def rotary(t, rows):
        return t * cos_ref[rows, :] + pltpu.roll(t, RET_DK // 2, 1) * sin_ref[rows, :]

    def kv_body(c, carry):
        rows = rows_of(c)
        kr = rotary(qkvg_ref[rows, k_cols].astype(F32), rows) * k_scale
        kr_ref[rows, :] = kr.astype(BF16)
        vb = qkvg_ref[rows, v_cols]
        kvf_ref[c] = lax.dot_general((kr * zeta_f).astype(BF16), vb, tn_dims, preferred_element_type=F32)
        kvb_ref[c] = lax.dot_general((kr * zeta_b).astype(BF16), vb, tn_dims, preferred_element_type=F32)
        return carry

    lax.fori_loop(0, n_chunks, kv_body, 0, unroll=RET_UNROLL)

    def scan_f(c, st):
        pf_ref[c] = st.astype(BF16)
        return st * decay_f + kvf_ref[c]

    lax.fori_loop(0, n_chunks, scan_f, jnp.zeros((RET_DK, RET_DV), F32))

    def scan_b(i, st):
        c = n_chunks - 1 - i
        nb_ref[c] = st.astype(BF16)
        return st * decay_b + kvb_ref[c]

    lax.fori_loop(0, n_chunks, scan_b, jnp.zeros((RET_DK, RET_DV), F32))

    def out_body(c, carry):
        rows = rows_of(c)
        qr = rotary(qkvg_ref[rows, q_cols].astype(F32), rows)
        vb = qkvg_ref[rows, v_cols]
        scores = lax.dot_general(qr.astype(BF16), kr_ref[rows, :], nt_dims, preferred_element_type=F32)
        o = jnp.dot((scores * dmat).astype(BF16), vb, preferred_element_type=F32)
        o += jnp.dot((qr * xi_f).astype(BF16), pf_ref[c], preferred_element_type=F32)
        o += jnp.dot((qr * xi_b).astype(BF16), nb_ref[c], preferred_element_type=F32)
        mu = jnp.mean(o, axis=-1, keepdims=True)
        d = o - mu
        var = jnp.mean(d * d, axis=-1, keepdims=True)
        y = d * lax.rsqrt(var + LN_EPS) * gn_ref[...]
        gate = qkvg_ref[rows, g_cols].astype(F32)
        o_ref[rows, :] = (gate * jax.nn.sigmoid(gate) * y).astype(o_ref.dtype)
        return carry

    lax.fori_loop(0, n_chunks, out_body, 0, unroll=RET_UNROLL)


def _retention(hcat3, dec, cos_t, sin_t, gn_g):
    b, s, _ = hcat3.shape
    n_chunks = s // RET_CHUNK
    kern = functools.partial(_retention_kernel, n_chunks=n_chunks)
    return pl.pallas_call(
        kern,
        grid=(b, RET_HEADS),
        in_specs=[
            pl.BlockSpec(memory_space=pltpu.SMEM),
            pl.BlockSpec((None, s, RET_HEAD_COLS), lambda i, h: (i, 0, h)),
            pl.BlockSpec((s, RET_DK), lambda i, h: (0, 0)),
            pl.BlockSpec((s, RET_DK), lambda i, h: (0, 0)),
            pl.BlockSpec((1, RET_DV), lambda i, h: (0, h)),
        ],
        out_specs=pl.BlockSpec((None, s, RET_DV), lambda i, h: (i, 0, h)),
        out_shape=jax.ShapeDtypeStruct((b, s, RET_HEADS * RET_DV), BF16),
        scratch_shapes=[pltpu.VMEM((s, RET_DK), BF16),
                        pltpu.VMEM((n_chunks, RET_DK, RET_DV), F32),
                        pltpu.VMEM((n_chunks, RET_DK, RET_DV), F32),
                        pltpu.VMEM((n_chunks, RET_DK, RET_DV), BF16),
                        pltpu.VMEM((n_chunks, RET_DK, RET_DV), BF16)],
        compiler_params=_cparams(("parallel", "arbitrary")),
        name="retention",
    )(dec, hcat3, cos_t, sin_t, gn_g)


def _attention_kernel(x_ref, w_ref, bias_ref, o_ref,
                      xb_ref, qkv_ref, qd_ref, kd_ref, vd_ref, op_ref, lp_ref, s_ref, p_ref, m_ref, *, s):
    nt_dims = (((1,), (1,)), ((), ()))

    @pl.when(pl.program_id(1) == 0)
    def _():
        xb_ref[...] = x_ref[...].astype(BF16)

    vd_ref[:, ATT_PAIR_W:2 * ATT_PAIR_W] = jnp.ones((s, ATT_PAIR_W), BF16)

    proj_rows = 512
    for c in range(s // proj_rows):
        rows = slice(c * proj_rows, (c + 1) * proj_rows)
        acc = jnp.dot(xb_ref[rows, :], w_ref[...], preferred_element_type=F32)
        for t in range(3):
            qkv_ref[t, rows, :] = acc[:, t * ATT_PAIR_W:(t + 1) * ATT_PAIR_W]

    first_head = lax.broadcasted_iota(jnp.int32, (ATT_BQ, ATT_PAIR_W), 1) < ATT_DH

    def pattern(p, dil):
        n = s // dil
        nq = n // ATT_BQ
        kw = min(ATT_KW, n)
        n_blocks = s // ATT_BQ

        copy_rows = min(n, 512)

        def gather(r, carry):
            for c in range(n // copy_rows):
                if dil == 1:
                    src = pl.ds(c * copy_rows, copy_rows)
                else:
                    src = pl.ds(r + dil * c * copy_rows, copy_rows, stride=dil)
                row0 = pl.multiple_of(r * n + c * copy_rows, copy_rows)
                dst = pl.ds(row0, copy_rows)
                qf = qkv_ref[0, src, :] * (ATT_DH ** -0.5)
                keep = lax.broadcasted_iota(jnp.int32, qf.shape, 1) < ATT_DH
                q_first = jnp.where(keep, qf, 0.0).astype(BF16)
                q_second = jnp.where(keep, 0.0, qf).astype(BF16)
                for k in range(copy_rows // ATT_BQ):
                    blk_rows = slice(k * ATT_BQ, (k + 1) * ATT_BQ)
                    tile0 = pl.multiple_of(2 * (row0 + k * ATT_BQ), 2 * ATT_BQ)
                    qd_ref[pl.ds(tile0, ATT_BQ), :] = q_first[blk_rows]
                    qd_ref[pl.ds(tile0 + ATT_BQ, ATT_BQ), :] = q_second[blk_rows]
                kd_ref[dst, :] = qkv_ref[1, src, :].astype(BF16)
                vd_ref[dst, 0:ATT_PAIR_W] = qkv_ref[2, src, :].astype(BF16)
            return carry

        lax.fori_loop(0, dil, gather, 0)

        def place(blk):
            r = blk // nq
            i = blk - r * nq
            q0 = i * ATT_BQ
            if nq == 1:
                kstart, var = 0, 0
            else:
                kstart = jnp.clip(q0 - ATT_RADIUS, 0, n - kw)
                var = jnp.where(i == 0, 0, jnp.where(i == nq - 1, 2, 1))
            k_rows = pl.ds(pl.multiple_of(r * n + kstart, ATT_RADIUS), kw)
            if dil == 1:
                dst = pl.ds(pl.multiple_of(q0, ATT_BQ), ATT_BQ)
            else:
                dst = pl.ds(r + dil * q0, ATT_BQ, stride=dil)
            return k_rows, var, dst

        def tile_rows(t):
            return pl.ds(pl.multiple_of(t * ATT_BQ, ATT_BQ), ATT_BQ)

        def pair_rows(blk):
            return pl.ds(pl.multiple_of(blk * 2 * ATT_BQ, 2 * ATT_BQ), 2 * ATT_BQ)

        def scores(blk, carry):
            k_rows, var, _ = place(blk)
            sc = lax.dot_general(qd_ref[pair_rows(blk), :], kd_ref[k_rows, :], nt_dims,
                                 preferred_element_type=F32)
            for hh in range(2):
                s_ref[tile_rows(2 * blk + hh), 0:kw] = (sc[hh * ATT_BQ:(hh + 1) * ATT_BQ]
                                                        + bias_ref[p, var, hh, :, 0:kw])
            return carry

        lax.fori_loop(0, n_blocks, scores, 0, unroll=ATT_UNROLL)

        def numerators(t, carry):
            rows = tile_rows(t)
            m = jnp.max(s_ref[rows, 0:kw], axis=-1, keepdims=True)
            p_ref[rows, 0:kw] = jnp.exp(s_ref[rows, 0:kw] - m).astype(BF16)
            m_ref[rows, :] = jnp.broadcast_to(m, (ATT_BQ, ATT_PAIR_W))
            return carry

        lax.fori_loop(0, 2 * n_blocks, numerators, 0, unroll=ATT_UNROLL)

        def values(blk, carry):
            k_rows, _, dst = place(blk)
            pv = jnp.dot(p_ref[pair_rows(blk), 0:kw], vd_ref[k_rows, :], preferred_element_type=F32)
            l = pv[:, ATT_PAIR_W:2 * ATT_PAIR_W]
            o = pv[:, 0:ATT_PAIR_W] / l
            lse = m_ref[pair_rows(blk), :] + jnp.log(l)
            op_ref[p, dst, :] = jnp.where(first_head, o[0:ATT_BQ], o[ATT_BQ:2 * ATT_BQ])
            lp_ref[p, dst, :] = jnp.where(first_head, lse[0:ATT_BQ], lse[ATT_BQ:2 * ATT_BQ])
            return carry

        lax.fori_loop(0, n_blocks, values, 0, unroll=ATT_UNROLL)

    for p, (_, dil) in enumerate(DILATED_PATTERNS):
        pattern(p, dil)

    mix_rows = 256
    for c in range(s // mix_rows):
        rows = slice(c * mix_rows, (c + 1) * mix_rows)
        l0, l1, l2 = lp_ref[0, rows, :], lp_ref[1, rows, :], lp_ref[2, rows, :]
        m = jnp.maximum(jnp.maximum(l0, l1), l2)
        e0, e1, e2 = jnp.exp(l0 - m), jnp.exp(l1 - m), jnp.exp(l2 - m)
        mixed = (e0 * op_ref[0, rows, :] + e1 * op_ref[1, rows, :] + e2 * op_ref[2, rows, :]) / (e0 + e1 + e2)
        o_ref[rows, :] = mixed.astype(o_ref.dtype)


def _attention(x3, w_pairs, bias):
    b, s, d = x3.shape
    kern = functools.partial(_attention_kernel, s=s)
    n_pat = len(DILATED_PATTERNS)
    return pl.pallas_call(
        kern,
        grid=(b, N_PAIRS),
        in_specs=[pl.BlockSpec((None, s, d), lambda i, j: (i, 0, 0)),
                  pl.BlockSpec((d, 3 * ATT_PAIR_W), lambda i, j: (0, j)),
                  pl.BlockSpec((n_pat, N_VARIANTS, 2, ATT_BQ, ATT_KW), lambda i, j: (0, 0, j, 0, 0))],
        out_specs=pl.BlockSpec((None, s, ATT_PAIR_W), lambda i, j: (i, 0, j)),
        out_shape=jax.ShapeDtypeStruct((b, s, ATT_W), BF16),
        scratch_shapes=[pltpu.VMEM((s, d), BF16),
                        pltpu.VMEM((3, s, ATT_PAIR_W), F32),
                        pltpu.VMEM((2 * s, ATT_PAIR_W), BF16),
                        pltpu.VMEM((s, ATT_PAIR_W), BF16),
                        pltpu.VMEM((s, 2 * ATT_PAIR_W), BF16),
                        pltpu.VMEM((n_pat, s, ATT_PAIR_W), F32),
                        pltpu.VMEM((n_pat, s, ATT_PAIR_W), F32),
                        pltpu.VMEM((2 * s, ATT_KW), F32),
                        pltpu.VMEM((2 * s, ATT_KW), BF16),
                        pltpu.VMEM((2 * s, ATT_PAIR_W), F32)],
        compiler_params=_cparams(("parallel", "arbitrary")),
        name="attention",
    )(x3, w_pairs, bias)


def _t5_bucket_np(rel):
    half = NUM_BUCKETS // 2
    max_exact = half // 2
    side = np.where(rel > 0, half, 0)
    n = np.abs(rel)
    large = max_exact + (np.log(np.maximum(n, 1).astype(np.float32) / max_exact)
                         / math.log(MAX_DISTANCE / max_exact) * (half - max_exact)).astype(np.int32)
    large = np.minimum(large, half - 1)
    return side + np.where(n < max_exact, n, large)


def _attention_bias(rel_bias, s):
    n_pat = len(DILATED_PATTERNS)
    bucket = np.zeros((n_pat, N_VARIANTS, ATT_BQ, ATT_KW), np.int32)
    inside = np.zeros((n_pat, N_VARIANTS, ATT_BQ, ATT_KW), bool)
    rows = np.arange(ATT_BQ)[:, None]
    cols = np.arange(ATT_KW)[None, :]
    for p, (_, dil) in enumerate(DILATED_PATTERNS):
        n = s // dil
        kw = min(ATT_KW, n)
        offsets = (0,) if n == ATT_BQ else (0, ATT_RADIUS, ATT_BQ)
        for v, off in enumerate(offsets):
            delta = cols - off - rows
            inside[p, v] = (np.abs(delta) <= ATT_RADIUS) & (cols < kw)
            bucket[p, v] = _t5_bucket_np(np.clip(delta, -ATT_RADIUS, ATT_RADIUS) * dil)
    bucket_j = jnp.asarray(bucket)[:, :, None]
    table = rel_bias.astype(F32)
    bias = jnp.zeros((n_pat, N_VARIANTS, ATT_HEADS, ATT_BQ, ATT_KW), F32)
    for bk in range(NUM_BUCKETS):
        bias = jnp.where(bucket_j == bk, table[bk][None, None, :, None, None], bias)
    return jnp.where(jnp.asarray(inside)[:, :, None], bias, NEG_INF)


def _layer_norm(z, g, b):
    mu = jnp.mean(z, axis=-1, keepdims=True)
    d = z - mu
    var = jnp.mean(d * d, axis=-1, keepdims=True)
    return d * lax.rsqrt(var + LN_EPS) * g + b


def _split_bf16(a):
    hi = a.astype(BF16)
    lo = (a - hi.astype(F32)).astype(BF16)
    return hi, lo


def _merge_kernel(ret_ref, att_ref, br_ref, ba_ref, x_ref,
                  wpr_ref, wpa_ref, wout_ref, lng_ref, lnb_ref, wrh_ref, wrl_ref, rb_ref,
                  x1r_ref):
    y_r = jnp.dot(ret_ref[...], wpr_ref[...], preferred_element_type=F32)
    y_a = jnp.dot(att_ref[...], wpa_ref[...], preferred_element_type=F32)
    merged = (jax.nn.sigmoid(br_ref[...].astype(F32)) * y_r
              + jax.nn.sigmoid(ba_ref[...].astype(F32)) * y_a)
    mix = jnp.dot(merged.astype(BF16), wout_ref[...], preferred_element_type=F32)
    x1 = _layer_norm(DEEPNORM_ALPHA * x_ref[...] + mix, lng_ref[...], lnb_ref[...])
    x1r_ref[:, 0:D_MODEL] = x1

    xh, xl = _split_bf16(x1)
    logits = (jnp.dot(xh, wrh_ref[...], preferred_element_type=F32)
              + jnp.dot(xh, wrl_ref[...], preferred_element_type=F32)
              + jnp.dot(xl, wrh_ref[...], preferred_element_type=F32)) + rb_ref[...]
    tm = logits.shape[0]
    lane = lax.broadcasted_iota(jnp.int32, (tm, LANES), 1)
    lane_f = lane.astype(F32)
    big = float(LANES)
    gmask = lane < N_GROUPS
    gl = jnp.where(gmask, logits, -jnp.inf)
    gmax = jnp.max(gl, axis=-1, keepdims=True)
    g_idx = jnp.min(jnp.where(gl == gmax, lane_f, big), axis=-1, keepdims=True)
    g_w = 1.0 / jnp.sum(jnp.where(gmask, jnp.exp(gl - gmax), 0.0), axis=-1, keepdims=True)
    in_group = jnp.floor((lane_f - N_GROUPS) * (1.0 / EXPERTS_PER_GROUP)) == g_idx
    emask = (lane >= N_GROUPS) & (lane < N_GROUPS + N_EXPERTS) & in_group
    el = jnp.where(emask, logits, -jnp.inf)
    v1 = jnp.max(el, axis=-1, keepdims=True)
    i1 = jnp.min(jnp.where(el == v1, lane_f, big), axis=-1, keepdims=True)
    el2 = jnp.where(lane_f == i1, -jnp.inf, el)
    v2 = jnp.max(el2, axis=-1, keepdims=True)
    i2 = jnp.min(jnp.where(el2 == v2, lane_f, big), axis=-1, keepdims=True)
    t = jnp.exp(v2 - v1)
    p1 = 1.0 / (1.0 + t)
    w1 = g_w * p1
    w2 = g_w * (t * p1)
    first_lo = i1 < i2
    e_lo = jnp.where(first_lo, i1, i2) - N_GROUPS
    e_hi = jnp.where(first_lo, i2, i1) - N_GROUPS
    a_loc = e_lo - EXPERTS_PER_GROUP * g_idx
    b_loc = e_hi - EXPERTS_PER_GROUP * g_idx
    pair = a_loc * (2 * EXPERTS_PER_GROUP - 1 - a_loc) * 0.5 + (b_loc - a_loc - 1.0)
    cls = g_idx * PAIRS_PER_GROUP + pair
    route = jnp.where(lane == ROUTE_W_LO, jnp.where(first_lo, w1, w2),
                      jnp.where(lane == ROUTE_W_HI, jnp.where(first_lo, w2, w1),
                                jnp.where(lane == ROUTE_CLS, cls, 0.0)))
    x1r_ref[:, D_MODEL:X1R_COLS] = route


def _merge(ret_act, attn, hcat, x2d, wpr, wpa, wout, lng, lnb, wrh, wrl, rb):
    t = x2d.shape[0]
    tm = min(MERGE_TM, t)
    row = lambda i: (i, 0)
    const = lambda i: (0, 0)
    in_specs = [
        pl.BlockSpec((tm, D_MODEL), row),
        pl.BlockSpec((tm, ATT_W), row),
        pl.BlockSpec((tm, D_MODEL), lambda i: (i, COL_BR // D_MODEL)),
        pl.BlockSpec((tm, D_MODEL), lambda i: (i, COL_BA // D_MODEL)),
        pl.BlockSpec((tm, D_MODEL), row),
        pl.BlockSpec((D_MODEL, D_MODEL), const),
        pl.BlockSpec((ATT_W, D_MODEL), const),
        pl.BlockSpec((D_MODEL, D_MODEL), const),
        pl.BlockSpec((1, D_MODEL), const),
        pl.BlockSpec((1, D_MODEL), const),
        pl.BlockSpec((D_MODEL, LANES), const),
        pl.BlockSpec((D_MODEL, LANES), const),
        pl.BlockSpec((1, LANES), const)]
    return pl.pallas_call(
        _merge_kernel,
        grid=(t // tm,),
        in_specs=in_specs,
        out_specs=pl.BlockSpec((tm, X1R_COLS), row),
        out_shape=jax.ShapeDtypeStruct((t, X1R_COLS), F32),
        compiler_params=_cparams(("parallel",)),
        name="merge_ln1_router",
    )(ret_act, attn, hcat, hcat, x2d, wpr, wpa, wout, lng, lnb, wrh, wrl, rb)


def _row_copies_start(idx_ref, n_rows, copy_of, sem):
    def start(g, u, r):
        src, dst = copy_of(g, u, idx_ref[0, r])
        pltpu.make_async_copy(src, dst, sem).start()

    def group(g, carry):
        for u in range(SUBLANES):
            start(g, u, g * SUBLANES + u)
        return carry

    n_groups = n_rows // SUBLANES
    lax.fori_loop(0, n_groups, group, 0)

    def tail(r, carry):
        start(n_groups, r - n_groups * SUBLANES, r)
        return carry

    lax.fori_loop(n_groups * SUBLANES, n_rows, tail, 0)


def _rows_wait(buf, slot, n_rows, sem):
    m = buf.shape[1] * SUBLANES
    assert m & (m - 1) == 0
    while m >= 1:
        @pl.when((n_rows & m) != 0)
        def _(m=m):
            if m >= SUBLANES:
                view = buf.at[slot, pl.ds(0, m // SUBLANES)]
            else:
                view = buf.at[slot, 0, pl.ds(0, m)]
            pltpu.make_async_copy(view, view, sem.at[slot]).wait()
        m //= 2


def _swiglu(xb, wg_ref, wu_ref, wd_ref):
    gate = jnp.dot(xb, wg_ref[...], preferred_element_type=F32)
    up = jnp.dot(xb, wu_ref[...], preferred_element_type=F32)
    hid = (gate * jax.nn.sigmoid(gate) * up).astype(BF16)
    return jnp.dot(hid, wd_ref[...], preferred_element_type=F32)


def _moe_kernel(ea_ref, eb_ref, nv_ref, tok_ref, tok_next_ref, x_hbm,
                wga_ref, wua_ref, wda_ref, wgb_ref, wub_ref, wdb_ref, lng_ref, lnb_ref,
                out_hbm, xbuf, obuf, gsem, ssem):
    del ea_ref, eb_ref
    j = pl.program_id(0)
    nb = pl.num_programs(0)
    slot = lax.rem(j, 2)
    other = 1 - slot
    nv = nv_ref[j]

    def gather_start(idx_ref, n_rows, slot_):
        _row_copies_start(idx_ref, n_rows,
                          lambda g, u, row: (x_hbm.at[pl.ds(row, 1), :], xbuf.at[slot_, g, pl.ds(u, 1), :]),
                          gsem.at[slot_])

    @pl.when(j == 0)
    def _():
        xbuf[...] = jnp.zeros_like(xbuf)
        gather_start(tok_ref, nv, 0)

    @pl.when(j + 1 < nb)
    def _():
        gather_start(tok_next_ref, nv_ref[jnp.minimum(j + 1, nb - 1)], other)

    _rows_wait(xbuf, slot, nv, gsem)

    @pl.when(j >= 2)
    def _():
        _rows_wait(obuf, slot, nv_ref[jnp.maximum(j - 2, 0)], ssem)

    @pl.when(nv > 0)
    def _():
        xr = xbuf[slot].reshape(EXPERT_ROWS, X1R_COLS)
        x1 = xr[:, 0:D_MODEL]
        w_a = xr[:, D_MODEL + ROUTE_W_LO:D_MODEL + ROUTE_W_LO + 1]
        w_b = xr[:, D_MODEL + ROUTE_W_HI:D_MODEL + ROUTE_W_HI + 1]
        xb = x1.astype(BF16)
        ffn = w_a * _swiglu(xb, wga_ref, wua_ref, wda_ref) + w_b * _swiglu(xb, wgb_ref, wub_ref, wdb_ref)
        out = _layer_norm(DEEPNORM_ALPHA * x1 + ffn, lng_ref[...], lnb_ref[...])
        obuf[slot] = out.reshape(EXPERT_ROWS // SUBLANES, SUBLANES, D_MODEL)
        _row_copies_start(tok_ref, nv,
                          lambda g, u, row: (obuf.at[slot, g, pl.ds(u, 1), :], out_hbm.at[pl.ds(row, 1), :]),
                          ssem.at[slot])

    @pl.when(j == nb - 1)
    def _():
        @pl.when(nb >= 2)
        def _():
            _rows_wait(obuf, other, nv_ref[jnp.maximum(j - 1, 0)], ssem)
        _rows_wait(obuf, slot, nv, ssem)


def _moe(x1r, slot_tok, block_ea, block_eb, block_nv, wg, wu, wd, lng, lnb):
    t = x1r.shape[0]
    n_blocks = block_ea.shape[0]
    r = EXPERT_ROWS
    tok3 = slot_tok.reshape(n_blocks, 1, r)
    const = lambda j, ea, eb, nv: (0, 0)
    grid_spec = pltpu.PrefetchScalarGridSpec(
        num_scalar_prefetch=3,
        grid=(n_blocks,),
        in_specs=[
            pl.BlockSpec((None, 1, r), lambda j, ea, eb, nv: (j, 0, 0), memory_space=pltpu.SMEM),
            pl.BlockSpec((None, 1, r), lambda j, ea, eb, nv: (jnp.minimum(j + 1, n_blocks - 1), 0, 0),
                         memory_space=pltpu.SMEM),
            pl.BlockSpec(memory_space=pl.ANY),
            pl.BlockSpec((None, D_MODEL, D_EXPERT), lambda j, ea, eb, nv: (ea[j], 0, 0)),
            pl.BlockSpec((None, D_MODEL, D_EXPERT), lambda j, ea, eb, nv: (ea[j], 0, 0)),
            pl.BlockSpec((None, D_EXPERT, D_MODEL), lambda j, ea, eb, nv: (ea[j], 0, 0)),
            pl.BlockSpec((None, D_MODEL, D_EXPERT), lambda j, ea, eb, nv: (eb[j], 0, 0)),
            pl.BlockSpec((None, D_MODEL, D_EXPERT), lambda j, ea, eb, nv: (eb[j], 0, 0)),
            pl.BlockSpec((None, D_EXPERT, D_MODEL), lambda j, ea, eb, nv: (eb[j], 0, 0)),
            pl.BlockSpec((1, D_MODEL), const),
            pl.BlockSpec((1, D_MODEL), const),
        ],
        out_specs=pl.BlockSpec(memory_space=pl.ANY),
        scratch_shapes=[pltpu.VMEM((2, r // SUBLANES, SUBLANES, X1R_COLS), F32),
                        pltpu.VMEM((2, r // SUBLANES, SUBLANES, D_MODEL), F32),
                        pltpu.SemaphoreType.DMA((2,)), pltpu.SemaphoreType.DMA((2,))],
    )
    return pl.pallas_call(
        _moe_kernel,
        grid_spec=grid_spec,
        out_shape=jax.ShapeDtypeStruct((t, D_MODEL), F32),
        compiler_params=_cparams(("arbitrary",)),
        name="experts_combine_ln2",
    )(block_ea, block_eb, block_nv, tok3, tok3, x1r, wg, wu, wd, wg, wu, wd, lng, lnb)


def _pair_tables():
    ea, eb = [], []
    for g in range(N_GROUPS):
        for a in range(EXPERTS_PER_GROUP):
            for b in range(a + 1, EXPERTS_PER_GROUP):
                ea.append(g * EXPERTS_PER_GROUP + a)
                eb.append(g * EXPERTS_PER_GROUP + b)
    return np.asarray(ea, np.int32), np.asarray(eb, np.int32)


def _routing_plan(cls, n_tokens):
    r = EXPERT_ROWS
    assert n_tokens & (n_tokens - 1) == 0
    classes = jnp.arange(N_CLASSES, dtype=jnp.int32)
    counts = jnp.sum((cls[:, None] == classes[None, :]).astype(jnp.int32), axis=0)
    padded = (counts + r - 1) // r * r
    ends = jnp.cumsum(padded)
    pstart = ends - padded
    n_blocks = n_tokens // r + N_CLASSES
    token_keys = cls * (2 * n_tokens) + jnp.arange(n_tokens, dtype=jnp.int32)
    filler_idx = jnp.arange(r, dtype=jnp.int32)[None, :]
    filler_keys = jnp.where(filler_idx < (padded - counts)[:, None],
                            classes[:, None] * (2 * n_tokens) + n_tokens + filler_idx,
                            N_CLASSES * 2 * n_tokens)
    keys = jnp.sort(jnp.concatenate([token_keys, filler_keys.reshape(-1)]))
    slot_tok = jnp.where((keys & n_tokens) != 0, 0, keys & (n_tokens - 1))
    block_start = jnp.arange(n_blocks, dtype=jnp.int32) * r
    block_cls = jnp.minimum(jnp.sum(ends[None, :] <= block_start[:, None], axis=1), N_CLASSES - 1)
    onehot_b = (block_cls[:, None] == classes[None, :]).astype(jnp.int32)
    ea_tab, eb_tab = _pair_tables()
    block_ea = jnp.sum(onehot_b * jnp.asarray(ea_tab)[None, :], axis=1).astype(jnp.int32)
    block_eb = jnp.sum(onehot_b * jnp.asarray(eb_tab)[None, :], axis=1).astype(jnp.int32)
    used = jnp.sum(onehot_b * (pstart + counts)[None, :], axis=1)
    block_nv = jnp.clip(used - block_start, 0, r).astype(jnp.int32)
    return slot_tok, block_ea, block_eb, block_nv


def _rotary_tables(s):
    half = RET_DK // 2
    inv = ROPE_BASE ** (-jnp.arange(half, dtype=F32) / half)
    ang = jnp.arange(s, dtype=F32)[:, None] * inv[None, :]
    cos, sin = jnp.cos(ang), jnp.sin(ang)
    return jnp.concatenate([cos, cos], axis=1), jnp.concatenate([-sin, sin], axis=1)


def _cols(w, off, width):
    return lax.slice_in_dim(w, off, off + width, axis=1)


def _layer(x, w_in, dec_f, dec_b, gn_g, w_proj_ret, rel_bias, w_proj_attn, w_out, ln1_g, ln1_b,
           rw_g, rb_g, rw_e, rb_e, w_gate, w_up, w_down, ln2_g, ln2_b):
    b, s, d = x.shape
    t = b * s
    x2d = x.reshape(t, d)
    for window, dil in DILATED_PATTERNS:
        assert window // (2 * dil) == ATT_RADIUS and (s // dil) % ATT_BQ == 0

    head_cols = [_cols(w_in, off + h * width, width) for h in range(RET_HEADS)
                 for off, width in ((REF_QR, RET_DK), (REF_KR, RET_DK), (REF_VR, RET_DV), (REF_GR, RET_DV))]
    w_main = jnp.concatenate(head_cols + [_cols(w_in, REF_BR, D_MODEL), _cols(w_in, REF_BA, D_MODEL)],
                             axis=1).astype(BF16)
    hcat = _proj_in(x2d, w_main)
    hcat3 = hcat.reshape(b, s, MAIN_COLS)

    dec = jnp.stack([jax.nn.log_sigmoid(dec_f.astype(F32)), jax.nn.log_sigmoid(dec_b.astype(F32))], axis=0)
    cos_t, sin_t = _rotary_tables(s)
    ret_act = _retention(hcat3, dec, cos_t, sin_t, gn_g.reshape(1, -1).astype(F32)).reshape(t, -1)

    w_pairs = jnp.concatenate(
        [_cols(w_in, off + j * ATT_PAIR_W, ATT_PAIR_W) for j in range(N_PAIRS) for off in (REF_QA, REF_KA, REF_VA)],
        axis=1).astype(BF16)
    attn = _attention(x, w_pairs, _attention_bias(rel_bias, s)).reshape(t, ATT_W)

    wr = jnp.zeros((d, LANES), F32).at[:, :N_GROUPS].set(rw_g.astype(F32))
    wr = wr.at[:, N_GROUPS:N_GROUPS + N_EXPERTS].set(rw_e.astype(F32))
    rb = jnp.zeros((1, LANES), F32).at[0, :N_GROUPS].set(rb_g.astype(F32))
    rb = rb.at[0, N_GROUPS:N_GROUPS + N_EXPERTS].set(rb_e.astype(F32))
    wrh = wr.astype(BF16)
    wrl = (wr - wrh.astype(F32)).astype(BF16)
    x1r = _merge(ret_act, attn, hcat, x2d,
                 w_proj_ret.astype(BF16), w_proj_attn.astype(BF16), w_out.astype(BF16),
                 ln1_g.reshape(1, -1).astype(F32), ln1_b.reshape(1, -1).astype(F32), wrh, wrl, rb)

    cls = x1r[:, D_MODEL + ROUTE_CLS].astype(jnp.int32)
    slot_tok, block_ea, block_eb, block_nv = _routing_plan(cls, t)
    out = _moe(x1r, slot_tok, block_ea, block_eb, block_nv,
               w_gate.astype(BF16), w_up.astype(BF16), w_down.astype(BF16),
               ln2_g.reshape(1, -1).astype(F32), ln2_b.reshape(1, -1).astype(F32))
    return out.reshape(b, s, d)


def kernel(x, w_in, ret_decay_fwd, ret_decay_bwd, ret_gn_g, w_proj_ret, rel_bias, w_proj_attn, w_out,
           ln1_g, ln1_b, router_w_group, router_b_group, router_w_expert, router_b_expert,
           w_gate, w_up, w_down, ln2_g, ln2_b):
    for l in range(DEPTH):
        x = _layer(x, w_in[l], ret_decay_fwd[l], ret_decay_bwd[l], ret_gn_g[l], w_proj_ret[l], rel_bias,
                   w_proj_attn[l], w_out[l], ln1_g[l], ln1_b[l], router_w_group[l], router_b_group[l],
                   router_w_expert[l], router_b_expert[l], w_gate[l], w_up[l], w_down[l],
                   ln2_g[l], ln2_b[l])
    return x
```

```python
import functools
import math

import numpy as np
import jax
import jax.numpy as jnp
from jax import lax
from jax.experimental import pallas as pl
from jax.experimental.pallas import tpu as pltpu

F32 = jnp.float32
BF16 = jnp.bfloat16

D_MODEL = 1024
RET_HEADS = 4
RET_DK = 128
RET_DV = 256
RET_CHUNK = 128
ROPE_BASE = 10000.0
ATT_HEADS = 8
ATT_DH = 64
ATT_W = ATT_HEADS * ATT_DH
DILATED_PATTERNS = ((128, 1), (512, 4), (2048, 16))
NUM_BUCKETS = 32
MAX_DISTANCE = 1024
N_GROUPS = 4
EXPERTS_PER_GROUP = 4
N_EXPERTS = N_GROUPS * EXPERTS_PER_GROUP
TOP_K = 2
D_EXPERT = 512
DEPTH = 1
DEEPNORM_ALPHA = (2.0 * DEPTH) ** 0.25
LN_EPS = 1e-5
NEG_INF = -1e30

REF_QR, REF_KR, REF_VR, REF_GR, REF_QA, REF_KA, REF_VA, REF_BR, REF_BA = (
    0, 512, 1024, 2048, 3072, 3584, 4096, 4608, 5632)
RET_HEAD_COLS = 2 * RET_DK + 2 * RET_DV
COL_BR = RET_HEADS * RET_HEAD_COLS
COL_BA = COL_BR + D_MODEL
MAIN_COLS = COL_BA + D_MODEL

V7X_VMEM_LIMIT_BYTES = 56 * 1024 * 1024
LANES = 128
SUBLANES = 8

RET_UNROLL = 8
PROJ_TM = 1024
PROJ_TN = 1280
ATT_BQ = 128
ATT_RADIUS = 64
ATT_KW = 2 * ATT_BQ
ATT_PAIR_W = 2 * ATT_DH
N_PAIRS = ATT_HEADS // 2
N_VARIANTS = 3
ATT_UNROLL = 16
MERGE_TM = 512
EXPERT_ROWS = 256
PAIRS_PER_GROUP = EXPERTS_PER_GROUP * (EXPERTS_PER_GROUP - 1) // 2
N_CLASSES = N_GROUPS * PAIRS_PER_GROUP
X1R_COLS = D_MODEL + LANES
ROUTE_W_LO, ROUTE_W_HI, ROUTE_CLS = 0, 1, 2


def _cparams(sem):
    return pltpu.CompilerParams(dimension_semantics=sem, vmem_limit_bytes=V7X_VMEM_LIMIT_BYTES)


def _proj_in_kernel(x_ref, w_ref, o_ref, xb_ref):
    @pl.when(pl.program_id(1) == 0)
    def _():
        xb_ref[...] = x_ref[...].astype(BF16)

    o_ref[...] = jnp.dot(xb_ref[...], w_ref[...], preferred_element_type=F32).astype(o_ref.dtype)


def _proj_in(x2d, w_bf16):
    t = x2d.shape[0]
    n_cols = w_bf16.shape[1]
    tm = min(PROJ_TM, t)
    return pl.pallas_call(
        _proj_in_kernel,
        grid=(t // tm, n_cols // PROJ_TN),
        in_specs=[pl.BlockSpec((tm, D_MODEL), lambda i, j: (i, 0)),
                  pl.BlockSpec((D_MODEL, PROJ_TN), lambda i, j: (0, j))],
        out_specs=pl.BlockSpec((tm, PROJ_TN), lambda i, j: (i, j)),
        out_shape=jax.ShapeDtypeStruct((t, n_cols), BF16),
        scratch_shapes=[pltpu.VMEM((tm, D_MODEL), BF16)],
        compiler_params=_cparams(("parallel", "arbitrary")),
        name="proj_in",
    )(x2d, w_bf16)


def _retention_kernel(dec_ref, qkvg_ref, cos_ref, sin_ref, gn_ref, o_ref,
                      kr_ref, kvf_ref, kvb_ref, pf_ref, nb_ref, *, n_chunks):
    c_len = RET_CHUNK
    q_cols = slice(0, RET_DK)
    k_cols = slice(RET_DK, 2 * RET_DK)
    v_cols = slice(2 * RET_DK, 2 * RET_DK + RET_DV)
    g_cols = slice(2 * RET_DK + RET_DV, RET_HEAD_COLS)
    h = pl.program_id(1)
    lgf = dec_ref[0, h]
    lgb = dec_ref[1, h]
    row = lax.broadcasted_iota(jnp.int32, (c_len, c_len), 0).astype(F32)
    col = lax.broadcasted_iota(jnp.int32, (c_len, c_len), 1).astype(F32)
    diff = row - col
    dmat = jnp.where(diff > 0, jnp.exp(lgf * jnp.maximum(diff, 0.0)),
                     jnp.where(diff < 0, jnp.exp(lgb * jnp.maximum(-diff, 0.0)), 2.0))
    zeta_f = jnp.exp(lgf * (c_len - 1.0 - row))
    zeta_b = jnp.exp(lgb * row)
    xi_f = jnp.exp(lgf * (row + 1.0))
    xi_b = jnp.exp(lgb * (c_len - row))
    zero_row = jnp.zeros((1, RET_DV), F32)
    decay_f = jnp.exp(zero_row + lgf * c_len)
    decay_b = jnp.exp(zero_row + lgb * c_len)
    k_scale = RET_DK ** -0.5
    tn_dims = (((0,), (0,)), ((), ()))
    nt_dims = (((1,), (1,)), ((), ()))

    def rows_of(c):
        return pl.ds(pl.multiple_of(c * c_len, c_len), c_len)

    def rotary(t, rows):
        return t * cos_ref[rows, :] + pltpu.roll(t, RET_DK // 2, 1) * sin_ref[rows, :]

    def kv_body(c, carry):
        rows = rows_of(c)
        kr = rotary(qkvg_ref[rows, k_cols].astype(F32), rows) * k_scale
        kr_ref[rows, :] = kr.astype(BF16)
        vb = qkvg_ref[rows, v_cols]
        kvf_ref[c] = lax.dot_general((kr * zeta_f).astype(BF16), vb, tn_dims, preferred_element_type=F32)
        kvb_ref[c] = lax.dot_general((kr * zeta_b).astype(BF16), vb, tn_dims, preferred_element_type=F32)
        return carry

    lax.fori_loop(0, n_chunks, kv_body, 0, unroll=RET_UNROLL)

    def scan_f(c, st):
        pf_ref[c] = st.astype(BF16)
        return st * decay_f + kvf_ref[c]

    lax.fori_loop(0, n_chunks, scan_f, jnp.zeros((RET_DK, RET_DV), F32))

    def scan_b(i, st):
        c = n_chunks - 1 - i
        nb_ref[c] = st.astype(BF16)
        return st * decay_b + kvb_ref[c]

    lax.fori_loop(0, n_chunks, scan_b, jnp.zeros((RET_DK, RET_DV), F32))

    def out_body(c, carry):
        rows = rows_of(c)
        qr = rotary(qkvg_ref[rows, q_cols].astype(F32), rows)
        vb = qkvg_ref[rows, v_cols]
        scores = lax.dot_general(qr.astype(BF16), kr_ref[rows, :], nt_dims, preferred_element_type=F32)
        o = jnp.dot((scores * dmat).astype(BF16), vb, preferred_element_type=F32)
        o += jnp.dot((qr * xi_f).astype(BF16), pf_ref[c], preferred_element_type=F32)
        o += jnp.dot((qr * xi_b).astype(BF16), nb_ref[c], preferred_element_type=F32)
        mu = jnp.mean(o, axis=-1, keepdims=True)
        d = o - mu
        var = jnp.mean(d * d, axis=-1, keepdims=True)
        y = d * lax.rsqrt(var + LN_EPS) * gn_ref[...]
        gate = qkvg_ref[rows, g_cols].astype(F32)
        o_ref[rows, :] = (gate * jax.nn.sigmoid(gate) * y).astype(o_ref.dtype)
        return carry

    lax.fori_loop(0, n_chunks, out_body, 0, unroll=RET_UNROLL)


def _retention(hcat3, dec, cos_t, sin_t, gn_g):
    b, s, _ = hcat3.shape
    n_chunks = s // RET_CHUNK
    kern = functools.partial(_retention_kernel, n_chunks=n_chunks)
    return pl.pallas_call(
        kern,
        grid=(b, RET_HEADS),
        in_specs=[
            pl.BlockSpec(memory_space=pltpu.SMEM),
            pl.BlockSpec((None, s, RET_HEAD_COLS), lambda i, h: (i, 0, h)),
            pl.BlockSpec((s, RET_DK), lambda i, h: (0, 0)),
            pl.BlockSpec((s, RET_DK), lambda i, h: (0, 0)),
            pl.BlockSpec((1, RET_DV), lambda i, h: (0, h)),
        ],
        out_specs=pl.BlockSpec((None, s, RET_DV), lambda i, h: (i, 0, h)),
        out_shape=jax.ShapeDtypeStruct((b, s, RET_HEADS * RET_DV), BF16),
        scratch_shapes=[pltpu.VMEM((s, RET_DK), BF16),
                        pltpu.VMEM((n_chunks, RET_DK, RET_DV), F32),
                        pltpu.VMEM((n_chunks, RET_DK, RET_DV), F32),
                        pltpu.VMEM((n_chunks, RET_DK, RET_DV), BF16),
                        pltpu.VMEM((n_chunks, RET_DK, RET_DV), BF16)],
        compiler_params=_cparams(("parallel", "arbitrary")),
        name="retention",
    )(dec, hcat3, cos_t, sin_t, gn_g)


def _attention_kernel(x_ref, w_ref, bias_ref, o_ref,
                      xb_ref, qkv_ref, qkv_gathered_ref, qd_ref, kd_ref, vd_ref, op_ref, lp_ref, s_ref, p_ref,
                      m_ref, *, s):
    nt_dims = (((1,), (1,)), ((), ()))

    @pl.when(pl.program_id(1) == 0)
    def _():
        xb_ref[...] = x_ref[...].astype(BF16)

    vd_ref[:, ATT_PAIR_W:2 * ATT_PAIR_W] = jnp.ones((s, ATT_PAIR_W), BF16)

    proj_rows = 512
    for c in range(s // proj_rows):
        rows = slice(c * proj_rows, (c + 1) * proj_rows)
        acc = jnp.dot(xb_ref[rows, :], w_ref[...], preferred_element_type=F32)
        for t in range(3):
            qkv_ref[t, rows, :] = acc[:, t * ATT_PAIR_W:(t + 1) * ATT_PAIR_W]

    first_head = lax.broadcasted_iota(jnp.int32, (ATT_BQ, ATT_PAIR_W), 1) < ATT_DH

    def pattern(p, dil, prev_dil, src_ref, dst_ref):
        n = s // dil
        nq = n // ATT_BQ
        kw = min(ATT_KW, n)
        n_blocks = s // ATT_BQ

        assert dil % prev_dil == 0
        step = dil // prev_dil
        n_prev = s // prev_dil
        copy_rows = min(n, 512)

        def gather(r, carry):
            r_prev = lax.rem(r, prev_dil)
            j = r // prev_dil
            for c in range(n // copy_rows):
                first = r_prev * n_prev + j + step * c * copy_rows
                src = pl.ds(first, copy_rows) if step == 1 else pl.ds(first, copy_rows, stride=step)
                row0 = pl.multiple_of(r * n + c * copy_rows, copy_rows)
                dst = pl.ds(row0, copy_rows)
                q_raw = src_ref[0, src, :]
                k_raw = src_ref[1, src, :]
                v_raw = src_ref[2, src, :]
                if dst_ref is not None:
                    dst_ref[0, dst, :] = q_raw
                    dst_ref[1, dst, :] = k_raw
                    dst_ref[2, dst, :] = v_raw
                qf = q_raw * (ATT_DH ** -0.5)
                keep = lax.broadcasted_iota(jnp.int32, qf.shape, 1) < ATT_DH
                q_first = jnp.where(keep, qf, 0.0).astype(BF16)
                q_second = jnp.where(keep, 0.0, qf).astype(BF16)
                for k in range(copy_rows // ATT_BQ):
                    blk_rows = slice(k * ATT_BQ, (k + 1) * ATT_BQ)
                    tile0 = pl.multiple_of(2 * (row0 + k * ATT_BQ), 2 * ATT_BQ)
                    qd_ref[pl.ds(tile0, ATT_BQ), :] = q_first[blk_rows]
                    qd_ref[pl.ds(tile0 + ATT_BQ, ATT_BQ), :] = q_second[blk_rows]
                kd_ref[dst, :] = k_raw.astype(BF16)
                vd_ref[dst, 0:ATT_PAIR_W] = v_raw.astype(BF16)
            return carry

        lax.fori_loop(0, dil, gather, 0)

        def place(blk):
            r = blk // nq
            i = blk - r * nq
            q0 = i * ATT_BQ
            if nq == 1:
                kstart, var = 0, 0
            else:
                kstart = jnp.clip(q0 - ATT_RADIUS, 0, n - kw)
                var = jnp.where(i == 0, 0, jnp.where(i == nq - 1, 2, 1))
            k_rows = pl.ds(pl.multiple_of(r * n + kstart, ATT_RADIUS), kw)
            if dil == 1:
                dst = pl.ds(pl.multiple_of(q0, ATT_BQ), ATT_BQ)
            else:
                dst = pl.ds(r + dil * q0, ATT_BQ, stride=dil)
            return k_rows, var, dst

        def tile_rows(t):
            return pl.ds(pl.multiple_of(t * ATT_BQ, ATT_BQ), ATT_BQ)

        def pair_rows(blk):
            return pl.ds(pl.multiple_of(blk * 2 * ATT_BQ, 2 * ATT_BQ), 2 * ATT_BQ)

        def scores(blk, carry):
            k_rows, var, _ = place(blk)
            sc = lax.dot_general(qd_ref[pair_rows(blk), :], kd_ref[k_rows, :], nt_dims,
                                 preferred_element_type=F32)
            for hh in range(2):
                s_ref[tile_rows(2 * blk + hh), 0:kw] = (sc[hh * ATT_BQ:(hh + 1) * ATT_BQ]
                                                        + bias_ref[p, var, hh, :, 0:kw])
            return carry

        lax.fori_loop(0, n_blocks, scores, 0, unroll=ATT_UNROLL)

        def numerators(t, carry):
            rows = tile_rows(t)
            m = jnp.max(s_ref[rows, 0:kw], axis=-1, keepdims=True)
            p_ref[rows, 0:kw] = jnp.exp(s_ref[rows, 0:kw] - m).astype(BF16)
            m_ref[rows, :] = jnp.broadcast_to(m, (ATT_BQ, ATT_PAIR_W))
            return carry

        lax.fori_loop(0, 2 * n_blocks, numerators, 0, unroll=ATT_UNROLL)

        def values(blk, carry):
            k_rows, _, dst = place(blk)
            pv = jnp.dot(p_ref[pair_rows(blk), 0:kw], vd_ref[k_rows, :], preferred_element_type=F32)
            l = pv[:, ATT_PAIR_W:2 * ATT_PAIR_W]
            o = pv[:, 0:ATT_PAIR_W] / l
            lse = m_ref[pair_rows(blk), :] + jnp.log(l)
            op_ref[p, dst, :] = jnp.where(first_head, o[0:ATT_BQ], o[ATT_BQ:2 * ATT_BQ])
            lp_ref[p, dst, :] = jnp.where(first_head, lse[0:ATT_BQ], lse[ATT_BQ:2 * ATT_BQ])
            return carry

        lax.fori_loop(0, n_blocks, values, 0, unroll=ATT_UNROLL)

    prev_dil, src_ref, spare_ref = 1, qkv_ref, qkv_gathered_ref
    for p, (_, dil) in enumerate(DILATED_PATTERNS):
        keep_copy = dil != prev_dil and p + 1 < len(DILATED_PATTERNS)
        pattern(p, dil, prev_dil, src_ref, spare_ref if keep_copy else None)
        if keep_copy:
            prev_dil, src_ref, spare_ref = dil, spare_ref, src_ref

    mix_rows = 256
    for c in range(s // mix_rows):
        rows = slice(c * mix_rows, (c + 1) * mix_rows)
        l0, l1, l2 = lp_ref[0, rows, :], lp_ref[1, rows, :], lp_ref[2, rows, :]
        m = jnp.maximum(jnp.maximum(l0, l1), l2)
        e0, e1, e2 = jnp.exp(l0 - m), jnp.exp(l1 - m), jnp.exp(l2 - m)
        mixed = (e0 * op_ref[0, rows, :] + e1 * op_ref[1, rows, :] + e2 * op_ref[2, rows, :]) / (e0 + e1 + e2)
        o_ref[rows, :] = mixed.astype(o_ref.dtype)


def _attention(x3, w_pairs, bias):
    b, s, d = x3.shape
    kern = functools.partial(_attention_kernel, s=s)
    n_pat = len(DILATED_PATTERNS)
    return pl.pallas_call(
        kern,
        grid=(b, N_PAIRS),
        in_specs=[pl.BlockSpec((None, s, d), lambda i, j: (i, 0, 0)),
                  pl.BlockSpec((d, 3 * ATT_PAIR_W), lambda i, j: (0, j)),
                  pl.BlockSpec((n_pat, N_VARIANTS, 2, ATT_BQ, ATT_KW), lambda i, j: (0, 0, j, 0, 0))],
        out_specs=pl.BlockSpec((None, s, ATT_PAIR_W), lambda i, j: (i, 0, j)),
        out_shape=jax.ShapeDtypeStruct((b, s, ATT_W), BF16),
        scratch_shapes=[pltpu.VMEM((s, d), BF16),
                        pltpu.VMEM((3, s, ATT_PAIR_W), F32),
                        pltpu.VMEM((3, s, ATT_PAIR_W), F32),
                        pltpu.VMEM((2 * s, ATT_PAIR_W), BF16),
                        pltpu.VMEM((s, ATT_PAIR_W), BF16),
                        pltpu.VMEM((s, 2 * ATT_PAIR_W), BF16),
                        pltpu.VMEM((n_pat, s, ATT_PAIR_W), F32),
                        pltpu.VMEM((n_pat, s, ATT_PAIR_W), F32),
                        pltpu.VMEM((2 * s, ATT_KW), F32),
                        pltpu.VMEM((2 * s, ATT_KW), BF16),
                        pltpu.VMEM((2 * s, ATT_PAIR_W), F32)],
        compiler_params=_cparams(("parallel", "arbitrary")),
        name="attention",
    )(x3, w_pairs, bias)


def _t5_bucket_np(rel):
    half = NUM_BUCKETS // 2
    max_exact = half // 2
    side = np.where(rel > 0, half, 0)
    n = np.abs(rel)
    large = max_exact + (np.log(np.maximum(n, 1).astype(np.float32) / max_exact)
                         / math.log(MAX_DISTANCE / max_exact) * (half - max_exact)).astype(np.int32)
    large = np.minimum(large, half - 1)
    return side + np.where(n < max_exact, n, large)


def _attention_bias(rel_bias, s):
    n_pat = len(DILATED_PATTERNS)
    bucket = np.zeros((n_pat, N_VARIANTS, ATT_BQ, ATT_KW), np.int32)
    inside = np.zeros((n_pat, N_VARIANTS, ATT_BQ, ATT_KW), bool)
    rows = np.arange(ATT_BQ)[:, None]
    cols = np.arange(ATT_KW)[None, :]
    for p, (_, dil) in enumerate(DILATED_PATTERNS):
        n = s // dil
        kw = min(ATT_KW, n)
        offsets = (0,) if n == ATT_BQ else (0, ATT_RADIUS, ATT_BQ)
        for v, off in enumerate(offsets):
            delta = cols - off - rows
            inside[p, v] = (np.abs(delta) <= ATT_RADIUS) & (cols < kw)
            bucket[p, v] = _t5_bucket_np(np.clip(delta, -ATT_RADIUS, ATT_RADIUS) * dil)
    bucket_j = jnp.asarray(bucket)[:, :, None]
    table = rel_bias.astype(F32)
    bias = jnp.zeros((n_pat, N_VARIANTS, ATT_HEADS, ATT_BQ, ATT_KW), F32)
    for bk in range(NUM_BUCKETS):
        bias = jnp.where(bucket_j == bk, table[bk][None, None, :, None, None], bias)
    return jnp.where(jnp.asarray(inside)[:, :, None], bias, NEG_INF)


def _layer_norm(z, g, b):
    mu = jnp.mean(z, axis=-1, keepdims=True)
    d = z - mu
    var = jnp.mean(d * d, axis=-1, keepdims=True)
    return d * lax.rsqrt(var + LN_EPS) * g + b


def _split_bf16(a):
    hi = a.astype(BF16)
    lo = (a - hi.astype(F32)).astype(BF16)
    return hi, lo


def _merge_kernel(ret_ref, att_ref, br_ref, ba_ref, x_ref,
                  wpr_ref, wpa_ref, wout_ref, lng_ref, lnb_ref, wrh_ref, wrl_ref, rb_ref,
                  x1r_ref):
    y_r = jnp.dot(ret_ref[...], wpr_ref[...], preferred_element_type=F32)
    y_a = jnp.dot(att_ref[...], wpa_ref[...], preferred_element_type=F32)
    merged = (jax.nn.sigmoid(br_ref[...].astype(F32)) * y_r
              + jax.nn.sigmoid(ba_ref[...].astype(F32)) * y_a)
    mix = jnp.dot(merged.astype(BF16), wout_ref[...], preferred_element_type=F32)
    x1 = _layer_norm(DEEPNORM_ALPHA * x_ref[...] + mix, lng_ref[...], lnb_ref[...])
    x1r_ref[:, 0:D_MODEL] = x1

    xh, xl = _split_bf16(x1)
    logits = (jnp.dot(xh, wrh_ref[...], preferred_element_type=F32)
              + jnp.dot(xh, wrl_ref[...], preferred_element_type=F32)
              + jnp.dot(xl, wrh_ref[...], preferred_element_type=F32)) + rb_ref[...]
    tm = logits.shape[0]
    lane = lax.broadcasted_iota(jnp.int32, (tm, LANES), 1)
    lane_f = lane.astype(F32)
    big = float(LANES)
    gmask = lane < N_GROUPS
    gl = jnp.where(gmask, logits, -jnp.inf)
    gmax = jnp.max(gl, axis=-1, keepdims=True)
    g_idx = jnp.min(jnp.where(gl == gmax, lane_f, big), axis=-1, keepdims=True)
    g_w = 1.0 / jnp.sum(jnp.where(gmask, jnp.exp(gl - gmax), 0.0), axis=-1, keepdims=True)
    in_group = jnp.floor((lane_f - N_GROUPS) * (1.0 / EXPERTS_PER_GROUP)) == g_idx
    emask = (lane >= N_GROUPS) & (lane < N_GROUPS + N_EXPERTS) & in_group
    el = jnp.where(emask, logits, -jnp.inf)
    v1 = jnp.max(el, axis=-1, keepdims=True)
    i1 = jnp.min(jnp.where(el == v1, lane_f, big), axis=-1, keepdims=True)
    el2 = jnp.where(lane_f == i1, -jnp.inf, el)
    v2 = jnp.max(el2, axis=-1, keepdims=True)
    i2 = jnp.min(jnp.where(el2 == v2, lane_f, big), axis=-1, keepdims=True)
    t = jnp.exp(v2 - v1)
    p1 = 1.0 / (1.0 + t)
    w1 = g_w * p1
    w2 = g_w * (t * p1)
    first_lo = i1 < i2
    e_lo = jnp.where(first_lo, i1, i2) - N_GROUPS
    e_hi = jnp.where(first_lo, i2, i1) - N_GROUPS
    a_loc = e_lo - EXPERTS_PER_GROUP * g_idx
    b_loc = e_hi - EXPERTS_PER_GROUP * g_idx
    pair = a_loc * (2 * EXPERTS_PER_GROUP - 1 - a_loc) * 0.5 + (b_loc - a_loc - 1.0)
    cls = g_idx * PAIRS_PER_GROUP + pair
    route = jnp.where(lane == ROUTE_W_LO, jnp.where(first_lo, w1, w2),
                      jnp.where(lane == ROUTE_W_HI, jnp.where(first_lo, w2, w1),
                                jnp.where(lane == ROUTE_CLS, cls, 0.0)))
    x1r_ref[:, D_MODEL:X1R_COLS] = route


def _merge(ret_act, attn, hcat, x2d, wpr, wpa, wout, lng, lnb, wrh, wrl, rb):
    t = x2d.shape[0]
    tm = min(MERGE_TM, t)
    row = lambda i: (i, 0)
    const = lambda i: (0, 0)
    in_specs = [
        pl.BlockSpec((tm, D_MODEL), row),
        pl.BlockSpec((tm, ATT_W), row),
        pl.BlockSpec((tm, D_MODEL), lambda i: (i, COL_BR // D_MODEL)),
        pl.BlockSpec((tm, D_MODEL), lambda i: (i, COL_BA // D_MODEL)),
        pl.BlockSpec((tm, D_MODEL), row),
        pl.BlockSpec((D_MODEL, D_MODEL), const),
        pl.BlockSpec((ATT_W, D_MODEL), const),
        pl.BlockSpec((D_MODEL, D_MODEL), const),
        pl.BlockSpec((1, D_MODEL), const),
        pl.BlockSpec((1, D_MODEL), const),
        pl.BlockSpec((D_MODEL, LANES), const),
        pl.BlockSpec((D_MODEL, LANES), const),
        pl.BlockSpec((1, LANES), const)]
    return pl.pallas_call(
        _merge_kernel,
        grid=(t // tm,),
        in_specs=in_specs,
        out_specs=pl.BlockSpec((tm, X1R_COLS), row),
        out_shape=jax.ShapeDtypeStruct((t, X1R_COLS), F32),
        compiler_params=_cparams(("parallel",)),
        name="merge_ln1_router",
    )(ret_act, attn, hcat, hcat, x2d, wpr, wpa, wout, lng, lnb, wrh, wrl, rb)


def _row_copies_start(idx_ref, n_rows, copy_of, sem):
    def start(g, u, r):
        src, dst = copy_of(g, u, idx_ref[0, r])
        pltpu.make_async_copy(src, dst, sem).start()

    def group(g, carry):
        for u in range(SUBLANES):
            start(g, u, g * SUBLANES + u)
        return carry

    n_groups = n_rows // SUBLANES
    lax.fori_loop(0, n_groups, group, 0)

    def tail(r, carry):
        start(n_groups, r - n_groups * SUBLANES, r)
        return carry

    lax.fori_loop(n_groups * SUBLANES, n_rows, tail, 0)


def _rows_wait(buf, slot, n_rows, sem):
    m = buf.shape[1] * SUBLANES
    assert m & (m - 1) == 0
    while m >= 1:
        @pl.when((n_rows & m) != 0)
        def _(m=m):
            if m >= SUBLANES:
                view = buf.at[slot, pl.ds(0, m // SUBLANES)]
            else:
                view = buf.at[slot, 0, pl.ds(0, m)]
            pltpu.make_async_copy(view, view, sem.at[slot]).wait()
        m //= 2


def _swiglu(xb, wg_ref, wu_ref, wd_ref):
    gate = jnp.dot(xb, wg_ref[...], preferred_element_type=F32)
    up = jnp.dot(xb, wu_ref[...], preferred_element_type=F32)
    hid = (gate * jax.nn.sigmoid(gate) * up).astype(BF16)
    return jnp.dot(hid, wd_ref[...], preferred_element_type=F32)


def _moe_kernel(ea_ref, eb_ref, nv_ref, tok_ref, tok_next_ref, x_hbm,
                wga_ref, wua_ref, wda_ref, wgb_ref, wub_ref, wdb_ref, lng_ref, lnb_ref,
                out_hbm, xbuf, obuf, gsem, ssem):
    del ea_ref, eb_ref
    j = pl.program_id(0)
    nb = pl.num_programs(0)
    slot = lax.rem(j, 2)
    other = 1 - slot
    nv = nv_ref[j]

    def gather_start(idx_ref, n_rows, slot_):
        _row_copies_start(idx_ref, n_rows,
                          lambda g, u, row: (x_hbm.at[pl.ds(row, 1), :], xbuf.at[slot_, g, pl.ds(u, 1), :]),
                          gsem.at[slot_])

    @pl.when(j == 0)
    def _():
        xbuf[...] = jnp.zeros_like(xbuf)
        gather_start(tok_ref, nv, 0)

    @pl.when(j + 1 < nb)
    def _():
        gather_start(tok_next_ref, nv_ref[jnp.minimum(j + 1, nb - 1)], other)

    _rows_wait(xbuf, slot, nv, gsem)

    @pl.when(j >= 2)
    def _():
        _rows_wait(obuf, slot, nv_ref[jnp.maximum(j - 2, 0)], ssem)

    @pl.when(nv > 0)
    def _():
        xr = xbuf[slot].reshape(EXPERT_ROWS, X1R_COLS)
        x1 = xr[:, 0:D_MODEL]
        w_a = xr[:, D_MODEL + ROUTE_W_LO:D_MODEL + ROUTE_W_LO + 1]
        w_b = xr[:, D_MODEL + ROUTE_W_HI:D_MODEL + ROUTE_W_HI + 1]
        xb = x1.astype(BF16)
        ffn = w_a * _swiglu(xb, wga_ref, wua_ref, wda_ref) + w_b * _swiglu(xb, wgb_ref, wub_ref, wdb_ref)
        out = _layer_norm(DEEPNORM_ALPHA * x1 + ffn, lng_ref[...], lnb_ref[...])
        obuf[slot] = out.reshape(EXPERT_ROWS // SUBLANES, SUBLANES, D_MODEL)
        _row_copies_start(tok_ref, nv,
                          lambda g, u, row: (obuf.at[slot, g, pl.ds(u, 1), :], out_hbm.at[pl.ds(row, 1), :]),
                          ssem.at[slot])

    @pl.when(j == nb - 1)
    def _():
        @pl.when(nb >= 2)
        def _():
            _rows_wait(obuf, other, nv_ref[jnp.maximum(j - 1, 0)], ssem)
        _rows_wait(obuf, slot, nv, ssem)


def _moe(x1r, slot_tok, block_ea, block_eb, block_nv, wg, wu, wd, lng, lnb):
    t = x1r.shape[0]
    n_blocks = block_ea.shape[0]
    r = EXPERT_ROWS
    tok3 = slot_tok.reshape(n_blocks, 1, r)
    const = lambda j, ea, eb, nv: (0, 0)
    grid_spec = pltpu.PrefetchScalarGridSpec(
        num_scalar_prefetch=3,
        grid=(n_blocks,),
        in_specs=[
            pl.BlockSpec((None, 1, r), lambda j, ea, eb, nv: (j, 0, 0), memory_space=pltpu.SMEM),
            pl.BlockSpec((None, 1, r), lambda j, ea, eb, nv: (jnp.minimum(j + 1, n_blocks - 1), 0, 0),
                         memory_space=pltpu.SMEM),
            pl.BlockSpec(memory_space=pl.ANY),
            pl.BlockSpec((None, D_MODEL, D_EXPERT), lambda j, ea, eb, nv: (ea[j], 0, 0)),
            pl.BlockSpec((None, D_MODEL, D_EXPERT), lambda j, ea, eb, nv: (ea[j], 0, 0)),
            pl.BlockSpec((None, D_EXPERT, D_MODEL), lambda j, ea, eb, nv: (ea[j], 0, 0)),
            pl.BlockSpec((None, D_MODEL, D_EXPERT), lambda j, ea, eb, nv: (eb[j], 0, 0)),
            pl.BlockSpec((None, D_MODEL, D_EXPERT), lambda j, ea, eb, nv: (eb[j], 0, 0)),
            pl.BlockSpec((None, D_EXPERT, D_MODEL), lambda j, ea, eb, nv: (eb[j], 0, 0)),
            pl.BlockSpec((1, D_MODEL), const),
            pl.BlockSpec((1, D_MODEL), const),
        ],
        out_specs=pl.BlockSpec(memory_space=pl.ANY),
        scratch_shapes=[pltpu.VMEM((2, r // SUBLANES, SUBLANES, X1R_COLS), F32),
                        pltpu.VMEM((2, r // SUBLANES, SUBLANES, D_MODEL), F32),
                        pltpu.SemaphoreType.DMA((2,)), pltpu.SemaphoreType.DMA((2,))],
    )
    return pl.pallas_call(
        _moe_kernel,
        grid_spec=grid_spec,
        out_shape=jax.ShapeDtypeStruct((t, D_MODEL), F32),
        compiler_params=_cparams(("arbitrary",)),
        name="experts_combine_ln2",
    )(block_ea, block_eb, block_nv, tok3, tok3, x1r, wg, wu, wd, wg, wu, wd, lng, lnb)


def _pair_tables():
    ea, eb = [], []
    for g in range(N_GROUPS):
        for a in range(EXPERTS_PER_GROUP):
            for b in range(a + 1, EXPERTS_PER_GROUP):
                ea.append(g * EXPERTS_PER_GROUP + a)
                eb.append(g * EXPERTS_PER_GROUP + b)
    return np.asarray(ea, np.int32), np.asarray(eb, np.int32)


def _routing_plan(cls, n_tokens):
    r = EXPERT_ROWS
    assert n_tokens & (n_tokens - 1) == 0
    classes = jnp.arange(N_CLASSES, dtype=jnp.int32)
    counts = jnp.sum((cls[:, None] == classes[None, :]).astype(jnp.int32), axis=0)
    padded = (counts + r - 1) // r * r
    ends = jnp.cumsum(padded)
    pstart = ends - padded
    n_blocks = n_tokens // r + N_CLASSES
    token_keys = cls * (2 * n_tokens) + jnp.arange(n_tokens, dtype=jnp.int32)
    filler_idx = jnp.arange(r, dtype=jnp.int32)[None, :]
    filler_keys = jnp.where(filler_idx < (padded - counts)[:, None],
                            classes[:, None] * (2 * n_tokens) + n_tokens + filler_idx,
                            N_CLASSES * 2 * n_tokens)
    keys = jnp.sort(jnp.concatenate([token_keys, filler_keys.reshape(-1)]))
    slot_tok = jnp.where((keys & n_tokens) != 0, 0, keys & (n_tokens - 1))
    block_start = jnp.arange(n_blocks, dtype=jnp.int32) * r
    block_cls = jnp.minimum(jnp.sum(ends[None, :] <= block_start[:, None], axis=1), N_CLASSES - 1)
    onehot_b = (block_cls[:, None] == classes[None, :]).astype(jnp.int32)
    ea_tab, eb_tab = _pair_tables()
    block_ea = jnp.sum(onehot_b * jnp.asarray(ea_tab)[None, :], axis=1).astype(jnp.int32)
    block_eb = jnp.sum(onehot_b * jnp.asarray(eb_tab)[None, :], axis=1).astype(jnp.int32)
    used = jnp.sum(onehot_b * (pstart + counts)[None, :], axis=1)
    block_nv = jnp.clip(used - block_start, 0, r).astype(jnp.int32)
    return slot_tok, block_ea, block_eb, block_nv


def _rotary_tables(s):
    half = RET_DK // 2
    inv = ROPE_BASE ** (-jnp.arange(half, dtype=F32) / half)
    ang = jnp.arange(s, dtype=F32)[:, None] * inv[None, :]
    cos, sin = jnp.cos(ang), jnp.sin(ang)
    return jnp.concatenate([cos, cos], axis=1), jnp.concatenate([-sin, sin], axis=1)


def _cols(w, off, width):
    return lax.slice_in_dim(w, off, off + width, axis=1)


def _layer(x, w_in, dec_f, dec_b, gn_g, w_proj_ret, rel_bias, w_proj_attn, w_out, ln1_g, ln1_b,
           rw_g, rb_g, rw_e, rb_e, w_gate, w_up, w_down, ln2_g, ln2_b):
    b, s, d = x.shape
    t = b * s
    x2d = x.reshape(t, d)
    for window, dil in DILATED_PATTERNS:
        assert window // (2 * dil) == ATT_RADIUS and (s // dil) % ATT_BQ == 0

    head_cols = [_cols(w_in, off + h * width, width) for h in range(RET_HEADS)
                 for off, width in ((REF_QR, RET_DK), (REF_KR, RET_DK), (REF_VR, RET_DV), (REF_GR, RET_DV))]
    w_main = jnp.concatenate(head_cols + [_cols(w_in, REF_BR, D_MODEL), _cols(w_in, REF_BA, D_MODEL)],
                             axis=1).astype(BF16)
    hcat = _proj_in(x2d, w_main)
    hcat3 = hcat.reshape(b, s, MAIN_COLS)

    dec = jnp.stack([jax.nn.log_sigmoid(dec_f.astype(F32)), jax.nn.log_sigmoid(dec_b.astype(F32))], axis=0)
    cos_t, sin_t = _rotary_tables(s)
    ret_act = _retention(hcat3, dec, cos_t, sin_t, gn_g.reshape(1, -1).astype(F32)).reshape(t, -1)

    w_pairs = jnp.concatenate(
        [_cols(w_in, off + j * ATT_PAIR_W, ATT_PAIR_W) for j in range(N_PAIRS) for off in (REF_QA, REF_KA, REF_VA)],
        axis=1).astype(BF16)
    attn = _attention(x, w_pairs, _attention_bias(rel_bias, s)).reshape(t, ATT_W)

    wr = jnp.zeros((d, LANES), F32).at[:, :N_GROUPS].set(rw_g.astype(F32))
    wr = wr.at[:, N_GROUPS:N_GROUPS + N_EXPERTS].set(rw_e.astype(F32))
    rb = jnp.zeros((1, LANES), F32).at[0, :N_GROUPS].set(rb_g.astype(F32))
    rb = rb.at[0, N_GROUPS:N_GROUPS + N_EXPERTS].set(rb_e.astype(F32))
    wrh = wr.astype(BF16)
    wrl = (wr - wrh.astype(F32)).astype(BF16)
    x1r = _merge(ret_act, attn, hcat, x2d,
                 w_proj_ret.astype(BF16), w_proj_attn.astype(BF16), w_out.astype(BF16),
                 ln1_g.reshape(1, -1).astype(F32), ln1_b.reshape(1, -1).astype(F32), wrh, wrl, rb)

    cls = x1r[:, D_MODEL + ROUTE_CLS].astype(jnp.int32)
    slot_tok, block_ea, block_eb, block_nv = _routing_plan(cls, t)
    out = _moe(x1r, slot_tok, block_ea, block_eb, block_nv,
               w_gate.astype(BF16), w_up.astype(BF16), w_down.astype(BF16),
               ln2_g.reshape(1, -1).astype(F32), ln2_b.reshape(1, -1).astype(F32))
    return out.reshape(b, s, d)


def kernel(x, w_in, ret_decay_fwd, ret_decay_bwd, ret_gn_g, w_proj_ret, rel_bias, w_proj_attn, w_out,
           ln1_g, ln1_b, router_w_group, router_b_group, router_w_expert, router_b_expert,
           w_gate, w_up, w_down, ln2_g, ln2_b):
    for l in range(DEPTH):
        x = _layer(x, w_in[l], ret_decay_fwd[l], ret_decay_bwd[l], ret_gn_g[l], w_proj_ret[l], rel_bias,
                   w_proj_attn[l], w_out[l], ln1_g[l], ln1_b[l], router_w_group[l], router_b_group[l],
                   router_w_expert[l], router_b_expert[l], w_gate[l], w_up[l], w_down[l],
                   ln2_g[l], ln2_b[l])
    return x
```

```python
import functools
import math

import numpy as np
import jax
import jax.numpy as jnp
from jax import lax
from jax.experimental import pallas as pl
from jax.experimental.pallas import tpu as pltpu

F32 = jnp.float32
BF16 = jnp.bfloat16

D_MODEL = 1024
RET_HEADS = 4
RET_DK = 128
RET_DV = 256
RET_CHUNK = 128
ROPE_BASE = 10000.0
ATT_HEADS = 8
ATT_DH = 64
ATT_W = ATT_HEADS * ATT_DH
DILATED_PATTERNS = ((128, 1), (512, 4), (2048, 16))
NUM_BUCKETS = 32
MAX_DISTANCE = 1024
N_GROUPS = 4
EXPERTS_PER_GROUP = 4
N_EXPERTS = N_GROUPS * EXPERTS_PER_GROUP
TOP_K = 2
D_EXPERT = 512
DEPTH = 1
DEEPNORM_ALPHA = (2.0 * DEPTH) ** 0.25
LN_EPS = 1e-5
NEG_INF = -1e30

REF_QR, REF_KR, REF_VR, REF_GR, REF_QA, REF_KA, REF_VA, REF_BR, REF_BA = (
    0, 512, 1024, 2048, 3072, 3584, 4096, 4608, 5632)
RET_HEAD_COLS = 2 * RET_DK + 2 * RET_DV
COL_BR = RET_HEADS * RET_HEAD_COLS
COL_BA = COL_BR + D_MODEL
MAIN_COLS = COL_BA + D_MODEL

V7X_VMEM_LIMIT_BYTES = 56 * 1024 * 1024
LANES = 128
SUBLANES = 8

RET_UNROLL = 16
PROJ_TM = 1024
PROJ_TN = 1280
ATT_BQ = 128
ATT_RADIUS = 64
ATT_KW = 2 * ATT_BQ
ATT_PAIR_W = 2 * ATT_DH
N_PAIRS = ATT_HEADS // 2
N_VARIANTS = 3
ATT_UNROLL = 16
MERGE_TM = 512
EXPERT_ROWS = 256
PAIRS_PER_GROUP = EXPERTS_PER_GROUP * (EXPERTS_PER_GROUP - 1) // 2
N_CLASSES = N_GROUPS * PAIRS_PER_GROUP
X1R_COLS = D_MODEL + LANES
ROUTE_W_LO, ROUTE_W_HI, ROUTE_CLS = 0, 1, 2


def _cparams(sem):
    return pltpu.CompilerParams(dimension_semantics=sem, vmem_limit_bytes=V7X_VMEM_LIMIT_BYTES)


def _proj_in_kernel(x_ref, w_ref, o_ref, xb_ref):
    @pl.when(pl.program_id(1) == 0)
    def _():
        xb_ref[...] = x_ref[...].astype(BF16)

    o_ref[...] = jnp.dot(xb_ref[...], w_ref[...], preferred_element_type=F32).astype(o_ref.dtype)


def _proj_in(x2d, w_bf16):
    t = x2d.shape[0]
    n_cols = w_bf16.shape[1]
    tm = min(PROJ_TM, t)
    return pl.pallas_call(
        _proj_in_kernel,
        grid=(t // tm, n_cols // PROJ_TN),
        in_specs=[pl.BlockSpec((tm, D_MODEL), lambda i, j: (i, 0)),
                  pl.BlockSpec((D_MODEL, PROJ_TN), lambda i, j: (0, j))],
        out_specs=pl.BlockSpec((tm, PROJ_TN), lambda i, j: (i, j)),
        out_shape=jax.ShapeDtypeStruct((t, n_cols), BF16),
        scratch_shapes=[pltpu.VMEM((tm, D_MODEL), BF16)],
        compiler_params=_cparams(("parallel", "arbitrary")),
        name="proj_in",
    )(x2d, w_bf16)


def _retention_kernel(dec_ref, qkvg_ref, cos_ref, sin_ref, gn_ref, o_ref,
                      kr_ref, kvf_ref, kvb_ref, pf_ref, nb_ref, *, n_chunks):
    c_len = RET_CHUNK
    q_cols = slice(0, RET_DK)
    k_cols = slice(RET_DK, 2 * RET_DK)
    v_cols = slice(2 * RET_DK, 2 * RET_DK + RET_DV)
    g_cols = slice(2 * RET_DK + RET_DV, RET_HEAD_COLS)
    h = pl.program_id(1)
    lgf = dec_ref[0, h]
    lgb = dec_ref[1, h]
    row = lax.broadcasted_iota(jnp.int32, (c_len, c_len), 0).astype(F32)
    col = lax.broadcasted_iota(jnp.int32, (c_len, c_len), 1).astype(F32)
    diff = row - col
    dmat = jnp.where(diff > 0, jnp.exp(lgf * jnp.maximum(diff, 0.0)),
                     jnp.where(diff < 0, jnp.exp(lgb * jnp.maximum(-diff, 0.0)), 2.0))
    zeta_f = jnp.exp(lgf * (c_len - 1.0 - row))
    zeta_b = jnp.exp(lgb * row)
    xi_f = jnp.exp(lgf * (row + 1.0))
    xi_b = jnp.exp(lgb * (c_len - row))
    zero_row = jnp.zeros((1, RET_DV), F32)
    decay_f = jnp.exp(zero_row + lgf * c_len)
    decay_b = jnp.exp(zero_row + lgb * c_len)
    k_scale = RET_DK ** -0.5
    tn_dims = (((0,), (0,)), ((), ()))
    nt_dims = (((1,), (1,)), ((), ()))

    def rows_of(c):
        return pl.ds(pl.multiple_of(c * c_len, c_len), c_len)

    def rotary(t, rows):
        return t * cos_ref[rows, :] + pltpu.roll(t, RET_DK // 2, 1) * sin_ref[rows, :]

    def kv_body(c, carry):
        rows = rows_of(c)
        kr = rotary(qkvg_ref[rows, k_cols].astype(F32), rows) * k_scale
        kr_ref[rows, :] = kr.astype(BF16)
        vb = qkvg_ref[rows, v_cols]
        kvf_ref[c] = lax.dot_general((kr * zeta_f).astype(BF16), vb, tn_dims, preferred_element_type=F32)
        kvb_ref[c] = lax.dot_general((kr * zeta_b).astype(BF16), vb, tn_dims, preferred_element_type=F32)
        return carry

    lax.fori_loop(0, n_chunks, kv_body, 0, unroll=RET_UNROLL)

    def scan_f(c, st):
        pf_ref[c] = st.astype(BF16)
        return st * decay_f + kvf_ref[c]

    lax.fori_loop(0, n_chunks, scan_f, jnp.zeros((RET_DK, RET_DV), F32))

    def scan_b(i, st):
        c = n_chunks - 1 - i
        nb_ref[c] = st.astype(BF16)
        return st * decay_b + kvb_ref[c]

    lax.fori_loop(0, n_chunks, scan_b, jnp.zeros((RET_DK, RET_DV), F32))

    def out_body(c, carry):
        rows = rows_of(c)
        qr = rotary(qkvg_ref[rows, q_cols].astype(F32), rows)
        vb = qkvg_ref[rows, v_cols]
        scores = lax.dot_general(qr.astype(BF16), kr_ref[rows, :], nt_dims, preferred_element_type=F32)
        o = jnp.dot((scores * dmat).astype(BF16), vb, preferred_element_type=F32)
        o += jnp.dot((qr * xi_f).astype(BF16), pf_ref[c], preferred_element_type=F32)
        o += jnp.dot((qr * xi_b).astype(BF16), nb_ref[c], preferred_element_type=F32)
        mu = jnp.mean(o, axis=-1, keepdims=True)
        d = o - mu
        var = jnp.mean(d * d, axis=-1, keepdims=True)
        y = d * lax.rsqrt(var + LN_EPS) * gn_ref[...]
        gate = qkvg_ref[rows, g_cols].astype(F32)
        o_ref[rows, :] = (gate * jax.nn.sigmoid(gate) * y).astype(o_ref.dtype)
        return carry

    lax.fori_loop(0, n_chunks, out_body, 0, unroll=RET_UNROLL)


def _retention(hcat3, dec, cos_t, sin_t, gn_g):
    b, s, _ = hcat3.shape
    n_chunks = s // RET_CHUNK
    kern = functools.partial(_retention_kernel, n_chunks=n_chunks)
    return pl.pallas_call(
        kern,
        grid=(b, RET_HEADS),
        in_specs=[
            pl.BlockSpec(memory_space=pltpu.SMEM),
            pl.BlockSpec((None, s, RET_HEAD_COLS), lambda i, h: (i, 0, h)),
            pl.BlockSpec((s, RET_DK), lambda i, h: (0, 0)),
            pl.BlockSpec((s, RET_DK), lambda i, h: (0, 0)),
            pl.BlockSpec((1, RET_DV), lambda i, h: (0, h)),
        ],
        out_specs=pl.BlockSpec((None, s, RET_DV), lambda i, h: (i, 0, h)),
        out_shape=jax.ShapeDtypeStruct((b, s, RET_HEADS * RET_DV), BF16),
        scratch_shapes=[pltpu.VMEM((s, RET_DK), BF16),
                        pltpu.VMEM((n_chunks, RET_DK, RET_DV), F32),
                        pltpu.VMEM((n_chunks, RET_DK, RET_DV), F32),
                        pltpu.VMEM((n_chunks, RET_DK, RET_DV), BF16),
                        pltpu.VMEM((n_chunks, RET_DK, RET_DV), BF16)],
        compiler_params=_cparams(("parallel", "arbitrary")),
        name="retention",
    )(dec, hcat3, cos_t, sin_t, gn_g)


def _attention_kernel(x_ref, w_ref, bias_ref, o_ref,
                      xb_ref, qkv_ref, qkv_gathered_ref, qd_ref, kd_ref, vd_ref, op_ref, lp_ref, s_ref, p_ref,
                      m_ref, *, s):
    nt_dims = (((1,), (1,)), ((), ()))

    @pl.when(pl.program_id(1) == 0)
    def _():
        xb_ref[...] = x_ref[...].astype(BF16)

    vd_ref[:, ATT_PAIR_W:2 * ATT_PAIR_W] = jnp.ones((s, ATT_PAIR_W), BF16)

    proj_rows = 512
    for c in range(s // proj_rows):
        rows = slice(c * proj_rows, (c + 1) * proj_rows)
        acc = jnp.dot(xb_ref[rows, :], w_ref[...], preferred_element_type=F32)
        for t in range(3):
            qkv_ref[t, rows, :] = acc[:, t * ATT_PAIR_W:(t + 1) * ATT_PAIR_W]

    first_head = lax.broadcasted_iota(jnp.int32, (ATT_BQ, ATT_PAIR_W), 1) < ATT_DH

    def pattern(p, dil, prev_dil, src_ref, dst_ref):
        n = s // dil
        nq = n // ATT_BQ
        kw = min(ATT_KW, n)
        n_blocks = s // ATT_BQ

        assert dil % prev_dil == 0
        step = dil // prev_dil
        n_prev = s // prev_dil
        copy_rows = min(n, 512)

        def gather(r, carry):
            r_prev = lax.rem(r, prev_dil)
            j = r // prev_dil
            for c in range(n // copy_rows):
                first = r_prev * n_prev + j + step * c * copy_rows
                src = pl.ds(first, copy_rows) if step == 1 else pl.ds(first, copy_rows, stride=step)
                row0 = pl.multiple_of(r * n + c * copy_rows, copy_rows)
                dst = pl.ds(row0, copy_rows)
                q_raw = src_ref[0, src, :]
                k_raw = src_ref[1, src, :]
                v_raw = src_ref[2, src, :]
                if dst_ref is not None:
                    dst_ref[0, dst, :] = q_raw
                    dst_ref[1, dst, :] = k_raw
                    dst_ref[2, dst, :] = v_raw
                qf = q_raw * (ATT_DH ** -0.5)
                keep = lax.broadcasted_iota(jnp.int32, qf.shape, 1) < ATT_DH
                q_first = jnp.where(keep, qf, 0.0).astype(BF16)
                q_second = jnp.where(keep, 0.0, qf).astype(BF16)
                for k in range(copy_rows // ATT_BQ):
                    blk_rows = slice(k * ATT_BQ, (k + 1) * ATT_BQ)
                    tile0 = pl.multiple_of(2 * (row0 + k * ATT_BQ), 2 * ATT_BQ)
                    qd_ref[pl.ds(tile0, ATT_BQ), :] = q_first[blk_rows]
                    qd_ref[pl.ds(tile0 + ATT_BQ, ATT_BQ), :] = q_second[blk_rows]
                kd_ref[dst, :] = k_raw.astype(BF16)
                vd_ref[dst, 0:ATT_PAIR_W] = v_raw.astype(BF16)
            return carry

        lax.fori_loop(0, dil, gather, 0)

        def place(blk):
            r = blk // nq
            i = blk - r * nq
            q0 = i * ATT_BQ
            if nq == 1:
                kstart, var = 0, 0
            else:
                kstart = jnp.clip(q0 - ATT_RADIUS, 0, n - kw)
                var = jnp.where(i == 0, 0, jnp.where(i == nq - 1, 2, 1))
            k_rows = pl.ds(pl.multiple_of(r * n + kstart, ATT_RADIUS), kw)
            if dil == 1:
                dst = pl.ds(pl.multiple_of(q0, ATT_BQ), ATT_BQ)
            else:
                dst = pl.ds(r + dil * q0, ATT_BQ, stride=dil)
            return k_rows, var, dst

        def tile_rows(t):
            return pl.ds(pl.multiple_of(t * ATT_BQ, ATT_BQ), ATT_BQ)

        def pair_rows(blk):
            return pl.ds(pl.multiple_of(blk * 2 * ATT_BQ, 2 * ATT_BQ), 2 * ATT_BQ)

        def scores(blk, carry):
            k_rows, var, _ = place(blk)
            sc = lax.dot_general(qd_ref[pair_rows(blk), :], kd_ref[k_rows, :], nt_dims,
                                 preferred_element_type=F32)
            for hh in range(2):
                s_ref[tile_rows(2 * blk + hh), 0:kw] = (sc[hh * ATT_BQ:(hh + 1) * ATT_BQ]
                                                        + bias_ref[p, var, hh, :, 0:kw])
            return carry

        lax.fori_loop(0, n_blocks, scores, 0, unroll=ATT_UNROLL)

        def numerators(t, carry):
            rows = tile_rows(t)
            m = jnp.max(s_ref[rows, 0:kw], axis=-1, keepdims=True)
            p_ref[rows, 0:kw] = jnp.exp(s_ref[rows, 0:kw] - m).astype(BF16)
            m_ref[rows, :] = jnp.broadcast_to(m, (ATT_BQ, ATT_PAIR_W))
            return carry

        lax.fori_loop(0, 2 * n_blocks, numerators, 0, unroll=ATT_UNROLL)

        def values(blk, carry):
            k_rows, _, dst = place(blk)
            pv = jnp.dot(p_ref[pair_rows(blk), 0:kw], vd_ref[k_rows, :], preferred_element_type=F32)
            l = pv[:, ATT_PAIR_W:2 * ATT_PAIR_W]
            o = pv[:, 0:ATT_PAIR_W] / l
            lse = m_ref[pair_rows(blk), :] + jnp.log(l)
            op_ref[p, dst, :] = jnp.where(first_head, o[0:ATT_BQ], o[ATT_BQ:2 * ATT_BQ])
            lp_ref[p, dst, :] = jnp.where(first_head, lse[0:ATT_BQ], lse[ATT_BQ:2 * ATT_BQ])
            return carry

        lax.fori_loop(0, n_blocks, values, 0, unroll=ATT_UNROLL)

    prev_dil, src_ref, spare_ref = 1, qkv_ref, qkv_gathered_ref
    for p, (_, dil) in enumerate(DILATED_PATTERNS):
        keep_copy = dil != prev_dil and p + 1 < len(DILATED_PATTERNS)
        pattern(p, dil, prev_dil, src_ref, spare_ref if keep_copy else None)
        if keep_copy:
            prev_dil, src_ref, spare_ref = dil, spare_ref, src_ref

    mix_rows = 256
    for c in range(s // mix_rows):
        rows = slice(c * mix_rows, (c + 1) * mix_rows)
        l0, l1, l2 = lp_ref[0, rows, :], lp_ref[1, rows, :], lp_ref[2, rows, :]
        m = jnp.maximum(jnp.maximum(l0, l1), l2)
        e0, e1, e2 = jnp.exp(l0 - m), jnp.exp(l1 - m), jnp.exp(l2 - m)
        mixed = (e0 * op_ref[0, rows, :] + e1 * op_ref[1, rows, :] + e2 * op_ref[2, rows, :]) / (e0 + e1 + e2)
        o_ref[rows, :] = mixed.astype(o_ref.dtype)


def _attention(x3, w_pairs, bias):
    b, s, d = x3.shape
    kern = functools.partial(_attention_kernel, s=s)
    n_pat = len(DILATED_PATTERNS)
    return pl.pallas_call(
        kern,
        grid=(b, N_PAIRS),
        in_specs=[pl.BlockSpec((None, s, d), lambda i, j: (i, 0, 0)),
                  pl.BlockSpec((d, 3 * ATT_PAIR_W), lambda i, j: (0, j)),
                  pl.BlockSpec((n_pat, N_VARIANTS, 2, ATT_BQ, ATT_KW), lambda i, j: (0, 0, j, 0, 0))],
        out_specs=pl.BlockSpec((None, s, ATT_PAIR_W), lambda i, j: (i, 0, j)),
        out_shape=jax.ShapeDtypeStruct((b, s, ATT_W), BF16),
        scratch_shapes=[pltpu.VMEM((s, d), BF16),
                        pltpu.VMEM((3, s, ATT_PAIR_W), F32),
                        pltpu.VMEM((3, s, ATT_PAIR_W), F32),
                        pltpu.VMEM((2 * s, ATT_PAIR_W), BF16),
                        pltpu.VMEM((s, ATT_PAIR_W), BF16),
                        pltpu.VMEM((s, 2 * ATT_PAIR_W), BF16),
                        pltpu.VMEM((n_pat, s, ATT_PAIR_W), F32),
                        pltpu.VMEM((n_pat, s, ATT_PAIR_W), F32),
                        pltpu.VMEM((2 * s, ATT_KW), F32),
                        pltpu.VMEM((2 * s, ATT_KW), BF16),
                        pltpu.VMEM((2 * s, ATT_PAIR_W), F32)],
        compiler_params=_cparams(("parallel", "arbitrary")),
        name="attention",
    )(x3, w_pairs, bias)


def _t5_bucket_np(rel):
    half = NUM_BUCKETS // 2
    max_exact = half // 2
    side = np.where(rel > 0, half, 0)
    n = np.abs(rel)
    large = max_exact + (np.log(np.maximum(n, 1).astype(np.float32) / max_exact)
                         / math.log(MAX_DISTANCE / max_exact) * (half - max_exact)).astype(np.int32)
    large = np.minimum(large, half - 1)
    return side + np.where(n < max_exact, n, large)


def _attention_bias(rel_bias, s):
    n_pat = len(DILATED_PATTERNS)
    bucket = np.zeros((n_pat, N_VARIANTS, ATT_BQ, ATT_KW), np.int32)
    inside = np.zeros((n_pat, N_VARIANTS, ATT_BQ, ATT_KW), bool)
    rows = np.arange(ATT_BQ)[:, None]
    cols = np.arange(ATT_KW)[None, :]
    for p, (_, dil) in enumerate(DILATED_PATTERNS):
        n = s // dil
        kw = min(ATT_KW, n)
        offsets = (0,) if n == ATT_BQ else (0, ATT_RADIUS, ATT_BQ)
        for v, off in enumerate(offsets):
            delta = cols - off - rows
            inside[p, v] = (np.abs(delta) <= ATT_RADIUS) & (cols < kw)
            bucket[p, v] = _t5_bucket_np(np.clip(delta, -ATT_RADIUS, ATT_RADIUS) * dil)
    bucket_j = jnp.asarray(bucket)[:, :, None]
    table = rel_bias.astype(F32)
    bias = jnp.zeros((n_pat, N_VARIANTS, ATT_HEADS, ATT_BQ, ATT_KW), F32)
    for bk in range(NUM_BUCKETS):
        bias = jnp.where(bucket_j == bk, table[bk][None, None, :, None, None], bias)
    return jnp.where(jnp.asarray(inside)[:, :, None], bias, NEG_INF)


def _layer_norm(z, g, b):
    mu = jnp.mean(z, axis=-1, keepdims=True)
    d = z - mu
    var = jnp.mean(d * d, axis=-1, keepdims=True)
    return d * lax.rsqrt(var + LN_EPS) * g + b


def _split_bf16(a):
    hi = a.astype(BF16)
    lo = (a - hi.astype(F32)).astype(BF16)
    return hi, lo


def _merge_kernel(ret_ref, att_ref, br_ref, ba_ref, x_ref,
                  wpr_ref, wpa_ref, wout_ref, lng_ref, lnb_ref, wrh_ref, wrl_ref, rb_ref,
                  x1r_ref):
    y_r = jnp.dot(ret_ref[...], wpr_ref[...], preferred_element_type=F32)
    y_a = jnp.dot(att_ref[...], wpa_ref[...], preferred_element_type=F32)
    merged = (jax.nn.sigmoid(br_ref[...].astype(F32)) * y_r
              + jax.nn.sigmoid(ba_ref[...].astype(F32)) * y_a)
    mix = jnp.dot(merged.astype(BF16), wout_ref[...], preferred_element_type=F32)
    x1 = _layer_norm(DEEPNORM_ALPHA * x_ref[...] + mix, lng_ref[...], lnb_ref[...])
    x1r_ref[:, 0:D_MODEL] = x1

    xh, xl = _split_bf16(x1)
    logits = (jnp.dot(xh, wrh_ref[...], preferred_element_type=F32)
              + jnp.dot(xh, wrl_ref[...], preferred_element_type=F32)
              + jnp.dot(xl, wrh_ref[...], preferred_element_type=F32)) + rb_ref[...]
    tm = logits.shape[0]
    lane = lax.broadcasted_iota(jnp.int32, (tm, LANES), 1)
    lane_f = lane.astype(F32)
    big = float(LANES)
    gmask = lane < N_GROUPS
    gl = jnp.where(gmask, logits, -jnp.inf)
    gmax = jnp.max(gl, axis=-1, keepdims=True)
    g_idx = jnp.min(jnp.where(gl == gmax, lane_f, big), axis=-1, keepdims=True)
    g_w = 1.0 / jnp.sum(jnp.where(gmask, jnp.exp(gl - gmax), 0.0), axis=-1, keepdims=True)
    in_group = jnp.floor((lane_f - N_GROUPS) * (1.0 / EXPERTS_PER_GROUP)) == g_idx
    emask = (lane >= N_GROUPS) & (lane < N_GROUPS + N_EXPERTS) & in_group
    el = jnp.where(emask, logits, -jnp.inf)
    v1 = jnp.max(el, axis=-1, keepdims=True)
    i1 = jnp.min(jnp.where(el == v1, lane_f, big), axis=-1, keepdims=True)
    el2 = jnp.where(lane_f == i1, -jnp.inf, el)
    v2 = jnp.max(el2, axis=-1, keepdims=True)
    i2 = jnp.min(jnp.where(el2 == v2, lane_f, big), axis=-1, keepdims=True)
    t = jnp.exp(v2 - v1)
    p1 = 1.0 / (1.0 + t)
    w1 = g_w * p1
    w2 = g_w * (t * p1)
    first_lo = i1 < i2
    e_lo = jnp.where(first_lo, i1, i2) - N_GROUPS
    e_hi = jnp.where(first_lo, i2, i1) - N_GROUPS
    a_loc = e_lo - EXPERTS_PER_GROUP * g_idx
    b_loc = e_hi - EXPERTS_PER_GROUP * g_idx
    pair = a_loc * (2 * EXPERTS_PER_GROUP - 1 - a_loc) * 0.5 + (b_loc - a_loc - 1.0)
    cls = g_idx * PAIRS_PER_GROUP + pair
    route = jnp.where(lane == ROUTE_W_LO, jnp.where(first_lo, w1, w2),
                      jnp.where(lane == ROUTE_W_HI, jnp.where(first_lo, w2, w1),
                                jnp.where(lane == ROUTE_CLS, cls, 0.0)))
    x1r_ref[:, D_MODEL:X1R_COLS] = route


def _merge(ret_act, attn, hcat, x2d, wpr, wpa, wout, lng, lnb, wrh, wrl, rb):
    t = x2d.shape[0]
    tm = min(MERGE_TM, t)
    row = lambda i: (i, 0)
    const = lambda i: (0, 0)
    in_specs = [
        pl.BlockSpec((tm, D_MODEL), row),
        pl.BlockSpec((tm, ATT_W), row),
        pl.BlockSpec((tm, D_MODEL), lambda i: (i, COL_BR // D_MODEL)),
        pl.BlockSpec((tm, D_MODEL), lambda i: (i, COL_BA // D_MODEL)),
        pl.BlockSpec((tm, D_MODEL), row),
        pl.BlockSpec((D_MODEL, D_MODEL), const),
        pl.BlockSpec((ATT_W, D_MODEL), const),
        pl.BlockSpec((D_MODEL, D_MODEL), const),
        pl.BlockSpec((1, D_MODEL), const),
        pl.BlockSpec((1, D_MODEL), const),
        pl.BlockSpec((D_MODEL, LANES), const),
        pl.BlockSpec((D_MODEL, LANES), const),
        pl.BlockSpec((1, LANES), const)]
    return pl.pallas_call(
        _merge_kernel,
        grid=(t // tm,),
        in_specs=in_specs,
        out_specs=pl.BlockSpec((tm, X1R_COLS), row),
        out_shape=jax.ShapeDtypeStruct((t, X1R_COLS), F32),
        compiler_params=_cparams(("parallel",)),
        name="merge_ln1_router",
    )(ret_act, attn, hcat, hcat, x2d, wpr, wpa, wout, lng, lnb, wrh, wrl, rb)


def _row_copies_start(idx_ref, n_rows, copy_of, sem):
    def start(g, u, r):
        src, dst = copy_of(g, u, idx_ref[0, r])
        pltpu.make_async_copy(src, dst, sem).start()

    def group(g, carry):
        for u in range(SUBLANES):
            start(g, u, g * SUBLANES + u)
        return carry

    n_groups = n_rows // SUBLANES
    lax.fori_loop(0, n_groups, group, 0)

    def tail(r, carry):
        start(n_groups, r - n_groups * SUBLANES, r)
        return carry

    lax.fori_loop(n_groups * SUBLANES, n_rows, tail, 0)


def _rows_wait(buf, slot, n_rows, sem):
    m = buf.shape[1] * SUBLANES
    assert m & (m - 1) == 0
    while m >= 1:
        @pl.when((n_rows & m) != 0)
        def _(m=m):
            if m >= SUBLANES:
                view = buf.at[slot, pl.ds(0, m // SUBLANES)]
            else:
                view = buf.at[slot, 0, pl.ds(0, m)]
            pltpu.make_async_copy(view, view, sem.at[slot]).wait()
        m //= 2


def _swiglu(xb, wg_ref, wu_ref, wd_ref):
    gate = jnp.dot(xb, wg_ref[...], preferred_element_type=F32)
    up = jnp.dot(xb, wu_ref[...], preferred_element_type=F32)
    hid = (gate * jax.nn.sigmoid(gate) * up).astype(BF16)
    return jnp.dot(hid, wd_ref[...], preferred_element_type=F32)


def _moe_kernel(ea_ref, eb_ref, nv_ref, tok_ref, tok_next_ref, x_hbm,
                wga_ref, wua_ref, wda_ref, wgb_ref, wub_ref, wdb_ref, lng_ref, lnb_ref,
                out_hbm, xbuf, obuf, gsem, ssem):
    del ea_ref, eb_ref
    j = pl.program_id(0)
    nb = pl.num_programs(0)
    slot = lax.rem(j, 2)
    other = 1 - slot
    nv = nv_ref[j]

    def gather_start(idx_ref, n_rows, slot_):
        _row_copies_start(idx_ref, n_rows,
                          lambda g, u, row: (x_hbm.at[pl.ds(row, 1), :], xbuf.at[slot_, g, pl.ds(u, 1), :]),
                          gsem.at[slot_])

    @pl.when(j == 0)
    def _():
        xbuf[...] = jnp.zeros_like(xbuf)
        gather_start(tok_ref, nv, 0)

    @pl.when(j + 1 < nb)
    def _():
        gather_start(tok_next_ref, nv_ref[jnp.minimum(j + 1, nb - 1)], other)

    _rows_wait(xbuf, slot, nv, gsem)

    @pl.when(j >= 2)
    def _():
        _rows_wait(obuf, slot, nv_ref[jnp.maximum(j - 2, 0)], ssem)

    @pl.when(nv > 0)
    def _():
        xr = xbuf[slot].reshape(EXPERT_ROWS, X1R_COLS)
        x1 = xr[:, 0:D_MODEL]
        w_a = xr[:, D_MODEL + ROUTE_W_LO:D_MODEL + ROUTE_W_LO + 1]
        w_b = xr[:, D_MODEL + ROUTE_W_HI:D_MODEL + ROUTE_W_HI + 1]
        xb = x1.astype(BF16)
        ffn = w_a * _swiglu(xb, wga_ref, wua_ref, wda_ref) + w_b * _swiglu(xb, wgb_ref, wub_ref, wdb_ref)
        out = _layer_norm(DEEPNORM_ALPHA * x1 + ffn, lng_ref[...], lnb_ref[...])
        obuf[slot] = out.reshape(EXPERT_ROWS // SUBLANES, SUBLANES, D_MODEL)
        _row_copies_start(tok_ref, nv,
                          lambda g, u, row: (obuf.at[slot, g, pl.ds(u, 1), :], out_hbm.at[pl.ds(row, 1), :]),
                          ssem.at[slot])

    @pl.when(j == nb - 1)
    def _():
        @pl.when(nb >= 2)
        def _():
            _rows_wait(obuf, other, nv_ref[jnp.maximum(j - 1, 0)], ssem)
        _rows_wait(obuf, slot, nv, ssem)


def _moe(x1r, slot_tok, block_ea, block_eb, block_nv, wg, wu, wd, lng, lnb):
    t = x1r.shape[0]
    n_blocks = block_ea.shape[0]
    r = EXPERT_ROWS
    tok3 = slot_tok.reshape(n_blocks, 1, r)
    const = lambda j, ea, eb, nv: (0, 0)
    grid_spec = pltpu.PrefetchScalarGridSpec(
        num_scalar_prefetch=3,
        grid=(n_blocks,),
        in_specs=[
            pl.BlockSpec((None, 1, r), lambda j, ea, eb, nv: (j, 0, 0), memory_space=pltpu.SMEM),
            pl.BlockSpec((None, 1, r), lambda j, ea, eb, nv: (jnp.minimum(j + 1, n_blocks - 1), 0, 0),
                         memory_space=pltpu.SMEM),
            pl.BlockSpec(memory_space=pl.ANY),
            pl.BlockSpec((None, D_MODEL, D_EXPERT), lambda j, ea, eb, nv: (ea[j], 0, 0)),
            pl.BlockSpec((None, D_MODEL, D_EXPERT), lambda j, ea, eb, nv: (ea[j], 0, 0)),
            pl.BlockSpec((None, D_EXPERT, D_MODEL), lambda j, ea, eb, nv: (ea[j], 0, 0)),
            pl.BlockSpec((None, D_MODEL, D_EXPERT), lambda j, ea, eb, nv: (eb[j], 0, 0)),
            pl.BlockSpec((None, D_MODEL, D_EXPERT), lambda j, ea, eb, nv: (eb[j], 0, 0)),
            pl.BlockSpec((None, D_EXPERT, D_MODEL), lambda j, ea, eb, nv: (eb[j], 0, 0)),
            pl.BlockSpec((1, D_MODEL), const),
            pl.BlockSpec((1, D_MODEL), const),
        ],
        out_specs=pl.BlockSpec(memory_space=pl.ANY),
        scratch_shapes=[pltpu.VMEM((2, r // SUBLANES, SUBLANES, X1R_COLS), F32),
                        pltpu.VMEM((2, r // SUBLANES, SUBLANES, D_MODEL), F32),
                        pltpu.SemaphoreType.DMA((2,)), pltpu.SemaphoreType.DMA((2,))],
    )
    return pl.pallas_call(
        _moe_kernel,
        grid_spec=grid_spec,
        out_shape=jax.ShapeDtypeStruct((t, D_MODEL), F32),
        compiler_params=_cparams(("arbitrary",)),
        name="experts_combine_ln2",
    )(block_ea, block_eb, block_nv, tok3, tok3, x1r, wg, wu, wd, wg, wu, wd, lng, lnb)


def _pair_tables():
    ea, eb = [], []
    for g in range(N_GROUPS):
        for a in range(EXPERTS_PER_GROUP):
            for b in range(a + 1, EXPERTS_PER_GROUP):
                ea.append(g * EXPERTS_PER_GROUP + a)
                eb.append(g * EXPERTS_PER_GROUP + b)
    return np.asarray(ea, np.int32), np.asarray(eb, np.int32)


def _routing_plan(cls, n_tokens):
    r = EXPERT_ROWS
    assert n_tokens & (n_tokens - 1) == 0
    classes = jnp.arange(N_CLASSES, dtype=jnp.int32)
    counts = jnp.sum((cls[:, None] == classes[None, :]).astype(jnp.int32), axis=0)
    padded = (counts + r - 1) // r * r
    ends = jnp.cumsum(padded)
    pstart = ends - padded
    n_blocks = n_tokens // r + N_CLASSES
    token_keys = cls * (2 * n_tokens) + jnp.arange(n_tokens, dtype=jnp.int32)
    filler_idx = jnp.arange(r, dtype=jnp.int32)[None, :]
    filler_keys = jnp.where(filler_idx < (padded - counts)[:, None],
                            classes[:, None] * (2 * n_tokens) + n_tokens + filler_idx,
                            N_CLASSES * 2 * n_tokens)
    keys = jnp.sort(jnp.concatenate([token_keys, filler_keys.reshape(-1)]))
    slot_tok = jnp.where((keys & n_tokens) != 0, 0, keys & (n_tokens - 1))
    block_start = jnp.arange(n_blocks, dtype=jnp.int32) * r
    block_cls = jnp.minimum(jnp.sum(ends[None, :] <= block_start[:, None], axis=1), N_CLASSES - 1)
    onehot_b = (block_cls[:, None] == classes[None, :]).astype(jnp.int32)
    ea_tab, eb_tab = _pair_tables()
    block_ea = jnp.sum(onehot_b * jnp.asarray(ea_tab)[None, :], axis=1).astype(jnp.int32)
    block_eb = jnp.sum(onehot_b * jnp.asarray(eb_tab)[None, :], axis=1).astype(jnp.int32)
    used = jnp.sum(onehot_b * (pstart + counts)[None, :], axis=1)
    block_nv = jnp.clip(used - block_start, 0, r).astype(jnp.int32)
    return slot_tok, block_ea, block_eb, block_nv


def _rotary_tables(s):
    half = RET_DK // 2
    inv = ROPE_BASE ** (-jnp.arange(half, dtype=F32) / half)
    ang = jnp.arange(s, dtype=F32)[:, None] * inv[None, :]
    cos, sin = jnp.cos(ang), jnp.sin(ang)
    return jnp.concatenate([cos, cos], axis=1), jnp.concatenate([-sin, sin], axis=1)


def _cols(w, off, width):
    return lax.slice_in_dim(w, off, off + width, axis=1)


def _layer(x, w_in, dec_f, dec_b, gn_g, w_proj_ret, rel_bias, w_proj_attn, w_out, ln1_g, ln1_b,
           rw_g, rb_g, rw_e, rb_e, w_gate, w_up, w_down, ln2_g, ln2_b):
    b, s, d = x.shape
    t = b * s
    x2d = x.reshape(t, d)
    for window, dil in DILATED_PATTERNS:
        assert window // (2 * dil) == ATT_RADIUS and (s // dil) % ATT_BQ == 0

    head_cols = [_cols(w_in, off + h * width, width) for h in range(RET_HEADS)
                 for off, width in ((REF_QR, RET_DK), (REF_KR, RET_DK), (REF_VR, RET_DV), (REF_GR, RET_DV))]
    w_main = jnp.concatenate(head_cols + [_cols(w_in, REF_BR, D_MODEL), _cols(w_in, REF_BA, D_MODEL)],
                             axis=1).astype(BF16)
    hcat = _proj_in(x2d, w_main)
    hcat3 = hcat.reshape(b, s, MAIN_COLS)

    dec = jnp.stack([jax.nn.log_sigmoid(dec_f.astype(F32)), jax.nn.log_sigmoid(dec_b.astype(F32))], axis=0)
    cos_t, sin_t = _rotary_tables(s)
    ret_act = _retention(hcat3, dec, cos_t, sin_t, gn_g.reshape(1, -1).astype(F32)).reshape(t, -1)

    w_pairs = jnp.concatenate(
        [_cols(w_in, off + j * ATT_PAIR_W, ATT_PAIR_W) for j in range(N_PAIRS) for off in (REF_QA, REF_KA, REF_VA)],
        axis=1).astype(BF16)
    attn = _attention(x, w_pairs, _attention_bias(rel_bias, s)).reshape(t, ATT_W)

    wr = jnp.zeros((d, LANES), F32).at[:, :N_GROUPS].set(rw_g.astype(F32))
    wr = wr.at[:, N_GROUPS:N_GROUPS + N_EXPERTS].set(rw_e.astype(F32))
    rb = jnp.zeros((1, LANES), F32).at[0, :N_GROUPS].set(rb_g.astype(F32))
    rb = rb.at[0, N_GROUPS:N_GROUPS + N_EXPERTS].set(rb_e.astype(F32))
    wrh = wr.astype(BF16)
    wrl = (wr - wrh.astype(F32)).astype(BF16)
    x1r = _merge(ret_act, attn, hcat, x2d,
                 w_proj_ret.astype(BF16), w_proj_attn.astype(BF16), w_out.astype(BF16),
                 ln1_g.reshape(1, -1).astype(F32), ln1_b.reshape(1, -1).astype(F32), wrh, wrl, rb)

    cls = x1r[:, D_MODEL + ROUTE_CLS].astype(jnp.int32)
    slot_tok, block_ea, block_eb, block_nv = _routing_plan(cls, t)
    out = _moe(x1r, slot_tok, block_ea, block_eb, block_nv,
               w_gate.astype(BF16), w_up.astype(BF16), w_down.astype(BF16),
               ln2_g.reshape(1, -1).astype(F32), ln2_b.reshape(1, -1).astype(F32))
    return out.reshape(b, s, d)


def kernel(x, w_in, ret_decay_fwd, ret_decay_bwd, ret_gn_g, w_proj_ret, rel_bias, w_proj_attn, w_out,
           ln1_g, ln1_b, router_w_group, router_b_group, router_w_expert, router_b_expert,
           w_gate, w_up, w_down, ln2_g, ln2_b):
    for l in range(DEPTH):
        x = _layer(x, w_in[l], ret_decay_fwd[l], ret_decay_bwd[l], ret_gn_g[l], w_proj_ret[l], rel_bias,
                   w_proj_attn[l], w_out[l], ln1_g[l], ln1_b[l], router_w_group[l], router_b_group[l],
                   router_w_expert[l], router_b_expert[l], w_gate[l], w_up[l], w_down[l],
                   ln2_g[l], ln2_b[l])
    return x
```

```python
import functools
import math

import numpy as np
import jax
import jax.numpy as jnp
from jax import lax
from jax.experimental import pallas as pl
from jax.experimental.pallas import tpu as pltpu

F32 = jnp.float32
BF16 = jnp.bfloat16

D_MODEL = 1024
RET_HEADS = 4
RET_DK = 128
RET_DV = 256
RET_CHUNK = 128
ROPE_BASE = 10000.0
ATT_HEADS = 8
ATT_DH = 64
ATT_W = ATT_HEADS * ATT_DH
DILATED_PATTERNS = ((128, 1), (512, 4), (2048, 16))
NUM_BUCKETS = 32
MAX_DISTANCE = 1024
N_GROUPS = 4
EXPERTS_PER_GROUP = 4
N_EXPERTS = N_GROUPS * EXPERTS_PER_GROUP
TOP_K = 2
D_EXPERT = 512
DEPTH = 1
DEEPNORM_ALPHA = (2.0 * DEPTH) ** 0.25
LN_EPS = 1e-5
NEG_INF = -1e30

REF_QR, REF_KR, REF_VR, REF_GR, REF_QA, REF_KA, REF_VA, REF_BR, REF_BA = (
    0, 512, 1024, 2048, 3072, 3584, 4096, 4608, 5632)
RET_HEAD_COLS = 2 * RET_DK + 2 * RET_DV
COL_BR = RET_HEADS * RET_HEAD_COLS
COL_BA = COL_BR + D_MODEL
MAIN_COLS = COL_BA + D_MODEL

V7X_VMEM_LIMIT_BYTES = 56 * 1024 * 1024
LANES = 128
SUBLANES = 8

RET_UNROLL = 16
PROJ_TM = 1024
PROJ_TN = 1280
ATT_BQ = 128
ATT_RADIUS = 64
ATT_KW = 2 * ATT_BQ
ATT_PAIR_W = 2 * ATT_DH
N_PAIRS = ATT_HEADS // 2
N_VARIANTS = 3
ATT_UNROLL = 16
MERGE_TM = 512
EXPERT_ROWS = 256
PAIRS_PER_GROUP = EXPERTS_PER_GROUP * (EXPERTS_PER_GROUP - 1) // 2
N_CLASSES = N_GROUPS * PAIRS_PER_GROUP
X1R_COLS = D_MODEL + LANES
ROUTE_W_LO, ROUTE_W_HI, ROUTE_CLS = 0, 1, 2


def _cparams(sem):
    return pltpu.CompilerParams(dimension_semantics=sem, vmem_limit_bytes=V7X_VMEM_LIMIT_BYTES)


def _proj_in_kernel(x_ref, w_ref, o_ref, xb_ref):
    @pl.when(pl.program_id(1) == 0)
    def _():
        xb_ref[...] = x_ref[...].astype(BF16)

    o_ref[...] = jnp.dot(xb_ref[...], w_ref[...], preferred_element_type=F32).astype(o_ref.dtype)


def _proj_in(x2d, w_bf16):
    t = x2d.shape[0]
    n_cols = w_bf16.shape[1]
    tm = min(PROJ_TM, t)
    return pl.pallas_call(
        _proj_in_kernel,
        grid=(t // tm, n_cols // PROJ_TN),
        in_specs=[pl.BlockSpec((tm, D_MODEL), lambda i, j: (i, 0)),
                  pl.BlockSpec((D_MODEL, PROJ_TN), lambda i, j: (0, j))],
        out_specs=pl.BlockSpec((tm, PROJ_TN), lambda i, j: (i, j)),
        out_shape=jax.ShapeDtypeStruct((t, n_cols), BF16),
        scratch_shapes=[pltpu.VMEM((tm, D_MODEL), BF16)],
        compiler_params=_cparams(("parallel", "arbitrary")),
        name="proj_in",
    )(x2d, w_bf16)


def _retention_kernel(dec_ref, qkvg_ref, cos_ref, sin_ref, gn_ref, o_ref,
                      kr_ref, kvf_ref, kvb_ref, pf_ref, nb_ref, *, n_chunks):
    c_len = RET_CHUNK
    q_cols = slice(0, RET_DK)
    k_cols = slice(RET_DK, 2 * RET_DK)
    v_cols = slice(2 * RET_DK, 2 * RET_DK + RET_DV)
    g_cols = slice(2 * RET_DK + RET_DV, RET_HEAD_COLS)
    h = pl.program_id(1)
    lgf = dec_ref[0, h]
    lgb = dec_ref[1, h]
    row = lax.broadcasted_iota(jnp.int32, (c_len, c_len), 0).astype(F32)
    col = lax.broadcasted_iota(jnp.int32, (c_len, c_len), 1).astype(F32)
    diff = row - col
    dmat = jnp.where(diff > 0, jnp.exp(lgf * jnp.maximum(diff, 0.0)),
                     jnp.where(diff < 0, jnp.exp(lgb * jnp.maximum(-diff, 0.0)), 2.0))
    zeta_f = jnp.exp(lgf * (c_len - 1.0 - row))
    zeta_b = jnp.exp(lgb * row)
    xi_f = jnp.exp(lgf * (row + 1.0))
    xi_b = jnp.exp(lgb * (c_len - row))
    zero_row = jnp.zeros((1, RET_DV), F32)
    decay_f = jnp.exp(zero_row + lgf * c_len)
    decay_b = jnp.exp(zero_row + lgb * c_len)
    k_scale = RET_DK ** -0.5
    tn_dims = (((0,), (0,)), ((), ()))
    nt_dims = (((1,), (1,)), ((), ()))

    def rows_of(c):
        return pl.ds(pl.multiple_of(c * c_len, c_len), c_len)

    def rotary(t, rows):
        return t * cos_ref[rows, :] + pltpu.roll(t, RET_DK // 2, 1) * sin_ref[rows, :]

    def kv_body(c, carry):
        rows = rows_of(c)
        kr = rotary(qkvg_ref[rows, k_cols].astype(F32), rows) * k_scale
        kr_ref[rows, :] = kr.astype(BF16)
        vb = qkvg_ref[rows, v_cols]
        kvf_ref[c] = lax.dot_general((kr * zeta_f).astype(BF16), vb, tn_dims, preferred_element_type=F32)
        kvb_ref[c] = lax.dot_general((kr * zeta_b).astype(BF16), vb, tn_dims, preferred_element_type=F32)
        return carry

    lax.fori_loop(0, n_chunks, kv_body, 0, unroll=RET_UNROLL)

    def scan_f(c, st):
        pf_ref[c] = st.astype(BF16)
        return st * decay_f + kvf_ref[c]

    lax.fori_loop(0, n_chunks, scan_f, jnp.zeros((RET_DK, RET_DV), F32))

    def scan_b(i, st):
        c = n_chunks - 1 - i
        nb_ref[c] = st.astype(BF16)
        return st * decay_b + kvb_ref[c]

    lax.fori_loop(0, n_chunks, scan_b, jnp.zeros((RET_DK, RET_DV), F32))

    def out_body(c, carry):
        rows = rows_of(c)
        qr = rotary(qkvg_ref[rows, q_cols].astype(F32), rows)
        vb = qkvg_ref[rows, v_cols]
        scores = lax.dot_general(qr.astype(BF16), kr_ref[rows, :], nt_dims, preferred_element_type=F32)
        o = jnp.dot((scores * dmat).astype(BF16), vb, preferred_element_type=F32)
        o += jnp.dot((qr * xi_f).astype(BF16), pf_ref[c], preferred_element_type=F32)
        o += jnp.dot((qr * xi_b).astype(BF16), nb_ref[c], preferred_element_type=F32)
        mu = jnp.mean(o, axis=-1, keepdims=True)
        d = o - mu
        var = jnp.mean(d * d, axis=-1, keepdims=True)
        y = d * lax.rsqrt(var + LN_EPS) * gn_ref[...]
        gate = qkvg_ref[rows, g_cols].astype(F32)
        o_ref[rows, :] = (gate * jax.nn.sigmoid(gate) * y).astype(o_ref.dtype)
        return carry

    lax.fori_loop(0, n_chunks, out_body, 0, unroll=RET_UNROLL)


def _retention(hcat3, dec, cos_t, sin_t, gn_g):
    b, s, _ = hcat3.shape
    n_chunks = s // RET_CHUNK
    kern = functools.partial(_retention_kernel, n_chunks=n_chunks)
    return pl.pallas_call(
        kern,
        grid=(b, RET_HEADS),
        in_specs=[
            pl.BlockSpec(memory_space=pltpu.SMEM),
            pl.BlockSpec((None, s, RET_HEAD_COLS), lambda i, h: (i, 0, h)),
            pl.BlockSpec((s, RET_DK), lambda i, h: (0, 0)),
            pl.BlockSpec((s, RET_DK), lambda i, h: (0, 0)),
            pl.BlockSpec((1, RET_DV), lambda i, h: (0, h)),
        ],
        out_specs=pl.BlockSpec((None, s, RET_DV), lambda i, h: (i, 0, h)),
        out_shape=jax.ShapeDtypeStruct((b, s, RET_HEADS * RET_DV), BF16),
        scratch_shapes=[pltpu.VMEM((s, RET_DK), BF16),
                        pltpu.VMEM((n_chunks, RET_DK, RET_DV), F32),
                        pltpu.VMEM((n_chunks, RET_DK, RET_DV), F32),
                        pltpu.VMEM((n_chunks, RET_DK, RET_DV), BF16),
                        pltpu.VMEM((n_chunks, RET_DK, RET_DV), BF16)],
        compiler_params=_cparams(("parallel", "arbitrary")),
        name="retention",
    )(dec, hcat3, cos_t, sin_t, gn_g)


def _attention_kernel(x_ref, w_ref, bias_ref, o_ref,
                      xb_ref, qkv_ref, qkv_gathered_ref, qd_ref, kd_ref, vd_ref, op_ref, lp_ref, s_ref, p_ref,
                      m_ref, *, s):
    nt_dims = (((1,), (1,)), ((), ()))

    @pl.when(pl.program_id(1) == 0)
    def _():
        xb_ref[...] = x_ref[...].astype(BF16)

    vd_ref[:, ATT_PAIR_W:2 * ATT_PAIR_W] = jnp.ones((s, ATT_PAIR_W), BF16)

    proj_rows = 512
    for c in range(s // proj_rows):
        rows = slice(c * proj_rows, (c + 1) * proj_rows)
        acc = jnp.dot(xb_ref[rows, :], w_ref[...], preferred_element_type=F32)
        for t in range(3):
            qkv_ref[t, rows, :] = acc[:, t * ATT_PAIR_W:(t + 1) * ATT_PAIR_W]

    first_head = lax.broadcasted_iota(jnp.int32, (ATT_BQ, ATT_PAIR_W), 1) < ATT_DH

    def pattern(p, dil, prev_dil, src_ref, dst_ref):
        n = s // dil
        nq = n // ATT_BQ
        kw = min(ATT_KW, n)
        n_blocks = s // ATT_BQ

        assert dil % prev_dil == 0
        step = dil // prev_dil
        n_prev = s // prev_dil
        copy_rows = min(n, 512)

        def gather(r, carry):
            r_prev = lax.rem(r, prev_dil)
            j = r // prev_dil
            for c in range(n // copy_rows):
                first = r_prev * n_prev + j + step * c * copy_rows
                src = pl.ds(first, copy_rows) if step == 1 else pl.ds(first, copy_rows, stride=step)
                row0 = pl.multiple_of(r * n + c * copy_rows, copy_rows)
                dst = pl.ds(row0, copy_rows)
                q_raw = src_ref[0, src, :]
                k_raw = src_ref[1, src, :]
                v_raw = src_ref[2, src, :]
                if dst_ref is not None:
                    dst_ref[0, dst, :] = q_raw
                    dst_ref[1, dst, :] = k_raw
                    dst_ref[2, dst, :] = v_raw
                qf = q_raw * (ATT_DH ** -0.5)
                keep = lax.broadcasted_iota(jnp.int32, qf.shape, 1) < ATT_DH
                q_first = jnp.where(keep, qf, 0.0).astype(BF16)
                q_second = jnp.where(keep, 0.0, qf).astype(BF16)
                for k in range(copy_rows // ATT_BQ):
                    blk_rows = slice(k * ATT_BQ, (k + 1) * ATT_BQ)
                    tile0 = pl.multiple_of(2 * (row0 + k * ATT_BQ), 2 * ATT_BQ)
                    qd_ref[pl.ds(tile0, ATT_BQ), :] = q_first[blk_rows]
                    qd_ref[pl.ds(tile0 + ATT_BQ, ATT_BQ), :] = q_second[blk_rows]
                kd_ref[dst, :] = k_raw.astype(BF16)
                vd_ref[dst, 0:ATT_PAIR_W] = v_raw.astype(BF16)
            return carry

        lax.fori_loop(0, dil, gather, 0)

        def place(blk):
            r = blk // nq
            i = blk - r * nq
            q0 = i * ATT_BQ
            if nq == 1:
                kstart, var = 0, 0
            else:
                kstart = jnp.clip(q0 - ATT_RADIUS, 0, n - kw)
                var = jnp.where(i == 0, 0, jnp.where(i == nq - 1, 2, 1))
            k_rows = pl.ds(pl.multiple_of(r * n + kstart, ATT_RADIUS), kw)
            if dil == 1:
                dst = pl.ds(pl.multiple_of(q0, ATT_BQ), ATT_BQ)
            else:
                dst = pl.ds(r + dil * q0, ATT_BQ, stride=dil)
            return k_rows, var, dst

        def tile_rows(t):
            return pl.ds(pl.multiple_of(t * ATT_BQ, ATT_BQ), ATT_BQ)

        def pair_rows(blk):
            return pl.ds(pl.multiple_of(blk * 2 * ATT_BQ, 2 * ATT_BQ), 2 * ATT_BQ)

        def scores(blk, carry):
            k_rows, var, _ = place(blk)
            sc = lax.dot_general(qd_ref[pair_rows(blk), :], kd_ref[k_rows, :], nt_dims,
                                 preferred_element_type=F32)
            for hh in range(2):
                s_ref[tile_rows(2 * blk + hh), 0:kw] = (sc[hh * ATT_BQ:(hh + 1) * ATT_BQ]
                                                        + bias_ref[p, var, hh, :, 0:kw])
            return carry

        lax.fori_loop(0, n_blocks, scores, 0, unroll=ATT_UNROLL)

        def numerators(t, carry):
            rows = tile_rows(t)
            m = jnp.max(s_ref[rows, 0:kw], axis=-1, keepdims=True)
            p_ref[rows, 0:kw] = jnp.exp(s_ref[rows, 0:kw] - m).astype(BF16)
            m_ref[rows, :] = jnp.broadcast_to(m, (ATT_BQ, ATT_PAIR_W))
            return carry

        lax.fori_loop(0, 2 * n_blocks, numerators, 0, unroll=2 * ATT_UNROLL)

        def values(blk, carry):
            k_rows, _, dst = place(blk)
            pv = jnp.dot(p_ref[pair_rows(blk), 0:kw], vd_ref[k_rows, :], preferred_element_type=F32)
            l = pv[:, ATT_PAIR_W:2 * ATT_PAIR_W]
            o = pv[:, 0:ATT_PAIR_W] / l
            lse = m_ref[pair_rows(blk), :] + jnp.log(l)
            op_ref[p, dst, :] = jnp.where(first_head, o[0:ATT_BQ], o[ATT_BQ:2 * ATT_BQ])
            lp_ref[p, dst, :] = jnp.where(first_head, lse[0:ATT_BQ], lse[ATT_BQ:2 * ATT_BQ])
            return carry

        lax.fori_loop(0, n_blocks, values, 0, unroll=ATT_UNROLL)

    prev_dil, src_ref, spare_ref = 1, qkv_ref, qkv_gathered_ref
    for p, (_, dil) in enumerate(DILATED_PATTERNS):
        keep_copy = dil != prev_dil and p + 1 < len(DILATED_PATTERNS)
        pattern(p, dil, prev_dil, src_ref, spare_ref if keep_copy else None)
        if keep_copy:
            prev_dil, src_ref, spare_ref = dil, spare_ref, src_ref

    mix_rows = 256
    for c in range(s // mix_rows):
        rows = slice(c * mix_rows, (c + 1) * mix_rows)
        l0, l1, l2 = lp_ref[0, rows, :], lp_ref[1, rows, :], lp_ref[2, rows, :]
        m = jnp.maximum(jnp.maximum(l0, l1), l2)
        e0, e1, e2 = jnp.exp(l0 - m), jnp.exp(l1 - m), jnp.exp(l2 - m)
        mixed = (e0 * op_ref[0, rows, :] + e1 * op_ref[1, rows, :] + e2 * op_ref[2, rows, :]) / (e0 + e1 + e2)
        o_ref[rows, :] = mixed.astype(o_ref.dtype)


def _attention(x3, w_pairs, bias):
    b, s, d = x3.shape
    kern = functools.partial(_attention_kernel, s=s)
    n_pat = len(DILATED_PATTERNS)
    return pl.pallas_call(
        kern,
        grid=(b, N_PAIRS),
        in_specs=[pl.BlockSpec((None, s, d), lambda i, j: (i, 0, 0)),
                  pl.BlockSpec((d, 3 * ATT_PAIR_W), lambda i, j: (0, j)),
                  pl.BlockSpec((n_pat, N_VARIANTS, 2, ATT_BQ, ATT_KW), lambda i, j: (0, 0, j, 0, 0))],
        out_specs=pl.BlockSpec((None, s, ATT_PAIR_W), lambda i, j: (i, 0, j)),
        out_shape=jax.ShapeDtypeStruct((b, s, ATT_W), BF16),
        scratch_shapes=[pltpu.VMEM((s, d), BF16),
                        pltpu.VMEM((3, s, ATT_PAIR_W), F32),
                        pltpu.VMEM((3, s, ATT_PAIR_W), F32),
                        pltpu.VMEM((2 * s, ATT_PAIR_W), BF16),
                        pltpu.VMEM((s, ATT_PAIR_W), BF16),
                        pltpu.VMEM((s, 2 * ATT_PAIR_W), BF16),
                        pltpu.VMEM((n_pat, s, ATT_PAIR_W), F32),
                        pltpu.VMEM((n_pat, s, ATT_PAIR_W), F32),
                        pltpu.VMEM((2 * s, ATT_KW), F32),
                        pltpu.VMEM((2 * s, ATT_KW), BF16),
                        pltpu.VMEM((2 * s, ATT_PAIR_W), F32)],
        compiler_params=_cparams(("parallel", "arbitrary")),
        name="attention",
    )(x3, w_pairs, bias)


def _t5_bucket_np(rel):
    half = NUM_BUCKETS // 2
    max_exact = half // 2
    side = np.where(rel > 0, half, 0)
    n = np.abs(rel)
    large = max_exact + (np.log(np.maximum(n, 1).astype(np.float32) / max_exact)
                         / math.log(MAX_DISTANCE / max_exact) * (half - max_exact)).astype(np.int32)
    large = np.minimum(large, half - 1)
    return side + np.where(n < max_exact, n, large)


def _attention_bias(rel_bias, s):
    n_pat = len(DILATED_PATTERNS)
    bucket = np.zeros((n_pat, N_VARIANTS, ATT_BQ, ATT_KW), np.int32)
    inside = np.zeros((n_pat, N_VARIANTS, ATT_BQ, ATT_KW), bool)
    rows = np.arange(ATT_BQ)[:, None]
    cols = np.arange(ATT_KW)[None, :]
    for p, (_, dil) in enumerate(DILATED_PATTERNS):
        n = s // dil
        kw = min(ATT_KW, n)
        offsets = (0,) if n == ATT_BQ else (0, ATT_RADIUS, ATT_BQ)
        for v, off in enumerate(offsets):
            delta = cols - off - rows
            inside[p, v] = (np.abs(delta) <= ATT_RADIUS) & (cols < kw)
            bucket[p, v] = _t5_bucket_np(np.clip(delta, -ATT_RADIUS, ATT_RADIUS) * dil)
    bucket_j = jnp.asarray(bucket)[:, :, None]
    table = rel_bias.astype(F32)
    bias = jnp.zeros((n_pat, N_VARIANTS, ATT_HEADS, ATT_BQ, ATT_KW), F32)
    for bk in range(NUM_BUCKETS):
        bias = jnp.where(bucket_j == bk, table[bk][None, None, :, None, None], bias)
    return jnp.where(jnp.asarray(inside)[:, :, None], bias, NEG_INF)


def _layer_norm(z, g, b):
    mu = jnp.mean(z, axis=-1, keepdims=True)
    d = z - mu
    var = jnp.mean(d * d, axis=-1, keepdims=True)
    return d * lax.rsqrt(var + LN_EPS) * g + b


def _split_bf16(a):
    hi = a.astype(BF16)
    lo = (a - hi.astype(F32)).astype(BF16)
    return hi, lo


def _merge_kernel(ret_ref, att_ref, br_ref, ba_ref, x_ref,
                  wpr_ref, wpa_ref, wout_ref, lng_ref, lnb_ref, wrh_ref, wrl_ref, rb_ref,
                  x1r_ref):
    y_r = jnp.dot(ret_ref[...], wpr_ref[...], preferred_element_type=F32)
    y_a = jnp.dot(att_ref[...], wpa_ref[...], preferred_element_type=F32)
    merged = (jax.nn.sigmoid(br_ref[...].astype(F32)) * y_r
              + jax.nn.sigmoid(ba_ref[...].astype(F32)) * y_a)
    mix = jnp.dot(merged.astype(BF16), wout_ref[...], preferred_element_type=F32)
    x1 = _layer_norm(DEEPNORM_ALPHA * x_ref[...] + mix, lng_ref[...], lnb_ref[...])
    x1r_ref[:, 0:D_MODEL] = x1

    xh, xl = _split_bf16(x1)
    logits = (jnp.dot(xh, wrh_ref[...], preferred_element_type=F32)
              + jnp.dot(xh, wrl_ref[...], preferred_element_type=F32)
              + jnp.dot(xl, wrh_ref[...], preferred_element_type=F32)) + rb_ref[...]
    tm = logits.shape[0]
    lane = lax.broadcasted_iota(jnp.int32, (tm, LANES), 1)
    lane_f = lane.astype(F32)
    big = float(LANES)
    gmask = lane < N_GROUPS
    gl = jnp.where(gmask, logits, -jnp.inf)
    gmax = jnp.max(gl, axis=-1, keepdims=True)
    g_idx = jnp.min(jnp.where(gl == gmax, lane_f, big), axis=-1, keepdims=True)
    g_w = 1.0 / jnp.sum(jnp.where(gmask, jnp.exp(gl - gmax), 0.0), axis=-1, keepdims=True)
    in_group = jnp.floor((lane_f - N_GROUPS) * (1.0 / EXPERTS_PER_GROUP)) == g_idx
    emask = (lane >= N_GROUPS) & (lane < N_GROUPS + N_EXPERTS) & in_group
    el = jnp.where(emask, logits, -jnp.inf)
    v1 = jnp.max(el, axis=-1, keepdims=True)
    i1 = jnp.min(jnp.where(el == v1, lane_f, big), axis=-1, keepdims=True)
    el2 = jnp.where(lane_f == i1, -jnp.inf, el)
    v2 = jnp.max(el2, axis=-1, keepdims=True)
    i2 = jnp.min(jnp.where(el2 == v2, lane_f, big), axis=-1, keepdims=True)
    t = jnp.exp(v2 - v1)
    p1 = 1.0 / (1.0 + t)
    w1 = g_w * p1
    w2 = g_w * (t * p1)
    first_lo = i1 < i2
    e_lo = jnp.where(first_lo, i1, i2) - N_GROUPS
    e_hi = jnp.where(first_lo, i2, i1) - N_GROUPS
    a_loc = e_lo - EXPERTS_PER_GROUP * g_idx
    b_loc = e_hi - EXPERTS_PER_GROUP * g_idx
    pair = a_loc * (2 * EXPERTS_PER_GROUP - 1 - a_loc) * 0.5 + (b_loc - a_loc - 1.0)
    cls = g_idx * PAIRS_PER_GROUP + pair
    route = jnp.where(lane == ROUTE_W_LO, jnp.where(first_lo, w1, w2),
                      jnp.where(lane == ROUTE_W_HI, jnp.where(first_lo, w2, w1),
                                jnp.where(lane == ROUTE_CLS, cls, 0.0)))
    x1r_ref[:, D_MODEL:X1R_COLS] = route


def _merge(ret_act, attn, hcat, x2d, wpr, wpa, wout, lng, lnb, wrh, wrl, rb):
    t = x2d.shape[0]
    tm = min(MERGE_TM, t)
    row = lambda i: (i, 0)
    const = lambda i: (0, 0)
    in_specs = [
        pl.BlockSpec((tm, D_MODEL), row),
        pl.BlockSpec((tm, ATT_W), row),
        pl.BlockSpec((tm, D_MODEL), lambda i: (i, COL_BR // D_MODEL)),
        pl.BlockSpec((tm, D_MODEL), lambda i: (i, COL_BA // D_MODEL)),
        pl.BlockSpec((tm, D_MODEL), row),
        pl.BlockSpec((D_MODEL, D_MODEL), const),
        pl.BlockSpec((ATT_W, D_MODEL), const),
        pl.BlockSpec((D_MODEL, D_MODEL), const),
        pl.BlockSpec((1, D_MODEL), const),
        pl.BlockSpec((1, D_MODEL), const),
        pl.BlockSpec((D_MODEL, LANES), const),
        pl.BlockSpec((D_MODEL, LANES), const),
        pl.BlockSpec((1, LANES), const)]
    return pl.pallas_call(
        _merge_kernel,
        grid=(t // tm,),
        in_specs=in_specs,
        out_specs=pl.BlockSpec((tm, X1R_COLS), row),
        out_shape=jax.ShapeDtypeStruct((t, X1R_COLS), F32),
        compiler_params=_cparams(("parallel",)),
        name="merge_ln1_router",
    )(ret_act, attn, hcat, hcat, x2d, wpr, wpa, wout, lng, lnb, wrh, wrl, rb)


def _row_copies_start(idx_ref, n_rows, copy_of, sem):
    def start(g, u, r):
        src, dst = copy_of(g, u, idx_ref[0, r])
        pltpu.make_async_copy(src, dst, sem).start()

    def group(g, carry):
        for u in range(SUBLANES):
            start(g, u, g * SUBLANES + u)
        return carry

    n_groups = n_rows // SUBLANES
    lax.fori_loop(0, n_groups, group, 0)

    def tail(r, carry):
        start(n_groups, r - n_groups * SUBLANES, r)
        return carry

    lax.fori_loop(n_groups * SUBLANES, n_rows, tail, 0)


def _rows_wait(buf, slot, n_rows, sem):
    m = buf.shape[1] * SUBLANES
    assert m & (m - 1) == 0
    while m >= 1:
        @pl.when((n_rows & m) != 0)
        def _(m=m):
            if m >= SUBLANES:
                view = buf.at[slot, pl.ds(0, m // SUBLANES)]
            else:
                view = buf.at[slot, 0, pl.ds(0, m)]
            pltpu.make_async_copy(view, view, sem.at[slot]).wait()
        m //= 2


def _swiglu(xb, wg_ref, wu_ref, wd_ref):
    gate = jnp.dot(xb, wg_ref[...], preferred_element_type=F32)
    up = jnp.dot(xb, wu_ref[...], preferred_element_type=F32)
    hid = (gate * jax.nn.sigmoid(gate) * up).astype(BF16)
    return jnp.dot(hid, wd_ref[...], preferred_element_type=F32)


def _moe_kernel(ea_ref, eb_ref, nv_ref, tok_ref, tok_next_ref, x_hbm,
                wga_ref, wua_ref, wda_ref, wgb_ref, wub_ref, wdb_ref, lng_ref, lnb_ref,
                out_hbm, xbuf, obuf, gsem, ssem):
    del ea_ref, eb_ref
    j = pl.program_id(0)
    nb = pl.num_programs(0)
    slot = lax.rem(j, 2)
    other = 1 - slot
    nv = nv_ref[j]

    def gather_start(idx_ref, n_rows, slot_):
        _row_copies_start(idx_ref, n_rows,
                          lambda g, u, row: (x_hbm.at[pl.ds(row, 1), :], xbuf.at[slot_, g, pl.ds(u, 1), :]),
                          gsem.at[slot_])

    @pl.when(j == 0)
    def _():
        xbuf[...] = jnp.zeros_like(xbuf)
        gather_start(tok_ref, nv, 0)

    @pl.when(j + 1 < nb)
    def _():
        gather_start(tok_next_ref, nv_ref[jnp.minimum(j + 1, nb - 1)], other)

    _rows_wait(xbuf, slot, nv, gsem)

    @pl.when(j >= 2)
    def _():
        _rows_wait(obuf, slot, nv_ref[jnp.maximum(j - 2, 0)], ssem)

    @pl.when(nv > 0)
    def _():
        xr = xbuf[slot].reshape(EXPERT_ROWS, X1R_COLS)
        x1 = xr[:, 0:D_MODEL]
        w_a = xr[:, D_MODEL + ROUTE_W_LO:D_MODEL + ROUTE_W_LO + 1]
        w_b = xr[:, D_MODEL + ROUTE_W_HI:D_MODEL + ROUTE_W_HI + 1]
        xb = x1.astype(BF16)
        ffn = w_a * _swiglu(xb, wga_ref, wua_ref, wda_ref) + w_b * _swiglu(xb, wgb_ref, wub_ref, wdb_ref)
        out = _layer_norm(DEEPNORM_ALPHA * x1 + ffn, lng_ref[...], lnb_ref[...])
        obuf[slot] = out.reshape(EXPERT_ROWS // SUBLANES, SUBLANES, D_MODEL)
        _row_copies_start(tok_ref, nv,
                          lambda g, u, row: (obuf.at[slot, g, pl.ds(u, 1), :], out_hbm.at[pl.ds(row, 1), :]),
                          ssem.at[slot])

    @pl.when(j == nb - 1)
    def _():
        @pl.when(nb >= 2)
        def _():
            _rows_wait(obuf, other, nv_ref[jnp.maximum(j - 1, 0)], ssem)
        _rows_wait(obuf, slot, nv, ssem)


def _moe(x1r, slot_tok, block_ea, block_eb, block_nv, wg, wu, wd, lng, lnb):
    t = x1r.shape[0]
    n_blocks = block_ea.shape[0]
    r = EXPERT_ROWS
    tok3 = slot_tok.reshape(n_blocks, 1, r)
    const = lambda j, ea, eb, nv: (0, 0)
    grid_spec = pltpu.PrefetchScalarGridSpec(
        num_scalar_prefetch=3,
        grid=(n_blocks,),
        in_specs=[
            pl.BlockSpec((None, 1, r), lambda j, ea, eb, nv: (j, 0, 0), memory_space=pltpu.SMEM),
            pl.BlockSpec((None, 1, r), lambda j, ea, eb, nv: (jnp.minimum(j + 1, n_blocks - 1), 0, 0),
                         memory_space=pltpu.SMEM),
            pl.BlockSpec(memory_space=pl.ANY),
            pl.BlockSpec((None, D_MODEL, D_EXPERT), lambda j, ea, eb, nv: (ea[j], 0, 0)),
            pl.BlockSpec((None, D_MODEL, D_EXPERT), lambda j, ea, eb, nv: (ea[j], 0, 0)),
            pl.BlockSpec((None, D_EXPERT, D_MODEL), lambda j, ea, eb, nv: (ea[j], 0, 0)),
            pl.BlockSpec((None, D_MODEL, D_EXPERT), lambda j, ea, eb, nv: (eb[j], 0, 0)),
            pl.BlockSpec((None, D_MODEL, D_EXPERT), lambda j, ea, eb, nv: (eb[j], 0, 0)),
            pl.BlockSpec((None, D_EXPERT, D_MODEL), lambda j, ea, eb, nv: (eb[j], 0, 0)),
            pl.BlockSpec((1, D_MODEL), const),
            pl.BlockSpec((1, D_MODEL), const),
        ],
        out_specs=pl.BlockSpec(memory_space=pl.ANY),
        scratch_shapes=[pltpu.VMEM((2, r // SUBLANES, SUBLANES, X1R_COLS), F32),
                        pltpu.VMEM((2, r // SUBLANES, SUBLANES, D_MODEL), F32),
                        pltpu.SemaphoreType.DMA((2,)), pltpu.SemaphoreType.DMA((2,))],
    )
    return pl.pallas_call(
        _moe_kernel,
        grid_spec=grid_spec,
        out_shape=jax.ShapeDtypeStruct((t, D_MODEL), F32),
        compiler_params=_cparams(("arbitrary",)),
        name="experts_combine_ln2",
    )(block_ea, block_eb, block_nv, tok3, tok3, x1r, wg, wu, wd, wg, wu, wd, lng, lnb)


def _pair_tables():
    ea, eb = [], []
    for g in range(N_GROUPS):
        for a in range(EXPERTS_PER_GROUP):
            for b in range(a + 1, EXPERTS_PER_GROUP):
                ea.append(g * EXPERTS_PER_GROUP + a)
                eb.append(g * EXPERTS_PER_GROUP + b)
    return np.asarray(ea, np.int32), np.asarray(eb, np.int32)


def _routing_plan(cls, n_tokens):
    r = EXPERT_ROWS
    assert n_tokens & (n_tokens - 1) == 0
    classes = jnp.arange(N_CLASSES, dtype=jnp.int32)
    counts = jnp.sum((cls[:, None] == classes[None, :]).astype(jnp.int32), axis=0)
    padded = (counts + r - 1) // r * r
    ends = jnp.cumsum(padded)
    pstart = ends - padded
    n_blocks = n_tokens // r + N_CLASSES
    token_keys = cls * (2 * n_tokens) + jnp.arange(n_tokens, dtype=jnp.int32)
    filler_idx = jnp.arange(r, dtype=jnp.int32)[None, :]
    filler_keys = jnp.where(filler_idx < (padded - counts)[:, None],
                            classes[:, None] * (2 * n_tokens) + n_tokens + filler_idx,
                            N_CLASSES * 2 * n_tokens)
    keys = jnp.sort(jnp.concatenate([token_keys, filler_keys.reshape(-1)]))
    slot_tok = jnp.where((keys & n_tokens) != 0, 0, keys & (n_tokens - 1))
    block_start = jnp.arange(n_blocks, dtype=jnp.int32) * r
    block_cls = jnp.minimum(jnp.sum(ends[None, :] <= block_start[:, None], axis=1), N_CLASSES - 1)
    onehot_b = (block_cls[:, None] == classes[None, :]).astype(jnp.int32)
    ea_tab, eb_tab = _pair_tables()
    block_ea = jnp.sum(onehot_b * jnp.asarray(ea_tab)[None, :], axis=1).astype(jnp.int32)
    block_eb = jnp.sum(onehot_b * jnp.asarray(eb_tab)[None, :], axis=1).astype(jnp.int32)
    used = jnp.sum(onehot_b * (pstart + counts)[None, :], axis=1)
    block_nv = jnp.clip(used - block_start, 0, r).astype(jnp.int32)
    return slot_tok, block_ea, block_eb, block_nv


def _rotary_tables(s):
    half = RET_DK // 2
    inv = ROPE_BASE ** (-jnp.arange(half, dtype=F32) / half)
    ang = jnp.arange(s, dtype=F32)[:, None] * inv[None, :]
    cos, sin = jnp.cos(ang), jnp.sin(ang)
    return jnp.concatenate([cos, cos], axis=1), jnp.concatenate([-sin, sin], axis=1)


def _cols(w, off, width):
    return lax.slice_in_dim(w, off, off + width, axis=1)


def _layer(x, w_in, dec_f, dec_b, gn_g, w_proj_ret, rel_bias, w_proj_attn, w_out, ln1_g, ln1_b,
           rw_g, rb_g, rw_e, rb_e, w_gate, w_up, w_down, ln2_g, ln2_b):
    b, s, d = x.shape
    t = b * s
    x2d = x.reshape(t, d)
    for window, dil in DILATED_PATTERNS:
        assert window // (2 * dil) == ATT_RADIUS and (s // dil) % ATT_BQ == 0

    head_cols = [_cols(w_in, off + h * width, width) for h in range(RET_HEADS)
                 for off, width in ((REF_QR, RET_DK), (REF_KR, RET_DK), (REF_VR, RET_DV), (REF_GR, RET_DV))]
    w_main = jnp.concatenate(head_cols + [_cols(w_in, REF_BR, D_MODEL), _cols(w_in, REF_BA, D_MODEL)],
                             axis=1).astype(BF16)
    hcat = _proj_in(x2d, w_main)
    hcat3 = hcat.reshape(b, s, MAIN_COLS)

    dec = jnp.stack([jax.nn.log_sigmoid(dec_f.astype(F32)), jax.nn.log_sigmoid(dec_b.astype(F32))], axis=0)
    cos_t, sin_t = _rotary_tables(s)
    ret_act = _retention(hcat3, dec, cos_t, sin_t, gn_g.reshape(1, -1).astype(F32)).reshape(t, -1)

    w_pairs = jnp.concatenate(
        [_cols(w_in, off + j * ATT_PAIR_W, ATT_PAIR_W) for j in range(N_PAIRS) for off in (REF_QA, REF_KA, REF_VA)],
        axis=1).astype(BF16)
    attn = _attention(x, w_pairs, _attention_bias(rel_bias, s)).reshape(t, ATT_W)

    wr = jnp.zeros((d, LANES), F32).at[:, :N_GROUPS].set(rw_g.astype(F32))
    wr = wr.at[:, N_GROUPS:N_GROUPS + N_EXPERTS].set(rw_e.astype(F32))
    rb = jnp.zeros((1, LANES), F32).at[0, :N_GROUPS].set(rb_g.astype(F32))
    rb = rb.at[0, N_GROUPS:N_GROUPS + N_EXPERTS].set(rb_e.astype(F32))
    wrh = wr.astype(BF16)
    wrl = (wr - wrh.astype(F32)).astype(BF16)
    x1r = _merge(ret_act, attn, hcat, x2d,
                 w_proj_ret.astype(BF16), w_proj_attn.astype(BF16), w_out.astype(BF16),
                 ln1_g.reshape(1, -1).astype(F32), ln1_b.reshape(1, -1).astype(F32), wrh, wrl, rb)

    cls = x1r[:, D_MODEL + ROUTE_CLS].astype(jnp.int32)
    slot_tok, block_ea, block_eb, block_nv = _routing_plan(cls, t)
    out = _moe(x1r, slot_tok, block_ea, block_eb, block_nv,
               w_gate.astype(BF16), w_up.astype(BF16), w_down.astype(BF16),
               ln2_g.reshape(1, -1).astype(F32), ln2_b.reshape(1, -1).astype(F32))
    return out.reshape(b, s, d)


def kernel(x, w_in, ret_decay_fwd, ret_decay_bwd, ret_gn_g, w_proj_ret, rel_bias, w_proj_attn, w_out,
           ln1_g, ln1_b, router_w_group, router_b_group, router_w_expert, router_b_expert,
           w_gate, w_up, w_down, ln2_g, ln2_b):
    for l in range(DEPTH):
        x = _layer(x, w_in[l], ret_decay_fwd[l], ret_decay_bwd[l], ret_gn_g[l], w_proj_ret[l], rel_bias,
                   w_proj_attn[l], w_out[l], ln1_g[l], ln1_b[l], router_w_group[l], router_b_group[l],
                   router_w_expert[l], router_b_expert[l], w_gate[l], w_up[l], w_down[l],
                   ln2_g[l], ln2_b[l])
    return x
```

```python
import functools
import math

import numpy as np
import jax
import jax.numpy as jnp
from jax import lax
from jax.experimental import pallas as pl
from jax.experimental.pallas import tpu as pltpu

F32 = jnp.float32
BF16 = jnp.bfloat16

D_MODEL = 1024
RET_HEADS = 4
RET_DK = 128
RET_DV = 256
RET_CHUNK = 128
ROPE_BASE = 10000.0
ATT_HEADS = 8
ATT_DH = 64
ATT_W = ATT_HEADS * ATT_DH
DILATED_PATTERNS = ((128, 1), (512, 4), (2048, 16))
NUM_BUCKETS = 32
MAX_DISTANCE = 1024
N_GROUPS = 4
EXPERTS_PER_GROUP = 4
N_EXPERTS = N_GROUPS * EXPERTS_PER_GROUP
TOP_K = 2
D_EXPERT = 512
DEPTH = 1
DEEPNORM_ALPHA = (2.0 * DEPTH) ** 0.25
LN_EPS = 1e-5
NEG_INF = -1e30

REF_QR, REF_KR, REF_VR, REF_GR, REF_QA, REF_KA, REF_VA, REF_BR, REF_BA = (
    0, 512, 1024, 2048, 3072, 3584, 4096, 4608, 5632)
RET_HEAD_COLS = 2 * RET_DK + 2 * RET_DV
COL_BR = RET_HEADS * RET_HEAD_COLS
COL_BA = COL_BR + D_MODEL
MAIN_COLS = COL_BA + D_MODEL

V7X_VMEM_LIMIT_BYTES = 56 * 1024 * 1024
LANES = 128
SUBLANES = 8

RET_UNROLL = 16
PROJ_TM = 1024
PROJ_TN = 1280
ATT_BQ = 128
ATT_RADIUS = 64
ATT_KW = 2 * ATT_BQ
ATT_PAIR_W = 2 * ATT_DH
N_PAIRS = ATT_HEADS // 2
N_VARIANTS = 3
ATT_UNROLL = 16
MERGE_TM = 512
EXPERT_ROWS = 256
PAIRS_PER_GROUP = EXPERTS_PER_GROUP * (EXPERTS_PER_GROUP - 1) // 2
N_CLASSES = N_GROUPS * PAIRS_PER_GROUP
X1R_COLS = D_MODEL + LANES
ROUTE_W_LO, ROUTE_W_HI, ROUTE_CLS = 0, 1, 2


def _cparams(sem):
    return pltpu.CompilerParams(dimension_semantics=sem, vmem_limit_bytes=V7X_VMEM_LIMIT_BYTES)


def _proj_in_kernel(x_ref, w_ref, o_ref, xb_ref):
    @pl.when(pl.program_id(1) == 0)
    def _():
        xb_ref[...] = x_ref[...].astype(BF16)

    o_ref[...] = jnp.dot(xb_ref[...], w_ref[...], preferred_element_type=F32).astype(o_ref.dtype)


def _proj_in(x2d, w_bf16):
    t = x2d.shape[0]
    n_cols = w_bf16.shape[1]
    tm = min(PROJ_TM, t)
    return pl.pallas_call(
        _proj_in_kernel,
        grid=(t // tm, n_cols // PROJ_TN),
        in_specs=[pl.BlockSpec((tm, D_MODEL), lambda i, j: (i, 0)),
                  pl.BlockSpec((D_MODEL, PROJ_TN), lambda i, j: (0, j))],
        out_specs=pl.BlockSpec((tm, PROJ_TN), lambda i, j: (i, j)),
        out_shape=jax.ShapeDtypeStruct((t, n_cols), BF16),
        scratch_shapes=[pltpu.VMEM((tm, D_MODEL), BF16)],
        compiler_params=_cparams(("parallel", "arbitrary")),
        name="proj_in",
    )(x2d, w_bf16)


def _retention_kernel(dec_ref, qkvg_ref, cos_ref, sin_ref, gn_ref, o_ref,
                      kr_ref, kvf_ref, kvb_ref, pf_ref, nb_ref, *, n_chunks):
    c_len = RET_CHUNK
    q_cols = slice(0, RET_DK)
    k_cols = slice(RET_DK, 2 * RET_DK)
    v_cols = slice(2 * RET_DK, 2 * RET_DK + RET_DV)
    g_cols = slice(2 * RET_DK + RET_DV, RET_HEAD_COLS)
    h = pl.program_id(1)
    lgf = dec_ref[0, h]
    lgb = dec_ref[1, h]
    row = lax.broadcasted_iota(jnp.int32, (c_len, c_len), 0).astype(F32)
    col = lax.broadcasted_iota(jnp.int32, (c_len, c_len), 1).astype(F32)
    diff = row - col
    dmat = jnp.where(diff > 0, jnp.exp(lgf * jnp.maximum(diff, 0.0)),
                     jnp.where(diff < 0, jnp.exp(lgb * jnp.maximum(-diff, 0.0)), 2.0))
    zeta_f = jnp.exp(lgf * (c_len - 1.0 - row))
    zeta_b = jnp.exp(lgb * row)
    xi_f = jnp.exp(lgf * (row + 1.0))
    xi_b = jnp.exp(lgb * (c_len - row))
    zero_row = jnp.zeros((1, RET_DV), F32)
    decay_f = jnp.exp(zero_row + lgf * c_len)
    decay_b = jnp.exp(zero_row + lgb * c_len)
    k_scale = RET_DK ** -0.5
    tn_dims = (((0,), (0,)), ((), ()))
    nt_dims = (((1,), (1,)), ((), ()))

    def rows_of(c):
        return pl.ds(pl.multiple_of(c * c_len, c_len), c_len)

    def rotary(t, rows):
        return t * cos_ref[rows, :] + pltpu.roll(t, RET_DK // 2, 1) * sin_ref[rows, :]

    def kv_body(c, carry):
        rows = rows_of(c)
        kr = rotary(qkvg_ref[rows, k_cols].astype(F32), rows) * k_scale
        kr_ref[rows, :] = kr.astype(BF16)
        vb = qkvg_ref[rows, v_cols]
        kvf_ref[c] = lax.dot_general((kr * zeta_f).astype(BF16), vb, tn_dims, preferred_element_type=F32)
        kvb_ref[c] = lax.dot_general((kr * zeta_b).astype(BF16), vb, tn_dims, preferred_element_type=F32)
        return carry

    lax.fori_loop(0, n_chunks, kv_body, 0, unroll=RET_UNROLL)

    def scan_f(c, st):
        pf_ref[c] = st.astype(BF16)
        return st * decay_f + kvf_ref[c]

    lax.fori_loop(0, n_chunks, scan_f, jnp.zeros((RET_DK, RET_DV), F32))

    def scan_b(i, st):
        c = n_chunks - 1 - i
        nb_ref[c] = st.astype(BF16)
        return st * decay_b + kvb_ref[c]

    lax.fori_loop(0, n_chunks, scan_b, jnp.zeros((RET_DK, RET_DV), F32))

    def out_body(c, carry):
        rows = rows_of(c)
        qr = rotary(qkvg_ref[rows, q_cols].astype(F32), rows)
        vb = qkvg_ref[rows, v_cols]
        scores = lax.dot_general(qr.astype(BF16), kr_ref[rows, :], nt_dims, preferred_element_type=F32)
        o = jnp.dot((scores * dmat).astype(BF16), vb, preferred_element_type=F32)
        o += jnp.dot((qr * xi_f).astype(BF16), pf_ref[c], preferred_element_type=F32)
        o += jnp.dot((qr * xi_b).astype(BF16), nb_ref[c], preferred_element_type=F32)
        mu = jnp.mean(o, axis=-1, keepdims=True)
        d = o - mu
        var = jnp.mean(d * d, axis=-1, keepdims=True)
        y = d * lax.rsqrt(var + LN_EPS) * gn_ref[...]
        gate = qkvg_ref[rows, g_cols].astype(F32)
        o_ref[rows, :] = (gate * jax.nn.sigmoid(gate) * y).astype(o_ref.dtype)
        return carry

    lax.fori_loop(0, n_chunks, out_body, 0, unroll=RET_UNROLL)


def _retention(hcat3, dec, cos_t, sin_t, gn_g):
    b, s, _ = hcat3.shape
    n_chunks = s // RET_CHUNK
    kern = functools.partial(_retention_kernel, n_chunks=n_chunks)
    return pl.pallas_call(
        kern,
        grid=(b, RET_HEADS),
        in_specs=[
            pl.BlockSpec(memory_space=pltpu.SMEM),
            pl.BlockSpec((None, s, RET_HEAD_COLS), lambda i, h: (i, 0, h)),
            pl.BlockSpec((s, RET_DK), lambda i, h: (0, 0)),
            pl.BlockSpec((s, RET_DK), lambda i, h: (0, 0)),
            pl.BlockSpec((1, RET_DV), lambda i, h: (0, h)),
        ],
        out_specs=pl.BlockSpec((None, s, RET_DV), lambda i, h: (i, 0, h)),
        out_shape=jax.ShapeDtypeStruct((b, s, RET_HEADS * RET_DV), BF16),
        scratch_shapes=[pltpu.VMEM((s, RET_DK), BF16),
                        pltpu.VMEM((n_chunks, RET_DK, RET_DV), F32),
                        pltpu.VMEM((n_chunks, RET_DK, RET_DV), F32),
                        pltpu.VMEM((n_chunks, RET_DK, RET_DV), BF16),
                        pltpu.VMEM((n_chunks, RET_DK, RET_DV), BF16)],
        compiler_params=_cparams(("parallel", "arbitrary")),
        name="retention",
    )(dec, hcat3, cos_t, sin_t, gn_g)


def _attention_kernel(x_ref, w_ref, bias_ref, o_ref,
                      xb_ref, qkv_ref, qkv_gathered_ref, qd_ref, kd_ref, vd_ref, op_ref, lp_ref, s_ref, p_ref,
                      m_ref, *, s):
    nt_dims = (((1,), (1,)), ((), ()))

    @pl.when(pl.program_id(1) == 0)
    def _():
        xb_ref[...] = x_ref[...].astype(BF16)

    vd_ref[:, ATT_PAIR_W:2 * ATT_PAIR_W] = jnp.ones((s, ATT_PAIR_W), BF16)

    proj_rows = 512
    for c in range(s // proj_rows):
        rows = slice(c * proj_rows, (c + 1) * proj_rows)
        acc = jnp.dot(xb_ref[rows, :], w_ref[...], preferred_element_type=F32)
        for t in range(3):
            qkv_ref[t, rows, :] = acc[:, t * ATT_PAIR_W:(t + 1) * ATT_PAIR_W]

    first_head = lax.broadcasted_iota(jnp.int32, (ATT_BQ, ATT_PAIR_W), 1) < ATT_DH

    def pattern(p, dil, prev_dil, src_ref, dst_ref):
        n = s // dil
        nq = n // ATT_BQ
        kw = min(ATT_KW, n)
        n_blocks = s // ATT_BQ

        assert dil % prev_dil == 0
        step = dil // prev_dil
        n_prev = s // prev_dil
        copy_rows = min(n, 512)

        def gather(r, carry):
            r_prev = lax.rem(r, prev_dil)
            j = r // prev_dil
            for c in range(n // copy_rows):
                first = r_prev * n_prev + j + step * c * copy_rows
                src = pl.ds(first, copy_rows) if step == 1 else pl.ds(first, copy_rows, stride=step)
                row0 = pl.multiple_of(r * n + c * copy_rows, copy_rows)
                dst = pl.ds(row0, copy_rows)
                q_raw = src_ref[0, src, :]
                k_raw = src_ref[1, src, :]
                v_raw = src_ref[2, src, :]
                if dst_ref is not None:
                    dst_ref[0, dst, :] = q_raw
                    dst_ref[1, dst, :] = k_raw
                    dst_ref[2, dst, :] = v_raw
                qf = q_raw * (ATT_DH ** -0.5)
                keep = lax.broadcasted_iota(jnp.int32, qf.shape, 1) < ATT_DH
                q_first = jnp.where(keep, qf, 0.0).astype(BF16)
                q_second = jnp.where(keep, 0.0, qf).astype(BF16)
                for k in range(copy_rows // ATT_BQ):
                    blk_rows = slice(k * ATT_BQ, (k + 1) * ATT_BQ)
                    tile0 = pl.multiple_of(2 * (row0 + k * ATT_BQ), 2 * ATT_BQ)
                    qd_ref[pl.ds(tile0, ATT_BQ), :] = q_first[blk_rows]
                    qd_ref[pl.ds(tile0 + ATT_BQ, ATT_BQ), :] = q_second[blk_rows]
                kd_ref[dst, :] = k_raw.astype(BF16)
                vd_ref[dst, 0:ATT_PAIR_W] = v_raw.astype(BF16)
            return carry

        lax.fori_loop(0, dil, gather, 0)

        def place(blk):
            r = blk // nq
            i = blk - r * nq
            q0 = i * ATT_BQ
            if nq == 1:
                kstart, var = 0, 0
            else:
                kstart = jnp.clip(q0 - ATT_RADIUS, 0, n - kw)
                var = jnp.where(i == 0, 0, jnp.where(i == nq - 1, 2, 1))
            k_rows = pl.ds(pl.multiple_of(r * n + kstart, ATT_RADIUS), kw)
            if dil == 1:
                dst = pl.ds(pl.multiple_of(q0, ATT_BQ), ATT_BQ)
            else:
                dst = pl.ds(r + dil * q0, ATT_BQ, stride=dil)
            return k_rows, var, dst

        def tile_rows(t):
            return pl.ds(pl.multiple_of(t * ATT_BQ, ATT_BQ), ATT_BQ)

        def pair_rows(blk):
            return pl.ds(pl.multiple_of(blk * 2 * ATT_BQ, 2 * ATT_BQ), 2 * ATT_BQ)

        def scores(blk, carry):
            k_rows, var, _ = place(blk)
            sc = lax.dot_general(qd_ref[pair_rows(blk), :], kd_ref[k_rows, :], nt_dims,
                                 preferred_element_type=F32)
            for hh in range(2):
                s_ref[tile_rows(2 * blk + hh), 0:kw] = (sc[hh * ATT_BQ:(hh + 1) * ATT_BQ]
                                                        + bias_ref[p, var, hh, :, 0:kw])
            return carry

        lax.fori_loop(0, n_blocks, scores, 0, unroll=ATT_UNROLL)

        def numerators(t, carry):
            rows = tile_rows(t)
            m = jnp.max(s_ref[rows, 0:kw], axis=-1, keepdims=True)
            p_ref[rows, 0:kw] = jnp.exp(s_ref[rows, 0:kw] - m).astype(BF16)
            m_ref[rows, :] = jnp.broadcast_to(m, (ATT_BQ, ATT_PAIR_W))
            return carry

        lax.fori_loop(0, 2 * n_blocks, numerators, 0, unroll=2 * ATT_UNROLL)

        def values(blk, carry):
            k_rows, _, dst = place(blk)
            pv = jnp.dot(p_ref[pair_rows(blk), 0:kw], vd_ref[k_rows, :], preferred_element_type=F32)
            l = pv[:, ATT_PAIR_W:2 * ATT_PAIR_W]
            o = pv[:, 0:ATT_PAIR_W] / l
            lse = m_ref[pair_rows(blk), :] + jnp.log(l)
            op_ref[p, dst, :] = jnp.where(first_head, o[0:ATT_BQ], o[ATT_BQ:2 * ATT_BQ])
            lp_ref[p, dst, :] = jnp.where(first_head, lse[0:ATT_BQ], lse[ATT_BQ:2 * ATT_BQ])
            return carry

        lax.fori_loop(0, n_blocks, values, 0, unroll=ATT_UNROLL)

    prev_dil, src_ref, spare_ref = 1, qkv_ref, qkv_gathered_ref
    for p, (_, dil) in enumerate(DILATED_PATTERNS):
        keep_copy = dil != prev_dil and p + 1 < len(DILATED_PATTERNS)
        pattern(p, dil, prev_dil, src_ref, spare_ref if keep_copy else None)
        if keep_copy:
            prev_dil, src_ref, spare_ref = dil, spare_ref, src_ref

    mix_rows = 256
    for c in range(s // mix_rows):
        rows = slice(c * mix_rows, (c + 1) * mix_rows)
        l0, l1, l2 = lp_ref[0, rows, :], lp_ref[1, rows, :], lp_ref[2, rows, :]
        m = jnp.maximum(jnp.maximum(l0, l1), l2)
        e0, e1, e2 = jnp.exp(l0 - m), jnp.exp(l1 - m), jnp.exp(l2 - m)
        mixed = (e0 * op_ref[0, rows, :] + e1 * op_ref[1, rows, :] + e2 * op_ref[2, rows, :]) / (e0 + e1 + e2)
        o_ref[rows, :] = mixed.astype(o_ref.dtype)


def _attention(x3, w_pairs, bias):
    b, s, d = x3.shape
    kern = functools.partial(_attention_kernel, s=s)
    n_pat = len(DILATED_PATTERNS)
    return pl.pallas_call(
        kern,
        grid=(b, N_PAIRS),
        in_specs=[pl.BlockSpec((None, s, d), lambda i, j: (i, 0, 0)),
                  pl.BlockSpec((d, 3 * ATT_PAIR_W), lambda i, j: (0, j)),
                  pl.BlockSpec((n_pat, N_VARIANTS, 2, ATT_BQ, ATT_KW), lambda i, j: (0, 0, j, 0, 0))],
        out_specs=pl.BlockSpec((None, s, ATT_PAIR_W), lambda i, j: (i, 0, j)),
        out_shape=jax.ShapeDtypeStruct((b, s, ATT_W), BF16),
        scratch_shapes=[pltpu.VMEM((s, d), BF16),
                        pltpu.VMEM((3, s, ATT_PAIR_W), F32),
                        pltpu.VMEM((3, s, ATT_PAIR_W), F32),
                        pltpu.VMEM((2 * s, ATT_PAIR_W), BF16),
                        pltpu.VMEM((s, ATT_PAIR_W), BF16),
                        pltpu.VMEM((s, 2 * ATT_PAIR_W), BF16),
                        pltpu.VMEM((n_pat, s, ATT_PAIR_W), F32),
                        pltpu.VMEM((n_pat, s, ATT_PAIR_W), F32),
                        pltpu.VMEM((2 * s, ATT_KW), F32),
                        pltpu.VMEM((2 * s, ATT_KW), BF16),
                        pltpu.VMEM((2 * s, ATT_PAIR_W), F32)],
        compiler_params=_cparams(("parallel", "arbitrary")),
        name="attention",
    )(x3, w_pairs, bias)


def _t5_bucket_np(rel):
    half = NUM_BUCKETS // 2
    max_exact = half // 2
    side = np.where(rel > 0, half, 0)
    n = np.abs(rel)
    large = max_exact + (np.log(np.maximum(n, 1).astype(np.float32) / max_exact)
                         / math.log(MAX_DISTANCE / max_exact) * (half - max_exact)).astype(np.int32)
    large = np.minimum(large, half - 1)
    return side + np.where(n < max_exact, n, large)


def _attention_bias(rel_bias, s):
    n_pat = len(DILATED_PATTERNS)
    bucket = np.zeros((n_pat, N_VARIANTS, ATT_BQ, ATT_KW), np.int32)
    inside = np.zeros((n_pat, N_VARIANTS, ATT_BQ, ATT_KW), bool)
    rows = np.arange(ATT_BQ)[:, None]
    cols = np.arange(ATT_KW)[None, :]
    for p, (_, dil) in enumerate(DILATED_PATTERNS):
        n = s // dil
        kw = min(ATT_KW, n)
        offsets = (0,) if n == ATT_BQ else (0, ATT_RADIUS, ATT_BQ)
        for v, off in enumerate(offsets):
            delta = cols - off - rows
            inside[p, v] = (np.abs(delta) <= ATT_RADIUS) & (cols < kw)
            bucket[p, v] = _t5_bucket_np(np.clip(delta, -ATT_RADIUS, ATT_RADIUS) * dil)
    bucket_j = jnp.asarray(bucket)[:, :, None]
    table = rel_bias.astype(F32)
    bias = jnp.zeros((n_pat, N_VARIANTS, ATT_HEADS, ATT_BQ, ATT_KW), F32)
    for bk in range(NUM_BUCKETS):
        bias = jnp.where(bucket_j == bk, table[bk][None, None, :, None, None], bias)
    return jnp.where(jnp.asarray(inside)[:, :, None], bias, NEG_INF)


def _layer_norm(z, g, b):
    mu = jnp.mean(z, axis=-1, keepdims=True)
    d = z - mu
    var = jnp.mean(d * d, axis=-1, keepdims=True)
    return d * lax.rsqrt(var + LN_EPS) * g + b


def _split_bf16(a):
    hi = a.astype(BF16)
    lo = (a - hi.astype(F32)).astype(BF16)
    return hi, lo


def _merge_kernel(ret_ref, att_ref, br_ref, ba_ref, x_ref,
                  wpr_ref, wpa_ref, wout_ref, lng_ref, lnb_ref, wrh_ref, wrl_ref, rb_ref,
                  x1r_ref):
    y_r = jnp.dot(ret_ref[...], wpr_ref[...], preferred_element_type=F32)
    y_a = jnp.dot(att_ref[...], wpa_ref[...], preferred_element_type=F32)
    merged = (jax.nn.sigmoid(br_ref[...].astype(F32)) * y_r
              + jax.nn.sigmoid(ba_ref[...].astype(F32)) * y_a)
    mix = jnp.dot(merged.astype(BF16), wout_ref[...], preferred_element_type=F32)
    x1 = _layer_norm(DEEPNORM_ALPHA * x_ref[...] + mix, lng_ref[...], lnb_ref[...])
    x1r_ref[:, 0:D_MODEL] = x1

    xh, xl = _split_bf16(x1)
    logits = (jnp.dot(xh, wrh_ref[...], preferred_element_type=F32)
              + jnp.dot(xh, wrl_ref[...], preferred_element_type=F32)
              + jnp.dot(xl, wrh_ref[...], preferred_element_type=F32)) + rb_ref[...]
    tm = logits.shape[0]
    lane = lax.broadcasted_iota(jnp.int32, (tm, LANES), 1)
    lane_f = lane.astype(F32)
    big = float(LANES)
    gmask = lane < N_GROUPS
    gl = jnp.where(gmask, logits, -jnp.inf)
    gmax = jnp.max(gl, axis=-1, keepdims=True)
    g_idx = jnp.min(jnp.where(gl == gmax, lane_f, big), axis=-1, keepdims=True)
    g_w = 1.0 / jnp.sum(jnp.where(gmask, jnp.exp(gl - gmax), 0.0), axis=-1, keepdims=True)
    in_group = jnp.floor((lane_f - N_GROUPS) * (1.0 / EXPERTS_PER_GROUP)) == g_idx
    emask = (lane >= N_GROUPS) & (lane < N_GROUPS + N_EXPERTS) & in_group
    el = jnp.where(emask, logits, -jnp.inf)
    v1 = jnp.max(el, axis=-1, keepdims=True)
    i1 = jnp.min(jnp.where(el == v1, lane_f, big), axis=-1, keepdims=True)
    el2 = jnp.where(lane_f == i1, -jnp.inf, el)
    v2 = jnp.max(el2, axis=-1, keepdims=True)
    i2 = jnp.min(jnp.where(el2 == v2, lane_f, big), axis=-1, keepdims=True)
    t = jnp.exp(v2 - v1)
    p1 = 1.0 / (1.0 + t)
    w1 = g_w * p1
    w2 = g_w * (t * p1)
    first_lo = i1 < i2
    e_lo = jnp.where(first_lo, i1, i2) - N_GROUPS
    e_hi = jnp.where(first_lo, i2, i1) - N_GROUPS
    a_loc = e_lo - EXPERTS_PER_GROUP * g_idx
    b_loc = e_hi - EXPERTS_PER_GROUP * g_idx
    pair = a_loc * (2 * EXPERTS_PER_GROUP - 1 - a_loc) * 0.5 + (b_loc - a_loc - 1.0)
    cls = g_idx * PAIRS_PER_GROUP + pair
    route = jnp.where(lane == ROUTE_W_LO, jnp.where(first_lo, w1, w2),
                      jnp.where(lane == ROUTE_W_HI, jnp.where(first_lo, w2, w1),
                                jnp.where(lane == ROUTE_CLS, cls, 0.0)))
    x1r_ref[:, D_MODEL:X1R_COLS] = route


def _merge(ret_act, attn, hcat, x2d, wpr, wpa, wout, lng, lnb, wrh, wrl, rb):
    t = x2d.shape[0]
    tm = min(MERGE_TM, t)
    row = lambda i: (i, 0)
    const = lambda i: (0, 0)
    in_specs = [
        pl.BlockSpec((tm, D_MODEL), row),
        pl.BlockSpec((tm, ATT_W), row),
        pl.BlockSpec((tm, D_MODEL), lambda i: (i, COL_BR // D_MODEL)),
        pl.BlockSpec((tm, D_MODEL), lambda i: (i, COL_BA // D_MODEL)),
        pl.BlockSpec((tm, D_MODEL), row),
        pl.BlockSpec((D_MODEL, D_MODEL), const),
        pl.BlockSpec((ATT_W, D_MODEL), const),
        pl.BlockSpec((D_MODEL, D_MODEL), const),
        pl.BlockSpec((1, D_MODEL), const),
        pl.BlockSpec((1, D_MODEL), const),
        pl.BlockSpec((D_MODEL, LANES), const),
        pl.BlockSpec((D_MODEL, LANES), const),
        pl.BlockSpec((1, LANES), const)]
    return pl.pallas_call(
        _merge_kernel,
        grid=(t // tm,),
        in_specs=in_specs,
        out_specs=pl.BlockSpec((tm, X1R_COLS), row),
        out_shape=jax.ShapeDtypeStruct((t, X1R_COLS), F32),
        compiler_params=_cparams(("parallel",)),
        name="merge_ln1_router",
    )(ret_act, attn, hcat, hcat, x2d, wpr, wpa, wout, lng, lnb, wrh, wrl, rb)


def _row_copies_start(idx_ref, n_rows, copy_of, sem):
    def start(g, u, r, priority=0):
        src, dst = copy_of(g, u, idx_ref[0, r])
        pltpu.make_async_copy(src, dst, sem).start(priority=priority)

    def group(g, carry):
        for u in range(SUBLANES):
            start(g, u, g * SUBLANES + u, priority=u % 2)
        return carry

    n_groups = n_rows // SUBLANES
    lax.fori_loop(0, n_groups, group, 0)

    def tail(r, carry):
        start(n_groups, r - n_groups * SUBLANES, r)
        return carry

    lax.fori_loop(n_groups * SUBLANES, n_rows, tail, 0)


def _rows_wait(buf, slot, n_rows, sem):
    m = buf.shape[1] * SUBLANES
    assert m & (m - 1) == 0
    while m >= 1:
        @pl.when((n_rows & m) != 0)
        def _(m=m):
            if m >= SUBLANES:
                view = buf.at[slot, pl.ds(0, m // SUBLANES)]
            else:
                view = buf.at[slot, 0, pl.ds(0, m)]
            pltpu.make_async_copy(view, view, sem.at[slot]).wait()
        m //= 2


def _swiglu(xb, wg_ref, wu_ref, wd_ref):
    gate = jnp.dot(xb, wg_ref[...], preferred_element_type=F32)
    up = jnp.dot(xb, wu_ref[...], preferred_element_type=F32)
    hid = (gate * jax.nn.sigmoid(gate) * up).astype(BF16)
    return jnp.dot(hid, wd_ref[...], preferred_element_type=F32)


def _moe_kernel(ea_ref, eb_ref, nv_ref, tok_ref, tok_next_ref, x_hbm,
                wga_ref, wua_ref, wda_ref, wgb_ref, wub_ref, wdb_ref, lng_ref, lnb_ref,
                out_hbm, xbuf, obuf, gsem, ssem):
    del ea_ref, eb_ref
    j = pl.program_id(0)
    nb = pl.num_programs(0)
    slot = lax.rem(j, 2)
    other = 1 - slot
    nv = nv_ref[j]

    def gather_start(idx_ref, n_rows, slot_):
        _row_copies_start(idx_ref, n_rows,
                          lambda g, u, row: (x_hbm.at[pl.ds(row, 1), :], xbuf.at[slot_, g, pl.ds(u, 1), :]),
                          gsem.at[slot_])

    @pl.when(j == 0)
    def _():
        xbuf[...] = jnp.zeros_like(xbuf)
        gather_start(tok_ref, nv, 0)

    @pl.when(j + 1 < nb)
    def _():
        gather_start(tok_next_ref, nv_ref[jnp.minimum(j + 1, nb - 1)], other)

    _rows_wait(xbuf, slot, nv, gsem)

    @pl.when(j >= 2)
    def _():
        _rows_wait(obuf, slot, nv_ref[jnp.maximum(j - 2, 0)], ssem)

    @pl.when(nv > 0)
    def _():
        xr = xbuf[slot].reshape(EXPERT_ROWS, X1R_COLS)
        x1 = xr[:, 0:D_MODEL]
        w_a = xr[:, D_MODEL + ROUTE_W_LO:D_MODEL + ROUTE_W_LO + 1]
        w_b = xr[:, D_MODEL + ROUTE_W_HI:D_MODEL + ROUTE_W_HI + 1]
        xb = x1.astype(BF16)
        ffn = w_a * _swiglu(xb, wga_ref, wua_ref, wda_ref) + w_b * _swiglu(xb, wgb_ref, wub_ref, wdb_ref)
        out = _layer_norm(DEEPNORM_ALPHA * x1 + ffn, lng_ref[...], lnb_ref[...])
        obuf[slot] = out.reshape(EXPERT_ROWS // SUBLANES, SUBLANES, D_MODEL)
        _row_copies_start(tok_ref, nv,
                          lambda g, u, row: (obuf.at[slot, g, pl.ds(u, 1), :], out_hbm.at[pl.ds(row, 1), :]),
                          ssem.at[slot])

    @pl.when(j == nb - 1)
    def _():
        @pl.when(nb >= 2)
        def _():
            _rows_wait(obuf, other, nv_ref[jnp.maximum(j - 1, 0)], ssem)
        _rows_wait(obuf, slot, nv, ssem)


def _moe(x1r, slot_tok, block_ea, block_eb, block_nv, wg, wu, wd, lng, lnb):
    t = x1r.shape[0]
    n_blocks = block_ea.shape[0]
    r = EXPERT_ROWS
    tok3 = slot_tok.reshape(n_blocks, 1, r)
    const = lambda j, ea, eb, nv: (0, 0)
    grid_spec = pltpu.PrefetchScalarGridSpec(
        num_scalar_prefetch=3,
        grid=(n_blocks,),
        in_specs=[
            pl.BlockSpec((None, 1, r), lambda j, ea, eb, nv: (j, 0, 0), memory_space=pltpu.SMEM),
            pl.BlockSpec((None, 1, r), lambda j, ea, eb, nv: (jnp.minimum(j + 1, n_blocks - 1), 0, 0),
                         memory_space=pltpu.SMEM),
            pl.BlockSpec(memory_space=pl.ANY),
            pl.BlockSpec((None, D_MODEL, D_EXPERT), lambda j, ea, eb, nv: (ea[j], 0, 0)),
            pl.BlockSpec((None, D_MODEL, D_EXPERT), lambda j, ea, eb, nv: (ea[j], 0, 0)),
            pl.BlockSpec((None, D_EXPERT, D_MODEL), lambda j, ea, eb, nv: (ea[j], 0, 0)),
            pl.BlockSpec((None, D_MODEL, D_EXPERT), lambda j, ea, eb, nv: (eb[j], 0, 0)),
            pl.BlockSpec((None, D_MODEL, D_EXPERT), lambda j, ea, eb, nv: (eb[j], 0, 0)),
            pl.BlockSpec((None, D_EXPERT, D_MODEL), lambda j, ea, eb, nv: (eb[j], 0, 0)),
            pl.BlockSpec((1, D_MODEL), const),
            pl.BlockSpec((1, D_MODEL), const),
        ],
        out_specs=pl.BlockSpec(memory_space=pl.ANY),
        scratch_shapes=[pltpu.VMEM((2, r // SUBLANES, SUBLANES, X1R_COLS), F32),
                        pltpu.VMEM((2, r // SUBLANES, SUBLANES, D_MODEL), F32),
                        pltpu.SemaphoreType.DMA((2,)), pltpu.SemaphoreType.DMA((2,))],
    )
    return pl.pallas_call(
        _moe_kernel,
        grid_spec=grid_spec,
        out_shape=jax.ShapeDtypeStruct((t, D_MODEL), F32),
        compiler_params=_cparams(("arbitrary",)),
        name="experts_combine_ln2",
    )(block_ea, block_eb, block_nv, tok3, tok3, x1r, wg, wu, wd, wg, wu, wd, lng, lnb)


def _pair_tables():
    ea, eb = [], []
    for g in range(N_GROUPS):
        for a in range(EXPERTS_PER_GROUP):
            for b in range(a + 1, EXPERTS_PER_GROUP):
                ea.append(g * EXPERTS_PER_GROUP + a)
                eb.append(g * EXPERTS_PER_GROUP + b)
    return np.asarray(ea, np.int32), np.asarray(eb, np.int32)


def _routing_plan(cls, n_tokens):
    r = EXPERT_ROWS
    assert n_tokens & (n_tokens - 1) == 0
    classes = jnp.arange(N_CLASSES, dtype=jnp.int32)
    counts = jnp.sum((cls[:, None] == classes[None, :]).astype(jnp.int32), axis=0)
    padded = (counts + r - 1) // r * r
    ends = jnp.cumsum(padded)
    pstart = ends - padded
    n_blocks = n_tokens // r + N_CLASSES
    token_keys = cls * (2 * n_tokens) + jnp.arange(n_tokens, dtype=jnp.int32)
    filler_idx = jnp.arange(r, dtype=jnp.int32)[None, :]
    filler_keys = jnp.where(filler_idx < (padded - counts)[:, None],
                            classes[:, None] * (2 * n_tokens) + n_tokens + filler_idx,
                            N_CLASSES * 2 * n_tokens)
    keys = jnp.sort(jnp.concatenate([token_keys, filler_keys.reshape(-1)]))
    slot_tok = jnp.where((keys & n_tokens) != 0, 0, keys & (n_tokens - 1))
    block_start = jnp.arange(n_blocks, dtype=jnp.int32) * r
    block_cls = jnp.minimum(jnp.sum(ends[None, :] <= block_start[:, None], axis=1), N_CLASSES - 1)
    onehot_b = (block_cls[:, None] == classes[None, :]).astype(jnp.int32)
    ea_tab, eb_tab = _pair_tables()
    block_ea = jnp.sum(onehot_b * jnp.asarray(ea_tab)[None, :], axis=1).astype(jnp.int32)
    block_eb = jnp.sum(onehot_b * jnp.asarray(eb_tab)[None, :], axis=1).astype(jnp.int32)
    used = jnp.sum(onehot_b * (pstart + counts)[None, :], axis=1)
    block_nv = jnp.clip(used - block_start, 0, r).astype(jnp.int32)
    return slot_tok, block_ea, block_eb, block_nv


def _rotary_tables(s):
    half = RET_DK // 2
    inv = ROPE_BASE ** (-jnp.arange(half, dtype=F32) / half)
    ang = jnp.arange(s, dtype=F32)[:, None] * inv[None, :]
    cos, sin = jnp.cos(ang), jnp.sin(ang)
    return jnp.concatenate([cos, cos], axis=1), jnp.concatenate([-sin, sin], axis=1)


def _cols(w, off, width):
    return lax.slice_in_dim(w, off, off + width, axis=1)


def _layer(x, w_in, dec_f, dec_b, gn_g, w_proj_ret, rel_bias, w_proj_attn, w_out, ln1_g, ln1_b,
           rw_g, rb_g, rw_e, rb_e, w_gate, w_up, w_down, ln2_g, ln2_b):
    b, s, d = x.shape
    t = b * s
    x2d = x.reshape(t, d)
    for window, dil in DILATED_PATTERNS:
        assert window // (2 * dil) == ATT_RADIUS and (s // dil) % ATT_BQ == 0

    head_cols = [_cols(w_in, off + h * width, width) for h in range(RET_HEADS)
                 for off, width in ((REF_QR, RET_DK), (REF_KR, RET_DK), (REF_VR, RET_DV), (REF_GR, RET_DV))]
    w_main = jnp.concatenate(head_cols + [_cols(w_in, REF_BR, D_MODEL), _cols(w_in, REF_BA, D_MODEL)],
                             axis=1).astype(BF16)
    hcat = _proj_in(x2d, w_main)
    hcat3 = hcat.reshape(b, s, MAIN_COLS)

    dec = jnp.stack([jax.nn.log_sigmoid(dec_f.astype(F32)), jax.nn.log_sigmoid(dec_b.astype(F32))], axis=0)
    cos_t, sin_t = _rotary_tables(s)
    ret_act = _retention(hcat3, dec, cos_t, sin_t, gn_g.reshape(1, -1).astype(F32)).reshape(t, -1)

    w_pairs = jnp.concatenate(
        [_cols(w_in, off + j * ATT_PAIR_W, ATT_PAIR_W) for j in range(N_PAIRS) for off in (REF_QA, REF_KA, REF_VA)],
        axis=1).astype(BF16)
    attn = _attention(x, w_pairs, _attention_bias(rel_bias, s)).reshape(t, ATT_W)

    wr = jnp.zeros((d, LANES), F32).at[:, :N_GROUPS].set(rw_g.astype(F32))
    wr = wr.at[:, N_GROUPS:N_GROUPS + N_EXPERTS].set(rw_e.astype(F32))
    rb = jnp.zeros((1, LANES), F32).at[0, :N_GROUPS].set(rb_g.astype(F32))
    rb = rb.at[0, N_GROUPS:N_GROUPS + N_EXPERTS].set(rb_e.astype(F32))
    wrh = wr.astype(BF16)
    wrl = (wr - wrh.astype(F32)).astype(BF16)
    x1r = _merge(ret_act, attn, hcat, x2d,
                 w_proj_ret.astype(BF16), w_proj_attn.astype(BF16), w_out.astype(BF16),
                 ln1_g.reshape(1, -1).astype(F32), ln1_b.reshape(1, -1).astype(F32), wrh, wrl, rb)

    cls = x1r[:, D_MODEL + ROUTE_CLS].astype(jnp.int32)
    slot_tok, block_ea, block_eb, block_nv = _routing_plan(cls, t)
    out = _moe(x1r, slot_tok, block_ea, block_eb, block_nv,
               w_gate.astype(BF16), w_up.astype(BF16), w_down.astype(BF16),
               ln2_g.reshape(1, -1).astype(F32), ln2_b.reshape(1, -1).astype(F32))
    return out.reshape(b, s, d)


def kernel(x, w_in, ret_decay_fwd, ret_decay_bwd, ret_gn_g, w_proj_ret, rel_bias, w_proj_attn, w_out,
           ln1_g, ln1_b, router_w_group, router_b_group, router_w_expert, router_b_expert,
           w_gate, w_up, w_down, ln2_g, ln2_b):
    for l in range(DEPTH):
        x = _layer(x, w_in[l], ret_decay_fwd[l], ret_decay_bwd[l], ret_gn_g[l], w_proj_ret[l], rel_bias,
                   w_proj_attn[l], w_out[l], ln1_g[l], ln1_b[l], router_w_group[l], router_b_group[l],
                   router_w_expert[l], router_b_expert[l], w_gate[l], w_up[l], w_down[l],
                   ln2_g[l], ln2_b[l])
    return x
```
